```python
import math
import jax, jax.numpy as jnp
from jax import lax
import numpy as np


D_MODEL = 1024
BATCH = 16
SEQ = 2048
DEPTH = 2

GRID_W = 64
CTX_LEN = 256
EPS = 1e-6

D_SSM = 256
SSM_GROUP = 16
N_SSM_GROUPS = D_SSM // SSM_GROUP
SSM_STATE = 64
DT_MIN = 1e-3
DT_MAX = 1e-1

D_POOL = 768
POOL_WINDOWS = (2, 4, 8, 16)
N_POOL_GROUPS = 4
POOL_GROUP = D_POOL // N_POOL_GROUPS

GATE_OFF = D_SSM + D_POOL
D_IN = D_SSM + D_POOL + 2 * D_MODEL

N_EXPERT_GROUPS = 4
EXPERTS_PER_GROUP = 4
N_EXPERTS = N_EXPERT_GROUPS * EXPERTS_PER_GROUP
D_EXPERT = 512
TOP_K_INNER = 2

kernel_name = 'hybrid_s5_pool_hmoe_dit_prefix'


def rmsnorm(x, g):
    xf = x.astype(jnp.float32)
    y = xf * lax.rsqrt(jnp.mean(xf * xf, axis=-1, keepdims=True) + EPS)
    return (y * g.astype(jnp.float32)).astype(x.dtype)


def modulate(x, g, shift, scale):
    return rmsnorm(x, g) * (1 + scale) + shift


def s5_discretize(lam_re, lam_im, log_dt, b_re, b_im):
    lr, li = lam_re.astype(jnp.float32), lam_im.astype(jnp.float32)
    dt = jnp.exp(log_dt.astype(jnp.float32))[:, None]
    zr, zi = lr * dt, li * dt
    mag = jnp.exp(zr)
    abar_r, abar_i = mag * jnp.cos(zi), mag * jnp.sin(zi)
    nr, ni = abar_r - 1.0, abar_i
    den = lr * lr + li * li
    fr = (nr * lr + ni * li) / den
    fi = (ni * lr - nr * li) / den
    br, bi = b_re.astype(jnp.float32), b_im.astype(jnp.float32)
    bbar_r = fr[..., None] * br - fi[..., None] * bi
    bbar_i = fr[..., None] * bi + fi[..., None] * br
    return zr, zi, abar_r, abar_i, bbar_r, bbar_i


def _linear_recurrence_combine(e1, e2):
    ar1, ai1, br1, bi1 = e1
    ar2, ai2, br2, bi2 = e2
    return (ar2 * ar1 - ai2 * ai1,
            ar2 * ai1 + ai2 * ar1,
            ar2 * br1 - ai2 * bi1 + br2,
            ar2 * bi1 + ai2 * br1 + bi2)


def s5_scan(u, disc, h0):
    zr, zi, abar_r, abar_i, bbar_r, bbar_i = disc
    T = u.shape[0]
    bu_r = jnp.einsum('tbgh,gnh->tbgn', u, bbar_r)
    bu_i = jnp.einsum('tbgh,gnh->tbgn', u, bbar_i)
    a_r = jnp.broadcast_to(abar_r, (T, 1) + abar_r.shape)
    a_i = jnp.broadcast_to(abar_i, (T, 1) + abar_i.shape)
    _, _, hr, hi = lax.associative_scan(_linear_recurrence_combine, (a_r, a_i, bu_r, bu_i), axis=0)
    if h0 is not None:
        steps = jnp.arange(1, T + 1, dtype=jnp.float32)[:, None, None, None]
        pm = jnp.exp(zr * steps)
        pr, pi = pm * jnp.cos(zi * steps), pm * jnp.sin(zi * steps)
        h0r, h0i = h0[0][None], h0[1][None]
        hr = hr + pr * h0r - pi * h0i
        hi = hi + pr * h0i + pi * h0r
    return hr, hi


def s5_readout(hr, hi, c_re, c_im):
    return (jnp.einsum('tbgn,ghn->tbgh', hr, c_re.astype(jnp.float32))
            - jnp.einsum('tbgn,ghn->tbgh', hi, c_im.astype(jnp.float32)))


def s5_direction(u_ctx, u_lat, lam_re, lam_im, log_dt, b_re, b_im, c_re, c_im, reverse, ctx_out):
    disc = s5_discretize(lam_re, lam_im, log_dt, b_re, b_im)
    if reverse:
        u_ctx, u_lat = u_ctx[::-1], u_lat[::-1]
    cr, ci = s5_scan(u_ctx, disc, None)
    lr, li = s5_scan(u_lat, disc, (cr[-1], ci[-1]))
    y_lat = s5_readout(lr, li, c_re, c_im)
    y_ctx = s5_readout(cr, ci, c_re, c_im) if ctx_out else None
    if reverse:
        y_lat = y_lat[::-1]
        y_ctx = y_ctx[::-1] if ctx_out else None
    return y_ctx, y_lat


def s5_branch(u_ctx, u_lat, lam_re, lam_im, log_dt, b_re, b_im, c_re, c_im,
              d_skip, w_glu, b_glu, ctx_out):
    def to_scan(u):
        B, T, _ = u.shape
        return u.astype(jnp.float32).transpose(1, 0, 2).reshape(T, B, N_SSM_GROUPS, SSM_GROUP)

    def finish(y, u):
        T, B = y.shape[0], y.shape[1]
        y = y.reshape(T, B, D_SSM).transpose(1, 0, 2) + d_skip.astype(jnp.float32) * u.astype(jnp.float32)
        z = jax.nn.gelu(y)
        gate = jax.nn.sigmoid(z @ w_glu.astype(jnp.float32) + b_glu.astype(jnp.float32))
        return (z * gate).astype(u.dtype)

    sc, sl = to_scan(u_ctx), to_scan(u_lat)
    yf_c, yf_l = s5_direction(sc, sl, lam_re[0], lam_im[0], log_dt[0], b_re[0], b_im[0],
                              c_re[0], c_im[0], False, ctx_out)
    yb_c, yb_l = s5_direction(sc, sl, lam_re[1], lam_im[1], log_dt[1], b_re[1], b_im[1],
                              c_re[1], c_im[1], True, ctx_out)
    y_lat = finish(yf_l + yb_l, u_lat)
    y_ctx = finish(yf_c + yb_c, u_ctx) if ctx_out else None
    return y_ctx, y_lat


def _window_bounds(n, w):
    lo, hi = -(w // 2), w - 1 - w // 2
    idx = jnp.arange(n)
    return jnp.clip(idx + lo, 0, n - 1), jnp.clip(idx + hi, 0, n - 1) + 1


def grid_pool(u, rows):
    B = u.shape[0]
    g = u.astype(jnp.float32).reshape(B, rows, GRID_W, N_POOL_GROUPS, POOL_GROUP)
    P = jnp.cumsum(jnp.cumsum(g, axis=1), axis=2)
    P = jnp.pad(P, ((0, 0), (1, 0), (1, 0), (0, 0), (0, 0)))
    outs = []
    for i, w in enumerate(POOL_WINDOWS):
        r0, r1 = _window_bounds(rows, w)
        c0, c1 = _window_bounds(GRID_W, w)
        Pi = P[:, :, :, i, :]

        def corner(ri, ci):
            return jnp.take(jnp.take(Pi, ri, axis=1), ci, axis=2)

        s = corner(r1, c1) - corner(r0, c1) - corner(r1, c0) + corner(r0, c0)
        cnt = ((r1 - r0)[:, None] * (c1 - c0)[None, :]).astype(jnp.float32)
        outs.append(s / cnt[None, :, :, None])
    return jnp.stack(outs, axis=3).reshape(B, rows * GRID_W, D_POOL)


def seq_pool(u):
    B, T, _ = u.shape
    g = u.astype(jnp.float32).reshape(B, T, N_POOL_GROUPS, POOL_GROUP)
    P = jnp.pad(jnp.cumsum(g, axis=1), ((0, 0), (1, 0), (0, 0), (0, 0)))
    outs = []
    for i, w in enumerate(POOL_WINDOWS):
        t0, t1 = _window_bounds(T, w)
        Pi = P[:, :, i, :]
        s = jnp.take(Pi, t1, axis=1) - jnp.take(Pi, t0, axis=1)
        outs.append(s / (t1 - t0).astype(jnp.float32)[None, :, None])
    return jnp.stack(outs, axis=2).reshape(B, T, D_POOL)


def pool_branch(u, pooled, w_pool, pool_scale):
    B, T, _ = u.shape
    d = (pooled - u.astype(jnp.float32)).reshape(B, T, N_POOL_GROUPS, POOL_GROUP)
    y = jnp.einsum('btgc,gce->btge', d, w_pool.astype(jnp.float32)).reshape(B, T, D_POOL)
    return (y * pool_scale.astype(jnp.float32)).astype(u.dtype)


def merge_branches(proj, y_a, y_b, w_br_a, w_br_b, w_out):
    g_a = jax.nn.sigmoid(proj[..., GATE_OFF:GATE_OFF + D_MODEL])
    g_b = jax.nn.sigmoid(proj[..., GATE_OFF + D_MODEL:])
    m = g_a * (y_a @ w_br_a) + g_b * (y_b @ w_br_b)
    return m @ w_out


def token_mixer(h_ctx, h_lat, p, ctx_out):
    rows = h_lat.shape[1] // GRID_W
    proj_lat = h_lat @ p['w_in']
    proj_ctx = h_ctx @ p['w_in'] if ctx_out else h_ctx @ p['w_in'][:, :D_SSM]
    ya_ctx, ya_lat = s5_branch(proj_ctx[..., :D_SSM], proj_lat[..., :D_SSM],
                               p['lam_re'], p['lam_im'], p['log_dt'], p['b_re'], p['b_im'],
                               p['c_re'], p['c_im'], p['d_skip'], p['w_glu'], p['b_glu'], ctx_out)
    ub_lat = proj_lat[..., D_SSM:GATE_OFF]
    yb_lat = pool_branch(ub_lat, grid_pool(ub_lat, rows), p['w_pool'], p['pool_scale'])
    out_lat = merge_branches(proj_lat, ya_lat, yb_lat, p['w_br_a'], p['w_br_b'], p['w_out'])
    if not ctx_out:
        return None, out_lat
    ub_ctx = proj_ctx[..., D_SSM:GATE_OFF]
    yb_ctx = pool_branch(ub_ctx, seq_pool(ub_ctx), p['w_pool'], p['pool_scale'])
    out_ctx = merge_branches(proj_ctx, ya_ctx, yb_ctx, p['w_br_a'], p['w_br_b'], p['w_out'])
    return out_ctx, out_lat


def hier_moe(h, w_r1, b_r1, w_r2, b_r2, w1, w3, w2):
    shp = h.shape
    hn = h.reshape(-1, shp[-1])
    hf = hn.astype(jnp.float32)
    logit1 = hf @ w_r1.astype(jnp.float32) + b_r1.astype(jnp.float32)
    p1 = jax.nn.softmax(logit1, axis=-1)
    _, grp = lax.top_k(logit1, 1)
    grp = grp[:, 0]
    p_grp = jnp.take_along_axis(p1, grp[:, None], axis=1)
    logit2 = jnp.einsum('nd,gde->nge', hf, w_r2.astype(jnp.float32)) + b_r2.astype(jnp.float32)
    logit2 = jnp.take_along_axis(logit2, grp[:, None, None], axis=1)[:, 0]
    top_v, top_i = lax.top_k(logit2, TOP_K_INNER)
    weights = p_grp * jax.nn.softmax(top_v, axis=-1)
    expert_id = grp[:, None] * EXPERTS_PER_GROUP + top_i
    combine = jnp.einsum('nke,nk->ne', jax.nn.one_hot(expert_id, N_EXPERTS, dtype=jnp.float32), weights)
    combine = combine.astype(hn.dtype)
    out = jnp.zeros_like(hn)
    for e in range(N_EXPERTS):
        he = jax.nn.silu(hn @ w1[e]) * (hn @ w3[e])
        out = out + combine[:, e:e + 1] * (he @ w2[e])
    return out.reshape(shp)


def setup_inputs(seed: int = 0) -> dict:
    key = jax.random.key(seed)
    ks = jax.random.split(key, 40)
    f32 = jnp.float32
    D, G, N, H = D_MODEL, N_SSM_GROUPS, SSM_STATE, SSM_GROUP

    def nrm(k, shape, scale):
        return jax.random.normal(k, shape, f32) * scale

    n_idx = jnp.arange(N, dtype=f32)
    lam_im = math.pi * n_idx + nrm(ks[8], (DEPTH, 2, G, N), 0.01)
    lam_re = -0.5 + nrm(ks[9], (DEPTH, 2, G, N), 0.01)
    log_dt = jax.random.uniform(ks[10], (DEPTH, 2, G), f32, math.log(DT_MIN), math.log(DT_MAX))
    return {
        'x': nrm(ks[0], (BATCH, SEQ, D), 1.0),
        'c': nrm(ks[1], (BATCH, D), 1.0),
        'ctx': nrm(ks[2], (BATCH, CTX_LEN, D), 1.0),
        'c_ctx': nrm(ks[3], (D,), 1.0),
        'norm1_g': 1.0 + nrm(ks[4], (DEPTH, D), 0.02),
        'norm2_g': 1.0 + nrm(ks[5], (DEPTH, D), 0.02),
        'w_mod': nrm(ks[6], (DEPTH, D, 6 * D), D ** -0.5),
        'b_mod': nrm(ks[7], (DEPTH, 6 * D), 0.01),
        'w_in': nrm(ks[11], (DEPTH, D, D_IN), D ** -0.5),
        'lam_re': lam_re,
        'lam_im': lam_im,
        'log_dt': log_dt,
        'b_re': nrm(ks[12], (DEPTH, 2, G, N, H), (2 * H) ** -0.5),
        'b_im': nrm(ks[13], (DEPTH, 2, G, N, H), (2 * H) ** -0.5),
        'c_re': nrm(ks[14], (DEPTH, 2, G, H, N), N ** -0.5),
        'c_im': nrm(ks[15], (DEPTH, 2, G, H, N), N ** -0.5),
        'd_skip': nrm(ks[16], (DEPTH, D_SSM), 1.0),
        'w_glu': nrm(ks[17], (DEPTH, D_SSM, D_SSM), D_SSM ** -0.5),
        'b_glu': nrm(ks[18], (DEPTH, D_SSM), 0.01),
        'w_pool': nrm(ks[19], (DEPTH, N_POOL_GROUPS, POOL_GROUP, POOL_GROUP), POOL_GROUP ** -0.5),
        'pool_scale': 1.0 + nrm(ks[20], (DEPTH, D_POOL), 0.02),
        'w_br_a': nrm(ks[21], (DEPTH, D_SSM, D), D_SSM ** -0.5),
        'w_br_b': nrm(ks[22], (DEPTH, D_POOL, D), D_POOL ** -0.5),
        'w_out': nrm(ks[23], (DEPTH, D, D), D ** -0.5),
        'w_r1': nrm(ks[24], (DEPTH, D, N_EXPERT_GROUPS), D ** -0.5),
        'b_r1': nrm(ks[25], (DEPTH, N_EXPERT_GROUPS), 0.01),
        'w_r2': nrm(ks[26], (DEPTH, N_EXPERT_GROUPS, D, EXPERTS_PER_GROUP), D ** -0.5),
        'b_r2': nrm(ks[27], (DEPTH, N_EXPERT_GROUPS, EXPERTS_PER_GROUP), 0.01),
        'w1': nrm(ks[28], (DEPTH, N_EXPERTS, D, D_EXPERT), D ** -0.5),
        'w3': nrm(ks[29], (DEPTH, N_EXPERTS, D, D_EXPERT), D ** -0.5),
        'w2': nrm(ks[30], (DEPTH, N_EXPERTS, D_EXPERT, D), D_EXPERT ** -0.5),
        'final_g': 1.0 + nrm(ks[31], (D,), 0.02),
    }


def reference(x, c, ctx, c_ctx, norm1_g, norm2_g, w_mod, b_mod, w_in, lam_re, lam_im, log_dt,
              b_re, b_im, c_re, c_im, d_skip, w_glu, b_glu, w_pool, pool_scale, w_br_a, w_br_b,
              w_out, w_r1, b_r1, w_r2, b_r2, w1, w3, w2, final_g):
    c_act = jax.nn.silu(c)
    c_ctx_act = jax.nn.silu(c_ctx)
    for l in range(DEPTH):
        last = l == DEPTH - 1
        p = {'w_in': w_in[l], 'lam_re': lam_re[l], 'lam_im': lam_im[l], 'log_dt': log_dt[l],
             'b_re': b_re[l], 'b_im': b_im[l], 'c_re': c_re[l], 'c_im': c_im[l],
             'd_skip': d_skip[l], 'w_glu': w_glu[l], 'b_glu': b_glu[l], 'w_pool': w_pool[l],
             'pool_scale': pool_scale[l], 'w_br_a': w_br_a[l], 'w_br_b': w_br_b[l], 'w_out': w_out[l]}
        mod = (c_act @ w_mod[l] + b_mod[l])[:, None, :]
        sh1, sc1, g1, sh2, sc2, g2 = jnp.split(mod, 6, axis=-1)
        mod_c = c_ctx_act @ w_mod[l] + b_mod[l]
        csh1, csc1, cg1, csh2, csc2, cg2 = jnp.split(mod_c, 6, axis=-1)

        h_lat = modulate(x, norm1_g[l], sh1, sc1)
        h_ctx = modulate(ctx, norm1_g[l], csh1, csc1)
        mix_ctx, mix_lat = token_mixer(h_ctx, h_lat, p, not last)
        x = x + g1 * mix_lat

        h2 = modulate(x, norm2_g[l], sh2, sc2)
        if last:
            x = x + g2 * hier_moe(h2, w_r1[l], b_r1[l], w_r2[l], b_r2[l], w1[l], w3[l], w2[l])
        else:
            ctx = ctx + cg1 * mix_ctx
            h2c = modulate(ctx, norm2_g[l], csh2, csc2)
            L = x.shape[1]
            tok = jnp.concatenate([h2c, h2], axis=1)
            f = hier_moe(tok, w_r1[l], b_r1[l], w_r2[l], b_r2[l], w1[l], w3[l], w2[l])
            ctx = ctx + cg2 * f[:, :ctx.shape[1]]
            x = x + g2 * f[:, ctx.shape[1]:ctx.shape[1] + L]
    return rmsnorm(x, final_g)
```

```python
import functools
import math

import numpy as np
import jax
import jax.numpy as jnp
from jax import lax
from jax.experimental import pallas as pl
from jax.experimental.pallas import tpu as pltpu

F32 = jnp.float32
BF16 = jnp.bfloat16

D_MODEL = 1024
DEPTH = 2
GRID_W = 64
CTX_LEN = 256
EPS = 1e-6

D_SSM = 256
SSM_GROUP = 16
N_SSM_GROUPS = 16
SSM_STATE = 64
S5_CHUNK = 16

D_POOL = 768
POOL_WINDOWS = (2, 4, 8, 16)
POOL_GROUP = 192
POOL_SEG = (0, 128, 384, 512)
GATE_OFF = D_SSM + D_POOL
D_IN = D_SSM + D_POOL + 2 * D_MODEL

N_EXPERT_GROUPS = 4
EXPERTS_PER_GROUP = 4
N_EXPERTS = 16
D_EXPERT = 512
N_PAIRS = 6
N_CLASSES = N_EXPERT_GROUPS * N_PAIRS
ROUTE_LANES = 128

TOKEN_TILE = 512
POOL_BLOCK = 256
EXPERT_TILE = 256
GATHER_TILE = 256
V7X_VMEM_LIMIT = 48 * 1024 * 1024


def _cparams(*sem):
    return pltpu.CompilerParams(dimension_semantics=sem, vmem_limit_bytes=V7X_VMEM_LIMIT)


def _split_bf16(v):
    hi = v.astype(BF16)
    lo = (v - hi.astype(F32)).astype(BF16)
    return hi, lo


def _mod_kernel(c_ref, w_ref, b_ref, o_ref):
    c = c_ref[...]
    a = c * jax.nn.sigmoid(c)
    a_hi, a_lo = _split_bf16(a)
    w_hi, w_lo = _split_bf16(w_ref[...])
    acc = jnp.dot(a_hi, w_hi, preferred_element_type=F32)
    acc += jnp.dot(a_lo, w_hi, preferred_element_type=F32)
    acc += jnp.dot(a_hi, w_lo, preferred_element_type=F32)
    o_ref[...] = acc + b_ref[...]


def _modulation(cc, w_mod, b_mod):
    rows = cc.shape[0]
    nblk = 4
    cols = 6 * D_MODEL // nblk
    return pl.pallas_call(
        _mod_kernel,
        out_shape=jax.ShapeDtypeStruct((DEPTH, rows, 6 * D_MODEL), F32),
        grid=(DEPTH, nblk),
        in_specs=[
            pl.BlockSpec((rows, D_MODEL), lambda l, j: (0, 0)),
            pl.BlockSpec((None, D_MODEL, cols), lambda l, j: (l, 0, j)),
            pl.BlockSpec((None, 1, cols), lambda l, j: (l, 0, j)),
        ],
        out_specs=pl.BlockSpec((None, rows, cols), lambda l, j: (l, 0, j)),
        compiler_params=_cparams("arbitrary", "arbitrary"),
        name="modulation",
    )(cc, w_mod, b_mod.reshape(DEPTH, 1, 6 * D_MODEL))


def _mod_row(i, n_ctx_tiles, tiles_per_batch, ctx_row):
    return jnp.where(i < n_ctx_tiles, ctx_row, (i - n_ctx_tiles) // tiles_per_batch)


def _pick(ctx_ref, lat_ref, tile, n_ctx_tiles):
    return jnp.where(tile < n_ctx_tiles, ctx_ref[...], lat_ref[...])


def _src_specs(cols, n_ctx_tiles, tile_off, lat_off):
    tm = TOKEN_TILE
    return [pl.BlockSpec((tm, cols), lambda i: (jnp.minimum(i + tile_off, n_ctx_tiles - 1), 0)),
            pl.BlockSpec((tm, cols),
                         lambda i: (jnp.maximum(i + tile_off - n_ctx_tiles, 0) + lat_off, 0))]


def _inproj_kernel(xc_ref, xl_ref, mod_ref, g_ref, w_ref, ua_ref, ub_ref, gate_ref, h_scr, *,
                   n_ctx_tiles):
    x = _pick(xc_ref, xl_ref, pl.program_id(0), n_ctx_tiles)
    y = x * lax.rsqrt(jnp.mean(x * x, axis=-1, keepdims=True) + EPS) * g_ref[...]
    shift = mod_ref[:, 0:D_MODEL]
    scale = mod_ref[:, D_MODEL:2 * D_MODEL]
    h_scr[...] = (y * (1.0 + scale) + shift).astype(BF16)
    h = h_scr[...]
    ua_ref[...] = jnp.dot(h, w_ref[:, 0:D_SSM], preferred_element_type=F32)
    ub_ref[...] = jnp.dot(h, w_ref[:, D_SSM:GATE_OFF], preferred_element_type=F32)
    for k in range(2):
        lo = GATE_OFF + k * D_MODEL
        g = jnp.dot(h, w_ref[:, lo:lo + D_MODEL], preferred_element_type=F32)
        gate_ref[:, k * D_MODEL:(k + 1) * D_MODEL] = jax.nn.sigmoid(g).astype(BF16)


def _inproj(x_ctx, x_lat, lat_off, mod4, norm_g, w_in_bf, layer, n_ctx_tiles, n_lat_tiles,
            tiles_per_batch, ctx_row):
    tm = TOKEN_TILE
    n_all = (n_ctx_tiles + n_lat_tiles) * tm
    row = functools.partial(_mod_row, n_ctx_tiles=n_ctx_tiles, tiles_per_batch=tiles_per_batch,
                            ctx_row=ctx_row)
    return pl.pallas_call(
        functools.partial(_inproj_kernel, n_ctx_tiles=n_ctx_tiles),
        out_shape=(jax.ShapeDtypeStruct((n_all, D_SSM), F32),
                   jax.ShapeDtypeStruct((n_all, D_POOL), F32),
                   jax.ShapeDtypeStruct((n_all, 2 * D_MODEL), BF16)),
        grid=(n_all // tm,),
        in_specs=_src_specs(D_MODEL, n_ctx_tiles, 0, lat_off) + [
            pl.BlockSpec((None, None, 1, 6 * D_MODEL), lambda i: (layer, row(i), 0, 0)),
            pl.BlockSpec((None, 1, D_MODEL), lambda i: (layer, 0, 0)),
            pl.BlockSpec((None, D_MODEL, D_IN), lambda i: (layer, 0, 0)),
        ],
        out_specs=(pl.BlockSpec((tm, D_SSM), lambda i: (i, 0)),
                   pl.BlockSpec((tm, D_POOL), lambda i: (i, 0)),
                   pl.BlockSpec((tm, 2 * D_MODEL), lambda i: (i, 0))),
        scratch_shapes=[pltpu.VMEM((tm, D_MODEL), BF16)],
        compiler_params=_cparams("arbitrary"),
        name="inproj",
    )(x_ctx, x_lat, mod4, norm_g, w_in_bf)


def _s5_matrices(lam_re, lam_im, log_dt, b_re, b_im, c_re, c_im):
    L, G, N, H = S5_CHUNK, N_SSM_GROUPS, SSM_STATE, SSM_GROUP
    lr, li = lam_re.astype(F32), lam_im.astype(F32)
    dt = jnp.exp(log_dt.astype(F32))[..., None]
    zr, zi = lr * dt, li * dt
    k = jnp.arange(L + 1, dtype=F32)[:, None, None, None]
    pm = jnp.exp(zr[None] * k)
    pr, pi = pm * jnp.cos(zi[None] * k), pm * jnp.sin(zi[None] * k)
    nr, ni = pr[1] - 1.0, pi[1]
    den = lr * lr + li * li
    fr = (nr * lr + ni * li) / den
    fi = (ni * lr - nr * li) / den
    br, bi = b_re.astype(F32), b_im.astype(F32)
    bbr = fr[..., None] * br - fi[..., None] * bi
    bbi = fr[..., None] * bi + fi[..., None] * br
    cr, ci = c_re.astype(F32), c_im.astype(F32)
    cpr = cr[None] * pr[:, :, :, None, :] - ci[None] * pi[:, :, :, None, :]
    cpi = cr[None] * pi[:, :, :, None, :] + ci[None] * pr[:, :, :, None, :]
    kern = (jnp.einsum('kdgan,dgnh->kdgah', cpr, bbr)
            - jnp.einsum('kdgan,dgnh->kdgah', cpi, bbi))
    s_idx = np.arange(L)[:, None]
    t_idx = np.arange(L)[None, :]
    lag_f = np.clip(t_idx - s_idx, 0, L - 1)
    lag_b = np.clip(s_idx - t_idx, 0, L - 1)
    mf = jnp.where((t_idx >= s_idx)[:, :, None, None, None], kern[:, 0][lag_f], 0.0)
    mb = jnp.where((s_idx >= t_idx)[:, :, None, None, None], kern[:, 1][lag_b], 0.0)
    m = (mf + mb).transpose(2, 0, 4, 1, 3).reshape(G, L * H, L * H)
    pf_r, pf_i = pr[L - 1 - np.arange(L), 0], pi[L - 1 - np.arange(L), 0]
    pb_r, pb_i = pr[np.arange(L), 1], pi[np.arange(L), 1]

    def inject(p_r, p_i, d):
        wr = p_r[..., None] * bbr[d][None] - p_i[..., None] * bbi[d][None]
        wi = p_r[..., None] * bbi[d][None] + p_i[..., None] * bbr[d][None]
        w = jnp.concatenate([wr, wi], axis=2)
        return w.transpose(1, 0, 3, 2).reshape(G, L * H, 2 * N)

    mcat = jnp.concatenate([m, inject(pf_r, pf_i, 0), inject(pb_r, pb_i, 1)], axis=2)
    def readout(powers, d):
        wr = cpr[powers, d]
        wi = -cpi[powers, d]
        w = jnp.concatenate([wr, wi], axis=3)
        return w.transpose(1, 3, 0, 2).reshape(G, 2 * N, L * H)

    wy = jnp.concatenate([readout(np.arange(L) + 1, 0), readout(L - np.arange(L), 1)], axis=1)
    a_r, a_i = pr[L], pi[L]
    zero = jnp.zeros_like(a_r[0])
    rows = []
    for d in range(2):
        rows.append(jnp.concatenate([a_r[d], a_r[d]], axis=-1))
        rows.append(jnp.concatenate([-a_i[d], a_i[d]], axis=-1))
    rows += [jnp.concatenate([zero, zero], axis=-1)] * 4
    coef = jnp.stack(rows, axis=1)
    return mcat.astype(BF16), wy.astype(BF16), coef


def _s5_kernel(u_ref, mcat_ref, wy_ref, coef_ref, y_ref, s_scr, h_scr, *, batch, n_chunks,
               n_ctx_chunks):
    n2 = 2 * SSM_STATE
    rb = S5_CHUNK * batch
    n_rb = n_chunks // S5_CHUNK

    def inject(i, c):
        r0 = pl.multiple_of(i * rb, rb)
        u = u_ref[pl.ds(r0, rb), :].astype(BF16)
        y_ref[pl.ds(r0, rb), :] = jnp.dot(u, mcat_ref[:, 0:D_SSM], preferred_element_type=F32)
        s_scr[pl.ds(r0, rb), :] = jnp.dot(u, mcat_ref[:, D_SSM:2 * D_SSM],
                                          preferred_element_type=F32)
        return c

    lax.fori_loop(0, n_rb, inject, 0)
    a1f, a2f = coef_ref[0:1, :], coef_ref[1:2, :]
    a1b, a2b = coef_ref[2:3, :], coef_ref[3:4, :]

    def step(k, carry):
        hf, hb = carry
        cf = k
        cb = jnp.where(k < n_ctx_chunks, n_ctx_chunks - 1 - k, n_chunks - 1 + n_ctx_chunks - k)
        rf = pl.multiple_of(cf * batch, batch)
        rb = pl.multiple_of(cb * batch, batch)
        h_scr[pl.ds(rf, batch), 0:n2] = hf
        h_scr[pl.ds(rb, batch), n2:2 * n2] = hb
        hf = a1f * hf + a2f * pltpu.roll(hf, SSM_STATE, 1) + s_scr[pl.ds(rf, batch), 0:n2]
        hb = a1b * hb + a2b * pltpu.roll(hb, SSM_STATE, 1) + s_scr[pl.ds(rb, batch), n2:2 * n2]
        return hf, hb

    zero = jnp.zeros((batch, n2), F32)
    lax.fori_loop(0, n_chunks, step, (zero, zero))

    def readout(i, c):
        r0 = pl.multiple_of(i * rb, rb)
        y_ref[pl.ds(r0, rb), :] += jnp.dot(h_scr[pl.ds(r0, rb), :].astype(BF16), wy_ref[...],
                                           preferred_element_type=F32)
        return c

    lax.fori_loop(0, n_rb, readout, 0)


def _s5(u_t, mcat, wy, coef, batch, n_chunks, n_ctx_chunks):
    G = N_SSM_GROUPS
    rows = u_t.shape[1]
    kern = functools.partial(_s5_kernel, batch=batch, n_chunks=n_chunks, n_ctx_chunks=n_ctx_chunks)
    return pl.pallas_call(
        kern,
        out_shape=jax.ShapeDtypeStruct((G, rows, D_SSM), F32),
        grid=(G,),
        in_specs=[
            pl.BlockSpec((None, rows, D_SSM), lambda g: (g, 0, 0)),
            pl.BlockSpec((None, D_SSM, 2 * D_SSM), lambda g: (g, 0, 0)),
            pl.BlockSpec((None, D_SSM, D_SSM), lambda g: (g, 0, 0)),
            pl.BlockSpec((None, 8, 2 * SSM_STATE), lambda g: (g, 0, 0)),
        ],
        out_specs=pl.BlockSpec((None, rows, D_SSM), lambda g: (g, 0, 0)),
        scratch_shapes=[pltpu.VMEM((rows, D_SSM), F32), pltpu.VMEM((rows, D_SSM), F32)],
        compiler_params=_cparams("arbitrary"),
        name="s5",
    )(u_t, mcat, wy, coef)


def _to_chunks(ua, batch):
    L, G, H = S5_CHUNK, N_SSM_GROUPS, SSM_GROUP
    n_ctx = batch * CTX_LEN
    uc = ua[:n_ctx].reshape(batch, CTX_LEN // L, L, G, H)
    ul = ua[n_ctx:].reshape(batch, -1, L, G, H)
    u = jnp.concatenate([uc, ul], axis=1)
    n_chunks = u.shape[1]
    return u.transpose(3, 1, 0, 2, 4).reshape(G, n_chunks * batch, L * H), n_chunks


def _from_chunks(y_t, batch, n_chunks):
    L, G, H = S5_CHUNK, N_SSM_GROUPS, SSM_GROUP
    y = y_t.reshape(G, n_chunks, batch, L, H).transpose(2, 1, 3, 0, 4)
    cc = CTX_LEN // L
    yc = y[:, :cc].reshape(batch * CTX_LEN, G * H)
    yl = y[:, cc:].reshape(-1, G * H)
    return jnp.concatenate([yc, yl], axis=0)


def _window(w):
    return -(w // 2), w - 1 - w // 2


def _pool_constants(rows, width):
    rpb = POOL_BLOCK // width
    pm = np.zeros((4, POOL_BLOCK, POOL_BLOCK), np.float32)
    inv = np.zeros((rows * width, 4), np.float32)
    col = np.arange(width)
    row = np.arange(rows)
    for i, w in enumerate(POOL_WINDOWS):
        lo, hi = _window(w)
        c0, c1 = np.clip(col + lo, 0, width - 1), np.clip(col + hi, 0, width - 1)
        r0, r1 = np.clip(row + lo, 0, rows - 1), np.clip(row + hi, 0, rows - 1)
        band = ((col[None, :] >= c0[:, None]) & (col[None, :] <= c1[:, None])).astype(np.float32)
        for r in range(rpb):
            pm[i, r * width:(r + 1) * width, r * width:(r + 1) * width] = band
        cnt = (r1 - r0 + 1)[:, None] * (c1 - c0 + 1)[None, :]
        inv[:, i] = (1.0 / cnt).reshape(-1)
    return jnp.asarray(pm, BF16), jnp.asarray(inv, F32)


def _pool_kernel(u_ref, pm_ref, inv_ref, wp_ref, ps_ref, o_ref, cs_scr, *, rows, width):
    n_tok = rows * width
    n_blk = n_tok // POOL_BLOCK
    pad = 8 * width if rows > 1 else 0
    if rows > 1:
        zeros = jnp.zeros((pad, 2 * 128), F32)
        for i in range(4):
            cs_scr[i, 0:pad, :] = zeros
            cs_scr[i, pad + n_tok:pad + n_tok + pad, :] = zeros
    for b in range(n_blk):
        t0 = b * POOL_BLOCK
        for i in range(4):
            seg = POOL_SEG[i]
            xb = u_ref[t0:t0 + POOL_BLOCK, seg:seg + 256].astype(BF16)
            cs_scr[i, pad + t0:pad + t0 + POOL_BLOCK, :] = jnp.dot(
                pm_ref[i], xb, preferred_element_type=F32)
    lane = lax.broadcasted_iota(jnp.int32, (POOL_BLOCK, 128), 1)
    low_half = lane < 64
    for b in range(n_blk):
        t0 = b * POOL_BLOCK
        win = []
        for i, w in enumerate(POOL_WINDOWS):
            lo, hi = _window(w) if rows > 1 else (0, 0)
            acc = None
            for k in range(lo, hi + 1):
                base = pad + t0 + k * width
                piece = cs_scr[i, base:base + POOL_BLOCK, :]
                acc = piece if acc is None else acc + piece
            win.append(acc * inv_ref[t0:t0 + POOL_BLOCK, i:i + 1])
        x = u_ref[t0:t0 + POOL_BLOCK, :]
        pooled = jnp.concatenate([
            win[0][:, 0:128],
            jnp.where(low_half, win[0][:, 128:256], win[1][:, 0:128]),
            win[1][:, 128:256],
            win[2][:, 0:128],
            jnp.where(low_half, win[2][:, 128:256], win[3][:, 0:128]),
            win[3][:, 128:256]], axis=1)
        d = (pooled - x).astype(BF16)
        y0 = jnp.dot(d[:, 0:384], wp_ref[0], preferred_element_type=F32)
        y1 = jnp.dot(d[:, 384:768], wp_ref[1], preferred_element_type=F32)
        y = jnp.concatenate([y0, y1], axis=1) * ps_ref[...]
        o_ref[t0:t0 + POOL_BLOCK, :] = y.astype(BF16)


def _pool(ub, wp2, pscale, layer, rows, width, n_items, in_off):
    n_tok = rows * width
    pad = 8 * width if rows > 1 else 0
    pm, inv = _pool_constants(rows, width)
    kern = functools.partial(_pool_kernel, rows=rows, width=width)
    in_specs = [
        pl.BlockSpec((n_tok, D_POOL), lambda b: (b + in_off, 0)),
        pl.BlockSpec((4, POOL_BLOCK, POOL_BLOCK), lambda b: (0, 0, 0)),
        pl.BlockSpec((n_tok, 4), lambda b: (0, 0)),
        pl.BlockSpec((None, 2, 384, 384), lambda b: (layer, 0, 0, 0)),
        pl.BlockSpec((None, 1, D_POOL), lambda b: (layer, 0, 0)),
    ]
    return pl.pallas_call(
        kern,
        out_shape=jax.ShapeDtypeStruct((n_items * n_tok, D_POOL), BF16),
        grid=(n_items,),
        in_specs=in_specs,
        out_specs=pl.BlockSpec((n_tok, D_POOL), lambda b: (b, 0)),
        scratch_shapes=[pltpu.VMEM((4, n_tok + 2 * pad, 256), F32)],
        compiler_params=_cparams("arbitrary"),
        name="pool",
    )(ub, pm, inv, wp2, pscale)


def _first_argmax(rows_):
    best, idx = rows_[0], jnp.zeros_like(rows_[0], dtype=jnp.int32)
    for k in range(1, len(rows_)):
        take = rows_[k] > best
        idx = jnp.where(take, k, idx)
        best = jnp.where(take, rows_[k], best)
    return best, idx


def _route(lt):
    g_rows = [lt[k:k + 1, :] for k in range(N_EXPERT_GROUPS)]
    best, grp = _first_argmax(g_rows)
    denom = sum(jnp.exp(r - best) for r in g_rows)
    p_grp = 1.0 / denom
    inner = []
    for e in range(EXPERTS_PER_GROUP):
        acc = jnp.zeros_like(best)
        for g in range(N_EXPERT_GROUPS):
            r = 4 + 4 * g + e
            acc = jnp.where(grp == g, lt[r:r + 1, :], acc)
        inner.append(acc)
    v1, i1 = _first_argmax(inner)
    masked = [jnp.where(i1 == e, -jnp.inf, inner[e]) for e in range(EXPERTS_PER_GROUP)]
    v2, i2 = _first_argmax(masked)
    e21 = jnp.exp(v2 - v1)
    w1 = p_grp / (1.0 + e21)
    w2 = p_grp * e21 / (1.0 + e21)
    first_low = i1 < i2
    lo = jnp.where(first_low, i1, i2)
    hi = jnp.where(first_low, i2, i1)
    w_lo = jnp.where(first_low, w1, w2)
    w_hi = jnp.where(first_low, w2, w1)
    off = jnp.where(lo == 0, 0, jnp.where(lo == 1, 3, 5))
    cls = N_PAIRS * grp + off + hi - lo - 1
    return cls.astype(F32), w_lo, w_hi


def _merge_kernel(xc_ref, xl_ref, ya_ref, ua_ref, ybc_ref, ybl_ref, gate_ref, mod_ref, g2_ref,
                  dskip_ref, wglu_ref, bglu_ref, wbra_ref, wbrb_ref, wout_ref, wrh_ref, wrl_ref,
                  br_ref, x1_ref, h2_ref, route_ref, *, n_ctx_tiles, tile_off):
    D = D_MODEL
    tile = pl.program_id(0) + tile_off
    x_in = _pick(xc_ref, xl_ref, tile, n_ctx_tiles)
    yb = _pick(ybc_ref, ybl_ref, tile, n_ctx_tiles)
    y = ya_ref[...] + dskip_ref[...] * ua_ref[...]
    z = jax.nn.gelu(y)
    glu = jax.nn.sigmoid(jnp.dot(z.astype(BF16), wglu_ref[...], preferred_element_type=F32)
                         + bglu_ref[...])
    ya = (z * glu).astype(BF16)
    m = gate_ref[:, 0:D].astype(F32) * jnp.dot(ya, wbra_ref[...], preferred_element_type=F32)
    m += gate_ref[:, D:2 * D].astype(F32) * jnp.dot(yb, wbrb_ref[...],
                                                    preferred_element_type=F32)
    out = jnp.dot(m.astype(BF16), wout_ref[...], preferred_element_type=F32)
    x1 = x_in + mod_ref[:, 2 * D:3 * D] * out
    x1_ref[...] = x1
    yn = x1 * lax.rsqrt(jnp.mean(x1 * x1, axis=-1, keepdims=True) + EPS) * g2_ref[...]
    h2 = yn * (1.0 + mod_ref[:, 4 * D:5 * D]) + mod_ref[:, 3 * D:4 * D]
    h2_ref[...] = h2
    h_hi, h_lo = _split_bf16(h2)
    logits = jnp.dot(h_hi, wrh_ref[...], preferred_element_type=F32)
    logits += jnp.dot(h_lo, wrh_ref[...], preferred_element_type=F32)
    logits += jnp.dot(h_hi, wrl_ref[...], preferred_element_type=F32)
    logits += br_ref[...]
    cls, w_lo, w_hi = _route(logits.T)
    zero = jnp.zeros_like(cls)
    route_ref[...] = jnp.concatenate([cls, w_lo, w_hi, zero, zero, zero, zero, zero], axis=0)


def _merge(x_ctx, x_lat, lat_off, ya, ua, yb_ctx, yb_lat, gates, mod4, norm2_g, p, layer, tile_off,
           n_tiles, n_ctx_tiles, tiles_per_batch, ctx_row):
    tm = TOKEN_TILE
    row = functools.partial(_mod_row, n_ctx_tiles=n_ctx_tiles, tiles_per_batch=tiles_per_batch,
                            ctx_row=ctx_row)
    tok = lambda i: (i + tile_off, 0)
    lay3 = lambda i: (layer, 0, 0)
    n_out = n_tiles * tm
    return pl.pallas_call(
        functools.partial(_merge_kernel, n_ctx_tiles=n_ctx_tiles, tile_off=tile_off),
        out_shape=(jax.ShapeDtypeStruct((n_out, D_MODEL), F32),
                   jax.ShapeDtypeStruct((n_out, D_MODEL), F32),
                   jax.ShapeDtypeStruct((8, n_out), F32)),
        grid=(n_tiles,),
        in_specs=_src_specs(D_MODEL, n_ctx_tiles, tile_off, lat_off) + [
            pl.BlockSpec((tm, D_SSM), tok),
            pl.BlockSpec((tm, D_SSM), tok)] + _src_specs(D_POOL, n_ctx_tiles, tile_off, 0) + [
            pl.BlockSpec((tm, 2 * D_MODEL), tok),
            pl.BlockSpec((None, None, 1, 6 * D_MODEL), lambda i: (layer, row(i + tile_off), 0, 0)),
            pl.BlockSpec((None, 1, D_MODEL), lay3),
            pl.BlockSpec((None, 1, D_SSM), lay3),
            pl.BlockSpec((None, D_SSM, D_SSM), lay3),
            pl.BlockSpec((None, 1, D_SSM), lay3),
            pl.BlockSpec((None, D_SSM, D_MODEL), lay3),
            pl.BlockSpec((None, D_POOL, D_MODEL), lay3),
            pl.BlockSpec((None, D_MODEL, D_MODEL), lay3),
            pl.BlockSpec((None, D_MODEL, ROUTE_LANES), lay3),
            pl.BlockSpec((None, D_MODEL, ROUTE_LANES), lay3),
            pl.BlockSpec((None, 1, ROUTE_LANES), lay3),
        ],
        out_specs=(pl.BlockSpec((tm, D_MODEL), lambda i: (i, 0)),
                   pl.BlockSpec((tm, D_MODEL), lambda i: (i, 0)),
                   pl.BlockSpec((8, tm), lambda i: (0, i))),
        compiler_params=_cparams("arbitrary"),
        name="merge",
    )(x_ctx, x_lat, ya, ua, yb_ctx, yb_lat, gates, mod4, norm2_g, p['d_skip'], p['w_glu'],
      p['b_glu'], p['w_br_a'], p['w_br_b'], p['w_out'], p['wr_hi'], p['wr_lo'], p['b_r'])


def _gather_kernel(idx_ref, src_ref, o_ref, sem):
    base = pl.program_id(0) * GATHER_TILE

    def copy(r):
        return pltpu.make_async_copy(src_ref.at[pl.ds(idx_ref[base + r], 1)],
                                     o_ref.at[pl.ds(r, 1)], sem)

    def start(r, c):
        copy(r).start()
        return c

    def wait(r, c):
        copy(r).wait()
        return c

    lax.fori_loop(0, GATHER_TILE, start, 0)
    lax.fori_loop(0, GATHER_TILE, wait, 0)


def _gather_rows(src, idx):
    n_out = idx.shape[0]
    width = src.shape[1]
    return pl.pallas_call(
        _gather_kernel,
        out_shape=jax.ShapeDtypeStruct((n_out, width), src.dtype),
        grid_spec=pltpu.PrefetchScalarGridSpec(
            num_scalar_prefetch=1,
            grid=(n_out // GATHER_TILE,),
            in_specs=[pl.BlockSpec(memory_space=pl.ANY)],
            out_specs=pl.BlockSpec((GATHER_TILE, width), lambda i, idx_ref: (i, 0)),
            scratch_shapes=[pltpu.SemaphoreType.DMA(())],
        ),
        compiler_params=_cparams("arbitrary"),
        name="gather_rows",
    )(idx, src)


def _expert_kernel(elo_ref, ehi_ref, act_ref, x_ref, wt_ref, w1a_ref, w1b_ref, w3a_ref, w3b_ref,
                   w2a_ref, w2b_ref, o_ref):
    j = pl.program_id(0)

    @pl.when(act_ref[j] == 1)
    def _():
        x = x_ref[...].astype(BF16)
        acc = None
        for k, (w1, w3, w2) in enumerate(((w1a_ref, w3a_ref, w2a_ref), (w1b_ref, w3b_ref, w2b_ref))):
            a = jnp.dot(x, w1[...], preferred_element_type=F32)
            b = jnp.dot(x, w3[...], preferred_element_type=F32)
            h = (a * jax.nn.sigmoid(a) * b * wt_ref[:, k:k + 1]).astype(BF16)
            o = jnp.dot(h, w2[...], preferred_element_type=F32)
            acc = o if acc is None else acc + o
        o_ref[...] = acc

    @pl.when(act_ref[j] == 0)
    def _():
        o_ref[...] = jnp.zeros_like(o_ref)


def _experts(xs, wts, e_lo, e_hi, active, w1, w3, w2, layer):
    n_rows = xs.shape[0]
    te = EXPERT_TILE
    up = lambda sel: pl.BlockSpec((None, None, D_MODEL, D_EXPERT),
                                  lambda j, lo, hi, act: (layer, (lo, hi)[sel][j], 0, 0))
    down = lambda sel: pl.BlockSpec((None, None, D_EXPERT, D_MODEL),
                                    lambda j, lo, hi, act: (layer, (lo, hi)[sel][j], 0, 0))
    return pl.pallas_call(
        _expert_kernel,
        out_shape=jax.ShapeDtypeStruct((n_rows, D_MODEL), F32),
        grid_spec=pltpu.PrefetchScalarGridSpec(
            num_scalar_prefetch=3,
            grid=(n_rows // te,),
            in_specs=[
                pl.BlockSpec((te, D_MODEL), lambda j, lo, hi, act: (j, 0)),
                pl.BlockSpec((te, 2), lambda j, lo, hi, act: (j, 0)),
                up(0), up(1), up(0), up(1), down(0), down(1),
            ],
            out_specs=pl.BlockSpec((te, D_MODEL), lambda j, lo, hi, act: (j, 0)),
        ),
        compiler_params=_cparams("arbitrary"),
        name="experts",
    )(e_lo, e_hi, active, xs, wts, w1, w1, w3, w3, w2, w2)


def _dispatch_plan(cls, w_lo, w_hi):
    n = cls.shape[0]
    te = EXPERT_TILE
    n_tiles = n // te + N_CLASSES
    onehot = (cls[:, None] == jnp.arange(N_CLASSES, dtype=jnp.int32)[None, :]).astype(jnp.int32)
    csum = jnp.cumsum(onehot, axis=0)
    counts = csum[-1]
    rank = jnp.sum((csum - onehot) * onehot, axis=1)
    padded = ((counts + te - 1) // te) * te
    ends = jnp.cumsum(padded)
    starts = ends - padded
    dest = jnp.sum(onehot * starts[None, :], axis=1) + rank
    src = jnp.zeros((n_tiles * te,), jnp.int32).at[dest].set(jnp.arange(n, dtype=jnp.int32))
    wts = jnp.zeros((n_tiles * te, 2), F32).at[dest].set(jnp.stack([w_lo, w_hi], axis=1))
    tile_start = jnp.arange(n_tiles, dtype=jnp.int32) * te
    tile_cls = jnp.minimum(jnp.sum((tile_start[:, None] >= ends[None, :]).astype(jnp.int32), axis=1),
                           N_CLASSES - 1)
    active = (tile_start < ends[-1]).astype(jnp.int32)
    grp, pair = tile_cls // N_PAIRS, tile_cls % N_PAIRS
    pair_lo = jnp.asarray([0, 0, 0, 1, 1, 2], jnp.int32)[pair]
    pair_hi = jnp.asarray([1, 2, 3, 2, 3, 3], jnp.int32)[pair]
    return dest, src, wts, grp * EXPERTS_PER_GROUP + pair_lo, grp * EXPERTS_PER_GROUP + pair_hi, active


def _combine_kernel(x_ref, f_ref, mod_ref, g_ref, o_ref, *, final):
    x = x_ref[...] + mod_ref[:, 5 * D_MODEL:6 * D_MODEL] * f_ref[...]
    if final:
        x = x * lax.rsqrt(jnp.mean(x * x, axis=-1, keepdims=True) + EPS) * g_ref[...]
    o_ref[...] = x


def _combine(x1, f, mod4, final_g, layer, tile_off, n_ctx_tiles, tiles_per_batch, ctx_row, final):
    n = x1.shape[0]
    tm = TOKEN_TILE
    row = functools.partial(_mod_row, n_ctx_tiles=n_ctx_tiles, tiles_per_batch=tiles_per_batch,
                            ctx_row=ctx_row)
    return pl.pallas_call(
        functools.partial(_combine_kernel, final=final),
        out_shape=jax.ShapeDtypeStruct((n, D_MODEL), F32),
        grid=(n // tm,),
        in_specs=[
            pl.BlockSpec((tm, D_MODEL), lambda i: (i, 0)),
            pl.BlockSpec((tm, D_MODEL), lambda i: (i, 0)),
            pl.BlockSpec((None, None, 1, 6 * D_MODEL), lambda i: (layer, row(i + tile_off), 0, 0)),
            pl.BlockSpec((1, D_MODEL), lambda i: (0, 0)),
        ],
        out_specs=pl.BlockSpec((tm, D_MODEL), lambda i: (i, 0)),
        compiler_params=_cparams("arbitrary"),
        name="combine",
    )(x1, f, mod4, final_g)


def kernel(x, c, ctx, c_ctx, norm1_g, norm2_g, w_mod, b_mod, w_in, lam_re, lam_im, log_dt, b_re,
           b_im, c_re, c_im, d_skip, w_glu, b_glu, w_pool, pool_scale, w_br_a, w_br_b, w_out, w_r1,
           b_r1, w_r2, b_r2, w1, w3, w2, final_g):
    batch, seq, d = x.shape
    assert d == D_MODEL and seq % (GRID_W * 8) == 0 and ctx.shape[1] == CTX_LEN
    tm = TOKEN_TILE
    n_ctx, n_lat = batch * CTX_LEN, batch * seq
    assert n_ctx % tm == 0 and seq % tm == 0
    n_ctx_tiles, n_lat_tiles, tiles_per_batch = n_ctx // tm, n_lat // tm, seq // tm
    mod_rows = ((batch + 1 + 7) // 8) * 8
    ctx_row = batch

    cc = jnp.concatenate([c, c_ctx[None, :], jnp.zeros((mod_rows - batch - 1, d), F32)], axis=0)
    mod = _modulation(cc, w_mod, b_mod)
    mod4 = mod.reshape(DEPTH, mod_rows, 1, 6 * d)

    w_in_bf = w_in.astype(BF16)
    w_r = jnp.concatenate([w_r1, w_r2.transpose(0, 2, 1, 3).reshape(DEPTH, d, N_EXPERTS),
                           jnp.zeros((DEPTH, d, ROUTE_LANES - 4 - N_EXPERTS), F32)], axis=2)
    wr_hi = w_r.astype(BF16)
    wr_lo = (w_r - wr_hi.astype(F32)).astype(BF16)
    b_r = jnp.concatenate([b_r1, b_r2.reshape(DEPTH, N_EXPERTS),
                           jnp.zeros((DEPTH, ROUTE_LANES - 4 - N_EXPERTS), F32)], axis=1)
    zeros_p = jnp.zeros((DEPTH, POOL_GROUP, POOL_GROUP), F32)
    wp2 = jnp.stack([
        jnp.concatenate([jnp.concatenate([w_pool[:, 0], zeros_p], axis=2),
                         jnp.concatenate([zeros_p, w_pool[:, 1]], axis=2)], axis=1),
        jnp.concatenate([jnp.concatenate([w_pool[:, 2], zeros_p], axis=2),
                         jnp.concatenate([zeros_p, w_pool[:, 3]], axis=2)], axis=1)],
        axis=1).astype(BF16)
    params = {
        'd_skip': d_skip.reshape(DEPTH, 1, D_SSM), 'w_glu': w_glu.astype(BF16),
        'b_glu': b_glu.reshape(DEPTH, 1, D_SSM), 'w_br_a': w_br_a.astype(BF16),
        'w_br_b': w_br_b.astype(BF16), 'w_out': w_out.astype(BF16),
        'wr_hi': wr_hi, 'wr_lo': wr_lo, 'b_r': b_r.reshape(DEPTH, 1, ROUTE_LANES),
    }
    w1_bf, w3_bf, w2_bf = w1.astype(BF16), w3.astype(BF16), w2.astype(BF16)
    norm1 = norm1_g.reshape(DEPTH, 1, d)
    norm2 = norm2_g.reshape(DEPTH, 1, d)
    pscale = pool_scale.reshape(DEPTH, 1, D_POOL)
    rows_lat = seq // GRID_W

    x_ctx, x_lat, lat_off = ctx.reshape(n_ctx, d), x.reshape(n_lat, d), 0
    out = None
    for l in range(DEPTH):
        last = l == DEPTH - 1
        ua, ub, gates = _inproj(x_ctx, x_lat, lat_off, mod4, norm1, w_in_bf, l, n_ctx_tiles,
                                n_lat_tiles, tiles_per_batch, ctx_row)

        mcat, wy, coef = _s5_matrices(lam_re[l], lam_im[l], log_dt[l], b_re[l], b_im[l],
                                      c_re[l], c_im[l])
        u_t, n_chunks = _to_chunks(ua, batch)
        y_t = _s5(u_t, mcat, wy, coef, batch, n_chunks, CTX_LEN // S5_CHUNK)
        ya = _from_chunks(y_t, batch, n_chunks)

        yb_lat = _pool(ub, wp2, pscale, l, rows_lat, GRID_W, batch, n_ctx // seq)
        yb_ctx = yb_lat if last else _pool(ub, wp2, pscale, l, 1, CTX_LEN, batch, 0)

        tile_off = n_ctx_tiles if last else 0
        n_tiles = n_lat_tiles if last else n_ctx_tiles + n_lat_tiles
        x1, h2, route = _merge(x_ctx, x_lat, lat_off, ya, ua, yb_ctx, yb_lat, gates, mod4, norm2,
                               params, l, tile_off, n_tiles, n_ctx_tiles, tiles_per_batch, ctx_row)

        cls = route[0].astype(jnp.int32)
        dest, src, wts, e_lo, e_hi, active = _dispatch_plan(cls, route[1], route[2])
        xs = _gather_rows(h2, src)
        fs = _experts(xs, wts, e_lo, e_hi, active, w1_bf, w3_bf, w2_bf, l)
        f = _gather_rows(fs, dest)
        res = _combine(x1, f, mod4, final_g.reshape(1, d), l, tile_off, n_ctx_tiles,
                       tiles_per_batch, ctx_row, last)
        if last:
            out = res.reshape(batch, seq, d)
        else:
            x_ctx, x_lat, lat_off = res, res, n_ctx_tiles
    return out
```

```python
import functools

import numpy as np
import jax
import jax.numpy as jnp
from jax import lax
from jax.experimental import pallas as pl
from jax.experimental.pallas import tpu as pltpu

F32 = jnp.float32
BF16 = jnp.bfloat16

D_MODEL = 1024
DEPTH = 2
GRID_W = 64
CTX_LEN = 256
EPS = 1e-6

D_SSM = 256
SSM_GROUP = 16
N_SSM_GROUPS = 16
SSM_STATE = 64
S5_CHUNK = 16

D_POOL = 768
POOL_WINDOWS = (2, 4, 8, 16)
POOL_GROUP = 192
POOL_SEG = (0, 128, 384, 512)
GATE_OFF = D_SSM + D_POOL
D_IN = D_SSM + D_POOL + 2 * D_MODEL

N_EXPERT_GROUPS = 4
EXPERTS_PER_GROUP = 4
N_EXPERTS = 16
D_EXPERT = 512
N_PAIRS = 6
N_CLASSES = N_EXPERT_GROUPS * N_PAIRS
ROUTE_LANES = 128

TILE_TOK = 32
ROW_PITCH = TILE_TOK + 4
POOL_BLOCK = 256
EXPERT_TILE = 256
GATHER_TILE = 256
V7X_VMEM_LIMIT = 48 * 1024 * 1024


def _cparams(*sem):
    return pltpu.CompilerParams(dimension_semantics=sem, vmem_limit_bytes=V7X_VMEM_LIMIT)


def _split_bf16(v):
    hi = v.astype(BF16)
    lo = (v - hi.astype(F32)).astype(BF16)
    return hi, lo


def _mod_kernel(c_ref, w_ref, b_ref, o_ref):
    c = c_ref[...]
    a = c * jax.nn.sigmoid(c)
    a_hi, a_lo = _split_bf16(a)
    w_hi, w_lo = _split_bf16(w_ref[...])
    acc = jnp.dot(a_hi, w_hi, preferred_element_type=F32)
    acc += jnp.dot(a_lo, w_hi, preferred_element_type=F32)
    acc += jnp.dot(a_hi, w_lo, preferred_element_type=F32)
    o_ref[...] = acc + b_ref[...]


def _modulation(cc, w_mod, b_mod):
    rows = cc.shape[0]
    nblk = 4
    cols = 6 * D_MODEL // nblk
    return pl.pallas_call(
        _mod_kernel,
        out_shape=jax.ShapeDtypeStruct((DEPTH, rows, 6 * D_MODEL), F32),
        grid=(DEPTH, nblk),
        in_specs=[
            pl.BlockSpec((rows, D_MODEL), lambda l, j: (0, 0)),
            pl.BlockSpec((None, D_MODEL, cols), lambda l, j: (l, 0, j)),
            pl.BlockSpec((None, 1, cols), lambda l, j: (l, 0, j)),
        ],
        out_specs=pl.BlockSpec((None, rows, cols), lambda l, j: (l, 0, j)),
        compiler_params=_cparams("arbitrary", "arbitrary"),
        name="modulation",
    )(cc, w_mod, b_mod.reshape(DEPTH, 1, 6 * D_MODEL))


def _pick(ctx_ref, lat_ref, tile, n_ctx_tiles):
    return jnp.where(tile < n_ctx_tiles, ctx_ref[...], lat_ref[...])


def _src_specs(cols, batch, n_ctx_tiles, tile_off, lat_off):
    blk = (batch, TILE_TOK, cols)
    return [pl.BlockSpec(blk, lambda j: (0, jnp.minimum(j + tile_off, n_ctx_tiles - 1), 0)),
            pl.BlockSpec(blk, lambda j: (0, jnp.maximum(j + tile_off - n_ctx_tiles, 0) + lat_off, 0))]


def _mod_spec(batch, layer, n_ctx_tiles, tile_off):
    return pl.BlockSpec((None, None, batch, 1, 6 * D_MODEL),
                        lambda j: (layer, (j + tile_off >= n_ctx_tiles).astype(jnp.int32), 0, 0, 0))


def _lane_group(batch):
    return lax.broadcasted_iota(jnp.int32, (batch, 128), 1) // SSM_GROUP


def _to_chunk_layout(ua_scr, u_ref, batch):
    blk = _lane_group(batch)
    for cl in range(TILE_TOK // S5_CHUNK):
        for g in range(N_SSM_GROUPS):
            for q in range(2):
                acc = None
                for s in range(8 * q, 8 * q + 8):
                    src = ua_scr[g // 8, pl.ds(cl * S5_CHUNK + s, batch, stride=ROW_PITCH), :]
                    shift = ((s - g) % 8) * SSM_GROUP
                    v = pltpu.roll(src, shift, 1) if shift else src
                    acc = v if acc is None else jnp.where(blk == s % 8, v, acc)
                u_ref[g, cl * batch:(cl + 1) * batch, 128 * q:128 * q + 128] = acc


def _from_chunk_layout(y_ref, ya_scr, batch):
    blk = _lane_group(batch)
    for cl in range(TILE_TOK // S5_CHUNK):
        for s in range(S5_CHUNK):
            for p in range(2):
                acc = None
                for g in range(8 * p, 8 * p + 8):
                    src = y_ref[g, cl * batch:(cl + 1) * batch, 128 * (s // 8):128 * (s // 8) + 128]
                    shift = ((g - s) % 8) * SSM_GROUP
                    v = pltpu.roll(src, shift, 1) if shift else src
                    acc = v if acc is None else jnp.where(blk == g % 8, v, acc)
                ya_scr[p, pl.ds(cl * S5_CHUNK + s, batch, stride=ROW_PITCH), :] = acc


def _inproj_kernel(xc_ref, xl_ref, mod_ref, g_ref, w_ref, u_ref, ub_ref, gate_ref, h_scr, ua_scr, *,
                   batch, n_ctx_tiles):
    rows = batch * TILE_TOK
    x = _pick(xc_ref, xl_ref, pl.program_id(0), n_ctx_tiles)
    y = x * lax.rsqrt(jnp.mean(x * x, axis=-1, keepdims=True) + EPS) * g_ref[...]
    shift = mod_ref[:, :, 0:D_MODEL]
    scale = mod_ref[:, :, D_MODEL:2 * D_MODEL]
    h_scr[...] = (y * (1.0 + scale) + shift).reshape(rows, D_MODEL).astype(BF16)
    h = h_scr[...]
    ua = jnp.dot(h, w_ref[:, 0:D_SSM], preferred_element_type=F32)
    for b in range(batch):
        for p in range(2):
            ua_scr[p, b * ROW_PITCH:b * ROW_PITCH + TILE_TOK, :] = (
                ua[b * TILE_TOK:(b + 1) * TILE_TOK, 128 * p:128 * p + 128])
    ub = jnp.dot(h, w_ref[:, D_SSM:GATE_OFF], preferred_element_type=F32)
    ub_ref[...] = ub.reshape(batch, TILE_TOK, D_POOL)
    for k in range(2):
        lo = GATE_OFF + k * D_MODEL
        g = jnp.dot(h, w_ref[:, lo:lo + D_MODEL], preferred_element_type=F32)
        gate_ref[:, :, k * D_MODEL:(k + 1) * D_MODEL] = (
            jax.nn.sigmoid(g).astype(BF16).reshape(batch, TILE_TOK, D_MODEL))
    _to_chunk_layout(ua_scr, u_ref, batch)


def _inproj(x_ctx, x_lat, lat_off, mod_sel, norm_g, w_in_bf, layer, batch, n_ctx_tiles, n_tiles):
    rows = batch * TILE_TOK
    n_tok = n_tiles * TILE_TOK
    chunk_rows = batch * TILE_TOK // S5_CHUNK
    return pl.pallas_call(
        functools.partial(_inproj_kernel, batch=batch, n_ctx_tiles=n_ctx_tiles),
        out_shape=(jax.ShapeDtypeStruct((N_SSM_GROUPS, n_tiles * chunk_rows, D_SSM), F32),
                   jax.ShapeDtypeStruct((batch, n_tok, D_POOL), F32),
                   jax.ShapeDtypeStruct((batch, n_tok, 2 * D_MODEL), BF16)),
        grid=(n_tiles,),
        in_specs=_src_specs(D_MODEL, batch, n_ctx_tiles, 0, lat_off) + [
            _mod_spec(batch, layer, n_ctx_tiles, 0),
            pl.BlockSpec((None, 1, D_MODEL), lambda j: (layer, 0, 0)),
            pl.BlockSpec((None, D_MODEL, D_IN), lambda j: (layer, 0, 0)),
        ],
        out_specs=(pl.BlockSpec((N_SSM_GROUPS, chunk_rows, D_SSM), lambda j: (0, j, 0)),
                   pl.BlockSpec((batch, TILE_TOK, D_POOL), lambda j: (0, j, 0)),
                   pl.BlockSpec((batch, TILE_TOK, 2 * D_MODEL), lambda j: (0, j, 0))),
        scratch_shapes=[pltpu.VMEM((rows, D_MODEL), BF16),
                        pltpu.VMEM((2, batch * ROW_PITCH, 128), F32)],
        compiler_params=_cparams("arbitrary"),
        name="inproj",
    )(x_ctx, x_lat, mod_sel, norm_g, w_in_bf)


def _s5_matrices(lam_re, lam_im, log_dt, b_re, b_im, c_re, c_im):
    L, G, N, H = S5_CHUNK, N_SSM_GROUPS, SSM_STATE, SSM_GROUP
    lr, li = lam_re.astype(F32), lam_im.astype(F32)
    dt = jnp.exp(log_dt.astype(F32))[..., None]
    zr, zi = lr * dt, li * dt
    k = jnp.arange(L + 1, dtype=F32)[:, None, None, None]
    pm = jnp.exp(zr[None] * k)
    pr, pi = pm * jnp.cos(zi[None] * k), pm * jnp.sin(zi[None] * k)
    nr, ni = pr[1] - 1.0, pi[1]
    den = lr * lr + li * li
    fr = (nr * lr + ni * li) / den
    fi = (ni * lr - nr * li) / den
    br, bi = b_re.astype(F32), b_im.astype(F32)
    bbr = fr[..., None] * br - fi[..., None] * bi
    bbi = fr[..., None] * bi + fi[..., None] * br
    cr, ci = c_re.astype(F32), c_im.astype(F32)
    cpr = cr[None] * pr[:, :, :, None, :] - ci[None] * pi[:, :, :, None, :]
    cpi = cr[None] * pi[:, :, :, None, :] + ci[None] * pr[:, :, :, None, :]
    kern = (jnp.einsum('kdgan,dgnh->kdgah', cpr, bbr)
            - jnp.einsum('kdgan,dgnh->kdgah', cpi, bbi))
    s_idx = np.arange(L)[:, None]
    t_idx = np.arange(L)[None, :]
    lag_f = np.clip(t_idx - s_idx, 0, L - 1)
    lag_b = np.clip(s_idx - t_idx, 0, L - 1)
    mf = jnp.where((t_idx >= s_idx)[:, :, None, None, None], kern[:, 0][lag_f], 0.0)
    mb = jnp.where((s_idx >= t_idx)[:, :, None, None, None], kern[:, 1][lag_b], 0.0)
    m = (mf + mb).transpose(2, 0, 4, 1, 3).reshape(G, L * H, L * H)
    powers = (L - 1 - np.arange(L), np.arange(L))
    inj = []
    for d in range(2):
        p_r, p_i = pr[powers[d], d], pi[powers[d], d]
        wr = p_r[..., None] * bbr[d][None] - p_i[..., None] * bbi[d][None]
        wi = p_r[..., None] * bbi[d][None] + p_i[..., None] * bbr[d][None]
        inj.append((wr, wi))
    w_inj = jnp.concatenate([inj[0][0], inj[1][0], inj[0][1], inj[1][1]], axis=2)
    mcat = jnp.concatenate([m, w_inj.transpose(1, 0, 3, 2).reshape(G, L * H, 4 * N)], axis=2)
    rd = (np.arange(L) + 1, L - np.arange(L))
    w_rd = jnp.concatenate([cpr[rd[0], 0], cpr[rd[1], 1], -cpi[rd[0], 0], -cpi[rd[1], 1]],
                           axis=3)
    wy = w_rd.transpose(1, 3, 0, 2).reshape(G, 4 * N, L * H)
    a_r = jnp.concatenate([pr[L][0], pr[L][1]], axis=-1)
    a_i = jnp.concatenate([pi[L][0], pi[L][1]], axis=-1)
    coef = jnp.stack([a_r, a_i] + [jnp.zeros_like(a_r)] * 6, axis=1)
    return mcat.astype(BF16), wy.astype(BF16), coef


def _s5_kernel(u_ref, mcat_ref, wy_ref, coef_ref, dskip_ref, y_ref, s_scr, h_scr, *, batch,
               n_chunks, n_ctx_chunks):
    n = SSM_STATE
    rb = S5_CHUNK * batch
    n_rb = n_chunks // S5_CHUNK

    def inject(i, c):
        r0 = pl.multiple_of(i * rb, rb)
        u = u_ref[pl.ds(r0, rb), :]
        ub = u.astype(BF16)
        y_ref[pl.ds(r0, rb), :] = (jnp.dot(ub, mcat_ref[:, 0:D_SSM], preferred_element_type=F32)
                                   + dskip_ref[...] * u)
        s_scr[pl.ds(r0, rb), :] = jnp.dot(ub, mcat_ref[:, D_SSM:2 * D_SSM],
                                          preferred_element_type=F32)
        return c

    lax.fori_loop(0, n_rb, inject, 0)
    a_r, a_i = coef_ref[0:1, :], coef_ref[1:2, :]
    fwd_lanes = lax.broadcasted_iota(jnp.int32, (batch, 2 * n), 1) < n

    def step(k, carry):
        h_re, h_im = carry
        cb = jnp.where(k < n_ctx_chunks, n_ctx_chunks - 1 - k, n_chunks - 1 + n_ctx_chunks - k)
        rf = pl.multiple_of(k * batch, batch)
        rk = pl.multiple_of(cb * batch, batch)
        h_scr[pl.ds(rf, batch), 0:n] = h_re[:, 0:n]
        h_scr[pl.ds(rk, batch), n:2 * n] = h_re[:, n:2 * n]
        h_scr[pl.ds(rf, batch), 2 * n:3 * n] = h_im[:, 0:n]
        h_scr[pl.ds(rk, batch), 3 * n:4 * n] = h_im[:, n:2 * n]
        s_re = jnp.where(fwd_lanes, s_scr[pl.ds(rf, batch), 0:2 * n], s_scr[pl.ds(rk, batch), 0:2 * n])
        s_im = jnp.where(fwd_lanes, s_scr[pl.ds(rf, batch), 2 * n:4 * n],
                         s_scr[pl.ds(rk, batch), 2 * n:4 * n])
        return (a_r * h_re - a_i * h_im + s_re, a_r * h_im + a_i * h_re + s_im)

    zero = jnp.zeros((batch, 2 * n), F32)
    lax.fori_loop(0, n_chunks, step, (zero, zero), unroll=4)

    def readout(i, c):
        r0 = pl.multiple_of(i * rb, rb)
        y_ref[pl.ds(r0, rb), :] += jnp.dot(h_scr[pl.ds(r0, rb), :].astype(BF16), wy_ref[...],
                                           preferred_element_type=F32)
        return c

    lax.fori_loop(0, n_rb, readout, 0)


def _s5(u_t, mcat, wy, coef, dskip, batch, n_chunks, n_ctx_chunks):
    G = N_SSM_GROUPS
    rows = u_t.shape[1]
    kern = functools.partial(_s5_kernel, batch=batch, n_chunks=n_chunks, n_ctx_chunks=n_ctx_chunks)
    per_group = lambda *blk: pl.BlockSpec((None,) + blk, lambda g: (g, 0, 0))
    return pl.pallas_call(
        kern,
        out_shape=jax.ShapeDtypeStruct((G, rows, D_SSM), F32),
        grid=(G,),
        in_specs=[per_group(rows, D_SSM), per_group(D_SSM, 2 * D_SSM), per_group(D_SSM, D_SSM),
                  per_group(8, 2 * SSM_STATE), per_group(1, D_SSM)],
        out_specs=per_group(rows, D_SSM),
        scratch_shapes=[pltpu.VMEM((rows, D_SSM), F32), pltpu.VMEM((rows, D_SSM), F32)],
        compiler_params=_cparams("arbitrary"),
        name="s5",
    )(u_t, mcat, wy, coef, dskip)


def _window(w):
    return -(w // 2), w - 1 - w // 2


def _pool_constants(rows, width):
    rpb = POOL_BLOCK // width
    pm = np.zeros((4, POOL_BLOCK, POOL_BLOCK), np.float32)
    inv = np.zeros((rows * width, 4), np.float32)
    col = np.arange(width)
    row = np.arange(rows)
    for i, w in enumerate(POOL_WINDOWS):
        lo, hi = _window(w)
        c0, c1 = np.clip(col + lo, 0, width - 1), np.clip(col + hi, 0, width - 1)
        r0, r1 = np.clip(row + lo, 0, rows - 1), np.clip(row + hi, 0, rows - 1)
        band = ((col[None, :] >= c0[:, None]) & (col[None, :] <= c1[:, None])).astype(np.float32)
        for r in range(rpb):
            pm[i, r * width:(r + 1) * width, r * width:(r + 1) * width] = band
        cnt = (r1 - r0 + 1)[:, None] * (c1 - c0 + 1)[None, :]
        inv[:, i] = (1.0 / cnt).reshape(-1)
    return jnp.asarray(pm, BF16), jnp.asarray(inv, F32)


def _pool_segment(u_ref, o_ref, cs_scr, pm_ref, inv_ref, wp_ref, ps_ref, tok0, rows, width):
    n_tok = rows * width
    n_blk = n_tok // POOL_BLOCK
    pad = 8 * width if rows > 1 else 0
    if rows > 1:
        zeros = jnp.zeros((pad, 2 * 128), F32)
        for i in range(4):
            cs_scr[i, 0:pad, :] = zeros
            cs_scr[i, pad + n_tok:pad + n_tok + pad, :] = zeros
    for b in range(n_blk):
        t0 = b * POOL_BLOCK
        for i in range(4):
            seg = POOL_SEG[i]
            xb = u_ref[tok0 + t0:tok0 + t0 + POOL_BLOCK, seg:seg + 256].astype(BF16)
            cs_scr[i, pad + t0:pad + t0 + POOL_BLOCK, :] = jnp.dot(
                pm_ref[i], xb, preferred_element_type=F32)
    lane = lax.broadcasted_iota(jnp.int32, (POOL_BLOCK, 128), 1)
    low_half = lane < 64
    for b in range(n_blk):
        t0 = b * POOL_BLOCK
        win = []
        for i, w in enumerate(POOL_WINDOWS):
            lo, hi = _window(w) if rows > 1 else (0, 0)
            acc = None
            for k in range(lo, hi + 1):
                base = pad + t0 + k * width
                piece = cs_scr[i, base:base + POOL_BLOCK, :]
                acc = piece if acc is None else acc + piece
            win.append(acc * inv_ref[t0:t0 + POOL_BLOCK, i:i + 1])
        x = u_ref[tok0 + t0:tok0 + t0 + POOL_BLOCK, :]
        pooled = jnp.concatenate([
            win[0][:, 0:128],
            jnp.where(low_half, win[0][:, 128:256], win[1][:, 0:128]),
            win[1][:, 128:256],
            win[2][:, 0:128],
            jnp.where(low_half, win[2][:, 128:256], win[3][:, 0:128]),
            win[3][:, 128:256]], axis=1)
        d = (pooled - x).astype(BF16)
        y0 = jnp.dot(d[:, 0:384], wp_ref[0], preferred_element_type=F32)
        y1 = jnp.dot(d[:, 384:768], wp_ref[1], preferred_element_type=F32)
        y = jnp.concatenate([y0, y1], axis=1) * ps_ref[...]
        o_ref[tok0 + t0:tok0 + t0 + POOL_BLOCK, :] = y.astype(BF16)


def _pool_kernel(u_ref, pmc_ref, invc_ref, pml_ref, invl_ref, wp_ref, ps_ref, o_ref, cs_scr, *,
                 lat_rows, with_ctx):
    if with_ctx:
        _pool_segment(u_ref, o_ref, cs_scr, pmc_ref, invc_ref, wp_ref, ps_ref, 0, 1, CTX_LEN)
    else:
        o_ref[0:CTX_LEN, :] = jnp.zeros((CTX_LEN, D_POOL), BF16)
    _pool_segment(u_ref, o_ref, cs_scr, pml_ref, invl_ref, wp_ref, ps_ref, CTX_LEN, lat_rows, GRID_W)


def _pool(ub, wp2, pscale, layer, with_ctx):
    batch, n_tok, _ = ub.shape
    lat_rows = (n_tok - CTX_LEN) // GRID_W
    pmc, invc = _pool_constants(1, CTX_LEN)
    pml, invl = _pool_constants(lat_rows, GRID_W)
    const = lambda *blk: pl.BlockSpec(blk, lambda b: (0,) * len(blk))
    return pl.pallas_call(
        functools.partial(_pool_kernel, lat_rows=lat_rows, with_ctx=with_ctx),
        out_shape=jax.ShapeDtypeStruct((batch, n_tok, D_POOL), BF16),
        grid=(batch,),
        in_specs=[
            pl.BlockSpec((None, n_tok, D_POOL), lambda b: (b, 0, 0)),
            const(4, POOL_BLOCK, POOL_BLOCK), const(CTX_LEN, 4),
            const(4, POOL_BLOCK, POOL_BLOCK), const(lat_rows * GRID_W, 4),
            pl.BlockSpec((None, 2, 384, 384), lambda b: (layer, 0, 0, 0)),
            pl.BlockSpec((None, 1, D_POOL), lambda b: (layer, 0, 0)),
        ],
        out_specs=pl.BlockSpec((None, n_tok, D_POOL), lambda b: (b, 0, 0)),
        scratch_shapes=[pltpu.VMEM((4, lat_rows * GRID_W + 16 * GRID_W, 256), F32)],
        compiler_params=_cparams("arbitrary"),
        name="pool",
    )(ub, pmc, invc, pml, invl, wp2, pscale)


def _first_argmax(rows_):
    best, idx = rows_[0], jnp.zeros_like(rows_[0], dtype=jnp.int32)
    for k in range(1, len(rows_)):
        take = rows_[k] > best
        idx = jnp.where(take, k, idx)
        best = jnp.where(take, rows_[k], best)
    return best, idx


def _route(lt):
    g_rows = [lt[k:k + 1, :] for k in range(N_EXPERT_GROUPS)]
    best, grp = _first_argmax(g_rows)
    denom = sum(jnp.exp(r - best) for r in g_rows)
    p_grp = 1.0 / denom
    inner = []
    for e in range(EXPERTS_PER_GROUP):
        acc = jnp.zeros_like(best)
        for g in range(N_EXPERT_GROUPS):
            r = 4 + 4 * g + e
            acc = jnp.where(grp == g, lt[r:r + 1, :], acc)
        inner.append(acc)
    v1, i1 = _first_argmax(inner)
    masked = [jnp.where(i1 == e, -jnp.inf, inner[e]) for e in range(EXPERTS_PER_GROUP)]
    v2, i2 = _first_argmax(masked)
    e21 = jnp.exp(v2 - v1)
    w1 = p_grp / (1.0 + e21)
    w2 = p_grp * e21 / (1.0 + e21)
    first_low = i1 < i2
    lo = jnp.where(first_low, i1, i2)
    hi = jnp.where(first_low, i2, i1)
    w_lo = jnp.where(first_low, w1, w2)
    w_hi = jnp.where(first_low, w2, w1)
    off = jnp.where(lo == 0, 0, jnp.where(lo == 1, 3, 5))
    cls = N_PAIRS * grp + off + hi - lo - 1
    return cls.astype(F32), w_lo, w_hi


def _merge_kernel(xc_ref, xl_ref, y_ref, yb_ref, gate_ref, mod_ref, g2_ref, wglu_ref, bglu_ref,
                  wbra_ref, wbrb_ref, wout_ref, wrh_ref, wrl_ref, br_ref, x1_ref, h2_ref, route_ref,
                  ya_scr, *, batch, n_ctx_tiles, tile_off):
    D = D_MODEL
    rows = batch * TILE_TOK
    tile = pl.program_id(0) + tile_off
    _from_chunk_layout(y_ref, ya_scr, batch)
    y = jnp.concatenate(
        [jnp.concatenate([ya_scr[p, b * ROW_PITCH:b * ROW_PITCH + TILE_TOK, :] for p in range(2)],
                         axis=1) for b in range(batch)], axis=0)
    z = jax.nn.gelu(y)
    glu = jax.nn.sigmoid(jnp.dot(z.astype(BF16), wglu_ref[...], preferred_element_type=F32)
                         + bglu_ref[...])
    ya = (z * glu).astype(BF16)
    gates = gate_ref[...].reshape(rows, 2 * D)
    m = gates[:, 0:D].astype(F32) * jnp.dot(ya, wbra_ref[...], preferred_element_type=F32)
    m += gates[:, D:2 * D].astype(F32) * jnp.dot(yb_ref[...].reshape(rows, D_POOL), wbrb_ref[...],
                                                 preferred_element_type=F32)
    out = jnp.dot(m.astype(BF16), wout_ref[...], preferred_element_type=F32)
    x_in = _pick(xc_ref, xl_ref, tile, n_ctx_tiles)
    x1 = x_in + mod_ref[:, :, 2 * D:3 * D] * out.reshape(batch, TILE_TOK, D)
    x1_ref[...] = x1
    yn = x1 * lax.rsqrt(jnp.mean(x1 * x1, axis=-1, keepdims=True) + EPS) * g2_ref[...]
    h2 = yn * (1.0 + mod_ref[:, :, 4 * D:5 * D]) + mod_ref[:, :, 3 * D:4 * D]
    h2_ref[...] = h2
    h_hi, h_lo = _split_bf16(h2.reshape(rows, D))
    logits = jnp.dot(h_hi, wrh_ref[...], preferred_element_type=F32)
    logits += jnp.dot(h_lo, wrh_ref[...], preferred_element_type=F32)
    logits += jnp.dot(h_hi, wrl_ref[...], preferred_element_type=F32)
    logits += br_ref[...]
    cls, w_lo, w_hi = _route(logits.T)
    zero = jnp.zeros_like(cls)
    route_ref[...] = jnp.concatenate([cls, w_lo, w_hi, zero, zero, zero, zero, zero], axis=0)


def _merge(x_ctx, x_lat, lat_off, y_t, yb, gates, mod_sel, norm2_g, p, layer, batch, tile_off,
           n_tiles, n_ctx_tiles):
    rows = batch * TILE_TOK
    chunk_rows = rows // S5_CHUNK
    tok = lambda cols: pl.BlockSpec((batch, TILE_TOK, cols), lambda j: (0, j + tile_off, 0))
    lay3 = lambda *blk: pl.BlockSpec((None,) + blk, lambda j: (layer, 0, 0))
    n_tok = n_tiles * TILE_TOK
    return pl.pallas_call(
        functools.partial(_merge_kernel, batch=batch, n_ctx_tiles=n_ctx_tiles, tile_off=tile_off),
        out_shape=(jax.ShapeDtypeStruct((batch, n_tok, D_MODEL), F32),
                   jax.ShapeDtypeStruct((batch, n_tok, D_MODEL), F32),
                   jax.ShapeDtypeStruct((8, n_tiles * rows), F32)),
        grid=(n_tiles,),
        in_specs=_src_specs(D_MODEL, batch, n_ctx_tiles, tile_off, lat_off) + [
            pl.BlockSpec((N_SSM_GROUPS, chunk_rows, D_SSM), lambda j: (0, j + tile_off, 0)),
            tok(D_POOL), tok(2 * D_MODEL),
            _mod_spec(batch, layer, n_ctx_tiles, tile_off),
            lay3(1, D_MODEL), lay3(D_SSM, D_SSM), lay3(1, D_SSM), lay3(D_SSM, D_MODEL),
            lay3(D_POOL, D_MODEL), lay3(D_MODEL, D_MODEL), lay3(D_MODEL, ROUTE_LANES),
            lay3(D_MODEL, ROUTE_LANES), lay3(1, ROUTE_LANES),
        ],
        out_specs=(pl.BlockSpec((batch, TILE_TOK, D_MODEL), lambda j: (0, j, 0)),
                   pl.BlockSpec((batch, TILE_TOK, D_MODEL), lambda j: (0, j, 0)),
                   pl.BlockSpec((8, rows), lambda j: (0, j))),
        scratch_shapes=[pltpu.VMEM((2, batch * ROW_PITCH, 128), F32)],
        compiler_params=_cparams("arbitrary"),
        name="merge",
    )(x_ctx, x_lat, y_t, yb, gates, mod_sel, norm2_g, p['w_glu'], p['b_glu'], p['w_br_a'],
      p['w_br_b'], p['w_out'], p['wr_hi'], p['wr_lo'], p['b_r'])


def _gather_kernel(idx_ref, src_ref, o_ref, sem):
    base = pl.program_id(0) * GATHER_TILE

    def copy(r):
        return pltpu.make_async_copy(src_ref.at[pl.ds(idx_ref[base + r], 1)],
                                     o_ref.at[pl.ds(r, 1)], sem)

    def start(r, c):
        copy(r).start()
        return c

    def wait(r, c):
        copy(r).wait()
        return c

    lax.fori_loop(0, GATHER_TILE, start, 0)
    lax.fori_loop(0, GATHER_TILE, wait, 0)


def _gather_rows(src, idx):
    n_out = idx.shape[0]
    width = src.shape[1]
    return pl.pallas_call(
        _gather_kernel,
        out_shape=jax.ShapeDtypeStruct((n_out, width), src.dtype),
        grid_spec=pltpu.PrefetchScalarGridSpec(
            num_scalar_prefetch=1,
            grid=(n_out // GATHER_TILE,),
            in_specs=[pl.BlockSpec(memory_space=pl.ANY)],
            out_specs=pl.BlockSpec((GATHER_TILE, width), lambda i, idx_ref: (i, 0)),
            scratch_shapes=[pltpu.SemaphoreType.DMA(())],
        ),
        compiler_params=_cparams("arbitrary"),
        name="gather_rows",
    )(idx, src)


def _expert_kernel(elo_ref, ehi_ref, act_ref, x_ref, wt_ref, w1a_ref, w1b_ref, w3a_ref, w3b_ref,
                   w2a_ref, w2b_ref, o_ref):
    j = pl.program_id(0)

    @pl.when(act_ref[j] == 1)
    def _():
        x = x_ref[...].astype(BF16)
        acc = None
        for k, (w1, w3, w2) in enumerate(((w1a_ref, w3a_ref, w2a_ref), (w1b_ref, w3b_ref, w2b_ref))):
            a = jnp.dot(x, w1[...], preferred_element_type=F32)
            b = jnp.dot(x, w3[...], preferred_element_type=F32)
            h = (a * jax.nn.sigmoid(a) * b * wt_ref[:, k:k + 1]).astype(BF16)
            o = jnp.dot(h, w2[...], preferred_element_type=F32)
            acc = o if acc is None else acc + o
        o_ref[...] = acc

    @pl.when(act_ref[j] == 0)
    def _():
        o_ref[...] = jnp.zeros_like(o_ref)


def _experts(xs, wts, e_lo, e_hi, active, w1, w3, w2, layer):
    n_rows = xs.shape[0]
    te = EXPERT_TILE
    up = lambda sel: pl.BlockSpec((None, None, D_MODEL, D_EXPERT),
                                  lambda j, lo, hi, act: (layer, (lo, hi)[sel][j], 0, 0))
    down = lambda sel: pl.BlockSpec((None, None, D_EXPERT, D_MODEL),
                                    lambda j, lo, hi, act: (layer, (lo, hi)[sel][j], 0, 0))
    return pl.pallas_call(
        _expert_kernel,
        out_shape=jax.ShapeDtypeStruct((n_rows, D_MODEL), F32),
        grid_spec=pltpu.PrefetchScalarGridSpec(
            num_scalar_prefetch=3,
            grid=(n_rows // te,),
            in_specs=[
                pl.BlockSpec((te, D_MODEL), lambda j, lo, hi, act: (j, 0)),
                pl.BlockSpec((te, 2), lambda j, lo, hi, act: (j, 0)),
                up(0), up(1), up(0), up(1), down(0), down(1),
            ],
            out_specs=pl.BlockSpec((te, D_MODEL), lambda j, lo, hi, act: (j, 0)),
        ),
        compiler_params=_cparams("arbitrary"),
        name="experts",
    )(e_lo, e_hi, active, xs, wts, w1, w1, w3, w3, w2, w2)


def _dispatch_plan(cls, w_lo, w_hi):
    n = cls.shape[0]
    te = EXPERT_TILE
    n_tiles = n // te + N_CLASSES
    onehot = (cls[:, None] == jnp.arange(N_CLASSES, dtype=jnp.int32)[None, :]).astype(jnp.int32)
    csum = jnp.cumsum(onehot, axis=0)
    counts = csum[-1]
    rank = jnp.sum((csum - onehot) * onehot, axis=1)
    padded = ((counts + te - 1) // te) * te
    ends = jnp.cumsum(padded)
    starts = ends - padded
    dest = jnp.sum(onehot * starts[None, :], axis=1) + rank
    src = jnp.zeros((n_tiles * te,), jnp.int32).at[dest].set(jnp.arange(n, dtype=jnp.int32))
    wts = jnp.zeros((n_tiles * te, 2), F32).at[dest].set(jnp.stack([w_lo, w_hi], axis=1))
    tile_start = jnp.arange(n_tiles, dtype=jnp.int32) * te
    tile_cls = jnp.minimum(jnp.sum((tile_start[:, None] >= ends[None, :]).astype(jnp.int32), axis=1),
                           N_CLASSES - 1)
    active = (tile_start < ends[-1]).astype(jnp.int32)
    grp, pair = tile_cls // N_PAIRS, tile_cls % N_PAIRS
    pair_lo = jnp.asarray([0, 0, 0, 1, 1, 2], jnp.int32)[pair]
    pair_hi = jnp.asarray([1, 2, 3, 2, 3, 3], jnp.int32)[pair]
    return dest, src, wts, grp * EXPERTS_PER_GROUP + pair_lo, grp * EXPERTS_PER_GROUP + pair_hi, active


def _combine_kernel(x_ref, f_ref, mod_ref, g_ref, o_ref, *, final):
    x = x_ref[...] + mod_ref[:, :, 5 * D_MODEL:6 * D_MODEL] * f_ref[...]
    if final:
        x = x * lax.rsqrt(jnp.mean(x * x, axis=-1, keepdims=True) + EPS) * g_ref[...]
    o_ref[...] = x


def _combine(x1, f, mod_sel, final_g, layer, tile_off, n_ctx_tiles, final):
    batch, n_tok, _ = x1.shape
    tok = pl.BlockSpec((batch, TILE_TOK, D_MODEL), lambda j: (0, j, 0))
    return pl.pallas_call(
        functools.partial(_combine_kernel, final=final),
        out_shape=jax.ShapeDtypeStruct((batch, n_tok, D_MODEL), F32),
        grid=(n_tok // TILE_TOK,),
        in_specs=[tok, tok, _mod_spec(batch, layer, n_ctx_tiles, tile_off),
                  pl.BlockSpec((1, D_MODEL), lambda j: (0, 0))],
        out_specs=tok,
        compiler_params=_cparams("arbitrary"),
        name="combine",
    )(x1, f, mod_sel, final_g)


def kernel(x, c, ctx, c_ctx, norm1_g, norm2_g, w_mod, b_mod, w_in, lam_re, lam_im, log_dt, b_re,
           b_im, c_re, c_im, d_skip, w_glu, b_glu, w_pool, pool_scale, w_br_a, w_br_b, w_out, w_r1,
           b_r1, w_r2, b_r2, w1, w3, w2, final_g):
    batch, seq, d = x.shape
    assert d == D_MODEL and seq % (GRID_W * 8) == 0 and ctx.shape[1] == CTX_LEN and batch % 8 == 0
    n_ctx_tiles, n_lat_tiles = CTX_LEN // TILE_TOK, seq // TILE_TOK
    n_all_tiles = n_ctx_tiles + n_lat_tiles
    n_chunks = (CTX_LEN + seq) // S5_CHUNK
    mod_rows = ((batch + 1 + 7) // 8) * 8

    cc = jnp.concatenate([c, c_ctx[None, :], jnp.zeros((mod_rows - batch - 1, d), F32)], axis=0)
    mod = _modulation(cc, w_mod, b_mod)
    mod_sel = jnp.stack([jnp.broadcast_to(mod[:, batch:batch + 1], (DEPTH, batch, 6 * d)),
                         mod[:, :batch]], axis=1)[:, :, :, None, :]

    w_in_bf = w_in.astype(BF16)
    w_r = jnp.concatenate([w_r1, w_r2.transpose(0, 2, 1, 3).reshape(DEPTH, d, N_EXPERTS),
                           jnp.zeros((DEPTH, d, ROUTE_LANES - 4 - N_EXPERTS), F32)], axis=2)
    wr_hi = w_r.astype(BF16)
    wr_lo = (w_r - wr_hi.astype(F32)).astype(BF16)
    b_r = jnp.concatenate([b_r1, b_r2.reshape(DEPTH, N_EXPERTS),
                           jnp.zeros((DEPTH, ROUTE_LANES - 4 - N_EXPERTS), F32)], axis=1)
    zeros_p = jnp.zeros((DEPTH, POOL_GROUP, POOL_GROUP), F32)
    wp2 = jnp.stack([
        jnp.concatenate([jnp.concatenate([w_pool[:, 0], zeros_p], axis=2),
                         jnp.concatenate([zeros_p, w_pool[:, 1]], axis=2)], axis=1),
        jnp.concatenate([jnp.concatenate([w_pool[:, 2], zeros_p], axis=2),
                         jnp.concatenate([zeros_p, w_pool[:, 3]], axis=2)], axis=1)],
        axis=1).astype(BF16)
    params = {
        'w_glu': w_glu.astype(BF16), 'b_glu': b_glu.reshape(DEPTH, 1, D_SSM),
        'w_br_a': w_br_a.astype(BF16), 'w_br_b': w_br_b.astype(BF16), 'w_out': w_out.astype(BF16),
        'wr_hi': wr_hi, 'wr_lo': wr_lo, 'b_r': b_r.reshape(DEPTH, 1, ROUTE_LANES),
    }
    w1_bf, w3_bf, w2_bf = w1.astype(BF16), w3.astype(BF16), w2.astype(BF16)
    norm1 = norm1_g.reshape(DEPTH, 1, d)
    norm2 = norm2_g.reshape(DEPTH, 1, d)
    pscale = pool_scale.reshape(DEPTH, 1, D_POOL)
    dskip = jnp.tile(d_skip.reshape(DEPTH, N_SSM_GROUPS, 1, SSM_GROUP), (1, 1, 1, S5_CHUNK))

    x_ctx, x_lat, lat_off = ctx, x, 0
    out = None
    for l in range(DEPTH):
        last = l == DEPTH - 1
        u_t, ub, gates = _inproj(x_ctx, x_lat, lat_off, mod_sel, norm1, w_in_bf, l, batch,
                                 n_ctx_tiles, n_all_tiles)
        mcat, wy, coef = _s5_matrices(lam_re[l], lam_im[l], log_dt[l], b_re[l], b_im[l],
                                      c_re[l], c_im[l])
        y_t = _s5(u_t, mcat, wy, coef, dskip[l], batch, n_chunks, CTX_LEN // S5_CHUNK)
        yb = _pool(ub, wp2, pscale, l, not last)

        tile_off = n_ctx_tiles if last else 0
        n_tiles = n_lat_tiles if last else n_all_tiles
        x1, h2, route = _merge(x_ctx, x_lat, lat_off, y_t, yb, gates, mod_sel, norm2, params, l,
                               batch, tile_off, n_tiles, n_ctx_tiles)
        n_tok = n_tiles * TILE_TOK
        route = route.reshape(8, n_tiles, batch, TILE_TOK).transpose(0, 2, 1, 3).reshape(8, -1)
        cls = route[0].astype(jnp.int32)
        dest, src, wts, e_lo, e_hi, active = _dispatch_plan(cls, route[1], route[2])
        xs = _gather_rows(h2.reshape(batch * n_tok, d), src)
        fs = _experts(xs, wts, e_lo, e_hi, active, w1_bf, w3_bf, w2_bf, l)
        f = _gather_rows(fs, dest).reshape(batch, n_tok, d)
        res = _combine(x1, f, mod_sel, final_g.reshape(1, d), l, tile_off, n_ctx_tiles, last)
        if last:
            out = res
        else:
            x_ctx, x_lat, lat_off = res, res, n_ctx_tiles
    return out
```

```python
import functools

import numpy as np
import jax
import jax.numpy as jnp
from jax import lax
from jax.experimental import pallas as pl
from jax.experimental.pallas import tpu as pltpu

F32 = jnp.float32
BF16 = jnp.bfloat16

D_MODEL = 1024
DEPTH = 2
GRID_W = 64
CTX_LEN = 256
EPS = 1e-6

D_SSM = 256
SSM_GROUP = 16
N_SSM_GROUPS = 16
SSM_STATE = 64
S5_CHUNK = 16

D_POOL = 768
POOL_WINDOWS = (2, 4, 8, 16)
POOL_GROUP = 192
POOL_SEG = (0, 128, 384, 512)
GATE_OFF = D_SSM + D_POOL
D_IN = D_SSM + D_POOL + 2 * D_MODEL

N_EXPERT_GROUPS = 4
EXPERTS_PER_GROUP = 4
N_EXPERTS = 16
D_EXPERT = 512
N_PAIRS = 6
N_CLASSES = N_EXPERT_GROUPS * N_PAIRS
ROUTE_LANES = 128

TILE_TOK = 32
ROW_PITCH = TILE_TOK + 4
POOL_BLOCK = 256
EXPERT_TILE = 256
META_LANES = 256
V7X_VMEM_LIMIT = 48 * 1024 * 1024


def _cparams(*sem):
    return pltpu.CompilerParams(dimension_semantics=sem, vmem_limit_bytes=V7X_VMEM_LIMIT)


def _split_bf16(v):
    hi = v.astype(BF16)
    lo = (v - hi.astype(F32)).astype(BF16)
    return hi, lo


def _mod_kernel(c_ref, w_ref, b_ref, o_ref):
    c = c_ref[...]
    a = c * jax.nn.sigmoid(c)
    a_hi, a_lo = _split_bf16(a)
    w_hi, w_lo = _split_bf16(w_ref[...])
    acc = jnp.dot(a_hi, w_hi, preferred_element_type=F32)
    acc += jnp.dot(a_lo, w_hi, preferred_element_type=F32)
    acc += jnp.dot(a_hi, w_lo, preferred_element_type=F32)
    o_ref[...] = acc + b_ref[...]


def _modulation(cc, w_mod, b_mod):
    rows = cc.shape[0]
    nblk = 4
    cols = 6 * D_MODEL // nblk
    return pl.pallas_call(
        _mod_kernel,
        out_shape=jax.ShapeDtypeStruct((DEPTH, rows, 6 * D_MODEL), F32),
        grid=(DEPTH, nblk),
        in_specs=[
            pl.BlockSpec((rows, D_MODEL), lambda l, j: (0, 0)),
            pl.BlockSpec((None, D_MODEL, cols), lambda l, j: (l, 0, j)),
            pl.BlockSpec((None, 1, cols), lambda l, j: (l, 0, j)),
        ],
        out_specs=pl.BlockSpec((None, rows, cols), lambda l, j: (l, 0, j)),
        compiler_params=_cparams("arbitrary", "arbitrary"),
        name="modulation",
    )(cc, w_mod, b_mod.reshape(DEPTH, 1, 6 * D_MODEL))


def _pick(ctx_ref, lat_ref, tile, n_ctx_tiles):
    return jnp.where(tile < n_ctx_tiles, ctx_ref[...], lat_ref[...])


def _src_specs(cols, batch, n_ctx_tiles, tile_off, lat_off):
    blk = (batch, TILE_TOK, cols)
    return [pl.BlockSpec(blk, lambda j: (0, jnp.minimum(j + tile_off, n_ctx_tiles - 1), 0)),
            pl.BlockSpec(blk, lambda j: (0, jnp.maximum(j + tile_off - n_ctx_tiles, 0) + lat_off, 0))]


def _mod_spec(batch, layer, n_ctx_tiles, tile_off):
    return pl.BlockSpec((None, None, batch, 1, 6 * D_MODEL),
                        lambda j: (layer, (j + tile_off >= n_ctx_tiles).astype(jnp.int32), 0, 0, 0))


def _lane_group(batch):
    return lax.broadcasted_iota(jnp.int32, (batch, 128), 1) // SSM_GROUP


def _to_chunk_layout(ua_scr, u_ref, batch):
    blk = _lane_group(batch)
    for cl in range(TILE_TOK // S5_CHUNK):
        for g in range(N_SSM_GROUPS):
            for q in range(2):
                acc = None
                for s in range(8 * q, 8 * q + 8):
                    src = ua_scr[g // 8, pl.ds(cl * S5_CHUNK + s, batch, stride=ROW_PITCH), :]
                    shift = ((s - g) % 8) * SSM_GROUP
                    v = pltpu.roll(src, shift, 1) if shift else src
                    acc = v if acc is None else jnp.where(blk == s % 8, v, acc)
                u_ref[g, cl * batch:(cl + 1) * batch, 128 * q:128 * q + 128] = acc


def _from_chunk_layout(y_ref, ya_scr, batch):
    blk = _lane_group(batch)
    for cl in range(TILE_TOK // S5_CHUNK):
        for s in range(S5_CHUNK):
            for p in range(2):
                acc = None
                for g in range(8 * p, 8 * p + 8):
                    src = y_ref[g, cl * batch:(cl + 1) * batch, 128 * (s // 8):128 * (s // 8) + 128]
                    shift = ((g - s) % 8) * SSM_GROUP
                    v = pltpu.roll(src, shift, 1) if shift else src
                    acc = v if acc is None else jnp.where(blk == g % 8, v, acc)
                ya_scr[p, pl.ds(cl * S5_CHUNK + s, batch, stride=ROW_PITCH), :] = acc


def _inproj_kernel(xc_ref, xl_ref, mod_ref, g_ref, w_ref, u_ref, ub_ref, gate_ref, h_scr, ua_scr, *,
                   batch, n_ctx_tiles):
    rows = batch * TILE_TOK
    x = _pick(xc_ref, xl_ref, pl.program_id(0), n_ctx_tiles)
    y = x * lax.rsqrt(jnp.mean(x * x, axis=-1, keepdims=True) + EPS) * g_ref[...]
    shift = mod_ref[:, :, 0:D_MODEL]
    scale = mod_ref[:, :, D_MODEL:2 * D_MODEL]
    h_scr[...] = (y * (1.0 + scale) + shift).reshape(rows, D_MODEL).astype(BF16)
    h = h_scr[...]
    ua = jnp.dot(h, w_ref[:, 0:D_SSM], preferred_element_type=F32)
    for b in range(batch):
        for p in range(2):
            ua_scr[p, b * ROW_PITCH:b * ROW_PITCH + TILE_TOK, :] = (
                ua[b * TILE_TOK:(b + 1) * TILE_TOK, 128 * p:128 * p + 128])
    ub = jnp.dot(h, w_ref[:, D_SSM:GATE_OFF], preferred_element_type=F32)
    ub_ref[...] = ub.reshape(batch, TILE_TOK, D_POOL)
    for k in range(2):
        lo = GATE_OFF + k * D_MODEL
        g = jnp.dot(h, w_ref[:, lo:lo + D_MODEL], preferred_element_type=F32)
        gate_ref[:, :, k * D_MODEL:(k + 1) * D_MODEL] = (
            jax.nn.sigmoid(g).astype(BF16).reshape(batch, TILE_TOK, D_MODEL))
    _to_chunk_layout(ua_scr, u_ref, batch)


def _inproj(x_ctx, x_lat, lat_off, mod_sel, norm_g, w_in_bf, layer, batch, n_ctx_tiles, n_tiles):
    rows = batch * TILE_TOK
    n_tok = n_tiles * TILE_TOK
    chunk_rows = batch * TILE_TOK // S5_CHUNK
    return pl.pallas_call(
        functools.partial(_inproj_kernel, batch=batch, n_ctx_tiles=n_ctx_tiles),
        out_shape=(jax.ShapeDtypeStruct((N_SSM_GROUPS, n_tiles * chunk_rows, D_SSM), F32),
                   jax.ShapeDtypeStruct((batch, n_tok, D_POOL), F32),
                   jax.ShapeDtypeStruct((batch, n_tok, 2 * D_MODEL), BF16)),
        grid=(n_tiles,),
        in_specs=_src_specs(D_MODEL, batch, n_ctx_tiles, 0, lat_off) + [
            _mod_spec(batch, layer, n_ctx_tiles, 0),
            pl.BlockSpec((None, 1, D_MODEL), lambda j: (layer, 0, 0)),
            pl.BlockSpec((None, D_MODEL, D_IN), lambda j: (layer, 0, 0)),
        ],
        out_specs=(pl.BlockSpec((N_SSM_GROUPS, chunk_rows, D_SSM), lambda j: (0, j, 0)),
                   pl.BlockSpec((batch, TILE_TOK, D_POOL), lambda j: (0, j, 0)),
                   pl.BlockSpec((batch, TILE_TOK, 2 * D_MODEL), lambda j: (0, j, 0))),
        scratch_shapes=[pltpu.VMEM((rows, D_MODEL), BF16),
                        pltpu.VMEM((2, batch * ROW_PITCH, 128), F32)],
        compiler_params=_cparams("arbitrary"),
        name="inproj",
    )(x_ctx, x_lat, mod_sel, norm_g, w_in_bf)


def _s5_matrices(lam_re, lam_im, log_dt, b_re, b_im, c_re, c_im):
    L, G, N, H = S5_CHUNK, N_SSM_GROUPS, SSM_STATE, SSM_GROUP
    lr, li = lam_re.astype(F32), lam_im.astype(F32)
    dt = jnp.exp(log_dt.astype(F32))[..., None]
    zr, zi = lr * dt, li * dt
    k = jnp.arange(L + 1, dtype=F32)[:, None, None, None]
    pm = jnp.exp(zr[None] * k)
    pr, pi = pm * jnp.cos(zi[None] * k), pm * jnp.sin(zi[None] * k)
    nr, ni = pr[1] - 1.0, pi[1]
    den = lr * lr + li * li
    fr = (nr * lr + ni * li) / den
    fi = (ni * lr - nr * li) / den
    br, bi = b_re.astype(F32), b_im.astype(F32)
    bbr = fr[..., None] * br - fi[..., None] * bi
    bbi = fr[..., None] * bi + fi[..., None] * br
    cr, ci = c_re.astype(F32), c_im.astype(F32)
    cpr = cr[None] * pr[:, :, :, None, :] - ci[None] * pi[:, :, :, None, :]
    cpi = cr[None] * pi[:, :, :, None, :] + ci[None] * pr[:, :, :, None, :]
    kern = (jnp.einsum('kdgan,dgnh->kdgah', cpr, bbr)
            - jnp.einsum('kdgan,dgnh->kdgah', cpi, bbi))
    s_idx = np.arange(L)[:, None]
    t_idx = np.arange(L)[None, :]
    lag_f = np.clip(t_idx - s_idx, 0, L - 1)
    lag_b = np.clip(s_idx - t_idx, 0, L - 1)
    mf = jnp.where((t_idx >= s_idx)[:, :, None, None, None], kern[:, 0][lag_f], 0.0)
    mb = jnp.where((s_idx >= t_idx)[:, :, None, None, None], kern[:, 1][lag_b], 0.0)
    m = (mf + mb).transpose(2, 0, 4, 1, 3).reshape(G, L * H, L * H)
    powers = (L - 1 - np.arange(L), np.arange(L))
    inj = []
    for d in range(2):
        p_r, p_i = pr[powers[d], d], pi[powers[d], d]
        wr = p_r[..., None] * bbr[d][None] - p_i[..., None] * bbi[d][None]
        wi = p_r[..., None] * bbi[d][None] + p_i[..., None] * bbr[d][None]
        inj.append((wr, wi))
    w_inj = jnp.concatenate([inj[0][0], inj[1][0], inj[0][1], inj[1][1]], axis=2)
    mcat = jnp.concatenate([m, w_inj.transpose(1, 0, 3, 2).reshape(G, L * H, 4 * N)], axis=2)
    rd = (np.arange(L) + 1, L - np.arange(L))
    w_rd = jnp.concatenate([cpr[rd[0], 0], cpr[rd[1], 1], -cpi[rd[0], 0], -cpi[rd[1], 1]],
                           axis=3)
    wy = w_rd.transpose(1, 3, 0, 2).reshape(G, 4 * N, L * H)
    a_r = jnp.concatenate([pr[L][0], pr[L][1]], axis=-1)
    a_i = jnp.concatenate([pi[L][0], pi[L][1]], axis=-1)
    coef = jnp.stack([a_r, a_i] + [jnp.zeros_like(a_r)] * 6, axis=1)
    return mcat.astype(BF16), wy.astype(BF16), coef


def _s5_kernel(u_ref, mcat_ref, wy_ref, coef_ref, dskip_ref, y_ref, s_scr, h_scr, *, batch,
               n_chunks, n_ctx_chunks):
    n = SSM_STATE
    rb = S5_CHUNK * batch
    n_rb = n_chunks // S5_CHUNK

    def inject(i, c):
        r0 = pl.multiple_of(i * rb, rb)
        u = u_ref[pl.ds(r0, rb), :]
        ub = u.astype(BF16)
        y_ref[pl.ds(r0, rb), :] = (jnp.dot(ub, mcat_ref[:, 0:D_SSM], preferred_element_type=F32)
                                   + dskip_ref[...] * u)
        s_scr[pl.ds(r0, rb), :] = jnp.dot(ub, mcat_ref[:, D_SSM:2 * D_SSM],
                                          preferred_element_type=F32)
        return c

    lax.fori_loop(0, n_rb, inject, 0)
    a_r, a_i = coef_ref[0:1, :], coef_ref[1:2, :]
    fwd_lanes = lax.broadcasted_iota(jnp.int32, (batch, 2 * n), 1) < n

    def step(k, carry):
        h_re, h_im = carry
        cb = jnp.where(k < n_ctx_chunks, n_ctx_chunks - 1 - k, n_chunks - 1 + n_ctx_chunks - k)
        rf = pl.multiple_of(k * batch, batch)
        rk = pl.multiple_of(cb * batch, batch)
        h_scr[pl.ds(rf, batch), 0:n] = h_re[:, 0:n]
        h_scr[pl.ds(rk, batch), n:2 * n] = h_re[:, n:2 * n]
        h_scr[pl.ds(rf, batch), 2 * n:3 * n] = h_im[:, 0:n]
        h_scr[pl.ds(rk, batch), 3 * n:4 * n] = h_im[:, n:2 * n]
        s_re = jnp.where(fwd_lanes, s_scr[pl.ds(rf, batch), 0:2 * n], s_scr[pl.ds(rk, batch), 0:2 * n])
        s_im = jnp.where(fwd_lanes, s_scr[pl.ds(rf, batch), 2 * n:4 * n],
                         s_scr[pl.ds(rk, batch), 2 * n:4 * n])
        return (a_r * h_re - a_i * h_im + s_re, a_r * h_im + a_i * h_re + s_im)

    zero = jnp.zeros((batch, 2 * n), F32)
    lax.fori_loop(0, n_chunks, step, (zero, zero), unroll=4)

    def readout(i, c):
        r0 = pl.multiple_of(i * rb, rb)
        y_ref[pl.ds(r0, rb), :] += jnp.dot(h_scr[pl.ds(r0, rb), :].astype(BF16), wy_ref[...],
                                           preferred_element_type=F32)
        return c

    lax.fori_loop(0, n_rb, readout, 0)


def _s5(u_t, mcat, wy, coef, dskip, batch, n_chunks, n_ctx_chunks):
    G = N_SSM_GROUPS
    rows = u_t.shape[1]
    kern = functools.partial(_s5_kernel, batch=batch, n_chunks=n_chunks, n_ctx_chunks=n_ctx_chunks)
    per_group = lambda *blk: pl.BlockSpec((None,) + blk, lambda g: (g, 0, 0))
    return pl.pallas_call(
        kern,
        out_shape=jax.ShapeDtypeStruct((G, rows, D_SSM), F32),
        grid=(G,),
        in_specs=[per_group(rows, D_SSM), per_group(D_SSM, 2 * D_SSM), per_group(D_SSM, D_SSM),
                  per_group(8, 2 * SSM_STATE), per_group(1, D_SSM)],
        out_specs=per_group(rows, D_SSM),
        scratch_shapes=[pltpu.VMEM((rows, D_SSM), F32), pltpu.VMEM((rows, D_SSM), F32)],
        compiler_params=_cparams("arbitrary"),
        name="s5",
    )(u_t, mcat, wy, coef, dskip)


def _window(w):
    return -(w // 2), w - 1 - w // 2


def _pool_constants(rows, width):
    rpb = POOL_BLOCK // width
    pm = np.zeros((4, POOL_BLOCK, POOL_BLOCK), np.float32)
    inv = np.zeros((rows * width, 4), np.float32)
    col = np.arange(width)
    row = np.arange(rows)
    for i, w in enumerate(POOL_WINDOWS):
        lo, hi = _window(w)
        c0, c1 = np.clip(col + lo, 0, width - 1), np.clip(col + hi, 0, width - 1)
        r0, r1 = np.clip(row + lo, 0, rows - 1), np.clip(row + hi, 0, rows - 1)
        band = ((col[None, :] >= c0[:, None]) & (col[None, :] <= c1[:, None])).astype(np.float32)
        for r in range(rpb):
            pm[i, r * width:(r + 1) * width, r * width:(r + 1) * width] = band
        cnt = (r1 - r0 + 1)[:, None] * (c1 - c0 + 1)[None, :]
        inv[:, i] = (1.0 / cnt).reshape(-1)
    return jnp.asarray(pm, BF16), jnp.asarray(inv, F32)


def _pool_segment(u_ref, o_ref, cs_scr, pm_ref, inv_ref, wp_ref, ps_ref, tok0, rows, width):
    n_tok = rows * width
    n_blk = n_tok // POOL_BLOCK
    pad = 8 * width if rows > 1 else 0
    if rows > 1:
        zeros = jnp.zeros((pad, 2 * 128), F32)
        for i in range(4):
            cs_scr[i, 0:pad, :] = zeros
            cs_scr[i, pad + n_tok:pad + n_tok + pad, :] = zeros
    for b in range(n_blk):
        t0 = b * POOL_BLOCK
        for i in range(4):
            seg = POOL_SEG[i]
            xb = u_ref[tok0 + t0:tok0 + t0 + POOL_BLOCK, seg:seg + 256].astype(BF16)
            cs_scr[i, pad + t0:pad + t0 + POOL_BLOCK, :] = jnp.dot(
                pm_ref[i], xb, preferred_element_type=F32)
    lane = lax.broadcasted_iota(jnp.int32, (POOL_BLOCK, 128), 1)
    low_half = lane < 64
    for b in range(n_blk):
        t0 = b * POOL_BLOCK
        win = []
        for i, w in enumerate(POOL_WINDOWS):
            lo, hi = _window(w) if rows > 1 else (0, 0)
            acc = None
            for k in range(lo, hi + 1):
                base = pad + t0 + k * width
                piece = cs_scr[i, base:base + POOL_BLOCK, :]
                acc = piece if acc is None else acc + piece
            win.append(acc * inv_ref[t0:t0 + POOL_BLOCK, i:i + 1])
        x = u_ref[tok0 + t0:tok0 + t0 + POOL_BLOCK, :]
        pooled = jnp.concatenate([
            win[0][:, 0:128],
            jnp.where(low_half, win[0][:, 128:256], win[1][:, 0:128]),
            win[1][:, 128:256],
            win[2][:, 0:128],
            jnp.where(low_half, win[2][:, 128:256], win[3][:, 0:128]),
            win[3][:, 128:256]], axis=1)
        d = (pooled - x).astype(BF16)
        y0 = jnp.dot(d[:, 0:384], wp_ref[0], preferred_element_type=F32)
        y1 = jnp.dot(d[:, 384:768], wp_ref[1], preferred_element_type=F32)
        y = jnp.concatenate([y0, y1], axis=1) * ps_ref[...]
        o_ref[tok0 + t0:tok0 + t0 + POOL_BLOCK, :] = y.astype(BF16)


def _pool_kernel(u_ref, pmc_ref, invc_ref, pml_ref, invl_ref, wp_ref, ps_ref, o_ref, cs_scr, *,
                 lat_rows, with_ctx):
    if with_ctx:
        _pool_segment(u_ref, o_ref, cs_scr, pmc_ref, invc_ref, wp_ref, ps_ref, 0, 1, CTX_LEN)
    else:
        o_ref[0:CTX_LEN, :] = jnp.zeros((CTX_LEN, D_POOL), BF16)
    _pool_segment(u_ref, o_ref, cs_scr, pml_ref, invl_ref, wp_ref, ps_ref, CTX_LEN, lat_rows, GRID_W)


def _pool(ub, wp2, pscale, layer, with_ctx):
    batch, n_tok, _ = ub.shape
    lat_rows = (n_tok - CTX_LEN) // GRID_W
    pmc, invc = _pool_constants(1, CTX_LEN)
    pml, invl = _pool_constants(lat_rows, GRID_W)
    const = lambda *blk: pl.BlockSpec(blk, lambda b: (0,) * len(blk))
    return pl.pallas_call(
        functools.partial(_pool_kernel, lat_rows=lat_rows, with_ctx=with_ctx),
        out_shape=jax.ShapeDtypeStruct((batch, n_tok, D_POOL), BF16),
        grid=(batch,),
        in_specs=[
            pl.BlockSpec((None, n_tok, D_POOL), lambda b: (b, 0, 0)),
            const(4, POOL_BLOCK, POOL_BLOCK), const(CTX_LEN, 4),
            const(4, POOL_BLOCK, POOL_BLOCK), const(lat_rows * GRID_W, 4),
            pl.BlockSpec((None, 2, 384, 384), lambda b: (layer, 0, 0, 0)),
            pl.BlockSpec((None, 1, D_POOL), lambda b: (layer, 0, 0)),
        ],
        out_specs=pl.BlockSpec((None, n_tok, D_POOL), lambda b: (b, 0, 0)),
        scratch_shapes=[pltpu.VMEM((4, lat_rows * GRID_W + 16 * GRID_W, 256), F32)],
        compiler_params=_cparams("arbitrary"),
        name="pool",
    )(ub, pmc, invc, pml, invl, wp2, pscale)


def _first_argmax(rows_):
    best, idx = rows_[0], jnp.zeros_like(rows_[0], dtype=jnp.int32)
    for k in range(1, len(rows_)):
        take = rows_[k] > best
        idx = jnp.where(take, k, idx)
        best = jnp.where(take, rows_[k], best)
    return best, idx


def _route(lt):
    g_rows = [lt[k:k + 1, :] for k in range(N_EXPERT_GROUPS)]
    best, grp = _first_argmax(g_rows)
    denom = sum(jnp.exp(r - best) for r in g_rows)
    p_grp = 1.0 / denom
    inner = []
    for e in range(EXPERTS_PER_GROUP):
        acc = jnp.zeros_like(best)
        for g in range(N_EXPERT_GROUPS):
            r = 4 + 4 * g + e
            acc = jnp.where(grp == g, lt[r:r + 1, :], acc)
        inner.append(acc)
    v1, i1 = _first_argmax(inner)
    masked = [jnp.where(i1 == e, -jnp.inf, inner[e]) for e in range(EXPERTS_PER_GROUP)]
    v2, i2 = _first_argmax(masked)
    e21 = jnp.exp(v2 - v1)
    w1 = p_grp / (1.0 + e21)
    w2 = p_grp * e21 / (1.0 + e21)
    first_low = i1 < i2
    lo = jnp.where(first_low, i1, i2)
    hi = jnp.where(first_low, i2, i1)
    w_lo = jnp.where(first_low, w1, w2)
    w_hi = jnp.where(first_low, w2, w1)
    off = jnp.where(lo == 0, 0, jnp.where(lo == 1, 3, 5))
    cls = N_PAIRS * grp + off + hi - lo - 1
    return cls.astype(F32), w_lo, w_hi


def _merge_kernel(xc_ref, xl_ref, y_ref, yb_ref, gate_ref, mod_ref, g2_ref, wglu_ref, bglu_ref,
                  wbra_ref, wbrb_ref, wout_ref, wrh_ref, wrl_ref, br_ref, x1_ref, h2_ref, route_ref,
                  ya_scr, *, batch, n_ctx_tiles, tile_off):
    D = D_MODEL
    rows = batch * TILE_TOK
    tile = pl.program_id(0) + tile_off
    _from_chunk_layout(y_ref, ya_scr, batch)
    y = jnp.concatenate(
        [jnp.concatenate([ya_scr[p, b * ROW_PITCH:b * ROW_PITCH + TILE_TOK, :] for p in range(2)],
                         axis=1) for b in range(batch)], axis=0)
    z = jax.nn.gelu(y)
    glu = jax.nn.sigmoid(jnp.dot(z.astype(BF16), wglu_ref[...], preferred_element_type=F32)
                         + bglu_ref[...])
    ya = (z * glu).astype(BF16)
    gates = gate_ref[...].reshape(rows, 2 * D)
    m = gates[:, 0:D].astype(F32) * jnp.dot(ya, wbra_ref[...], preferred_element_type=F32)
    m += gates[:, D:2 * D].astype(F32) * jnp.dot(yb_ref[...].reshape(rows, D_POOL), wbrb_ref[...],
                                                 preferred_element_type=F32)
    out = jnp.dot(m.astype(BF16), wout_ref[...], preferred_element_type=F32)
    x_in = _pick(xc_ref, xl_ref, tile, n_ctx_tiles)
    x1 = x_in + mod_ref[:, :, 2 * D:3 * D] * out.reshape(batch, TILE_TOK, D)
    x1_ref[...] = x1
    yn = x1 * lax.rsqrt(jnp.mean(x1 * x1, axis=-1, keepdims=True) + EPS) * g2_ref[...]
    h2 = yn * (1.0 + mod_ref[:, :, 4 * D:5 * D]) + mod_ref[:, :, 3 * D:4 * D]
    h2_ref[...] = h2.reshape(rows, D)
    h_hi, h_lo = _split_bf16(h2.reshape(rows, D))
    logits = jnp.dot(h_hi, wrh_ref[...], preferred_element_type=F32)
    logits += jnp.dot(h_lo, wrh_ref[...], preferred_element_type=F32)
    logits += jnp.dot(h_hi, wrl_ref[...], preferred_element_type=F32)
    logits += br_ref[...]
    cls, w_lo, w_hi = _route(logits.T)
    zero = jnp.zeros_like(cls)
    route_ref[...] = jnp.concatenate([cls, w_lo, w_hi, zero, zero, zero, zero, zero], axis=0)


def _merge(x_ctx, x_lat, lat_off, y_t, yb, gates, mod_sel, norm2_g, p, layer, batch, tile_off,
           n_tiles, n_ctx_tiles):
    rows = batch * TILE_TOK
    chunk_rows = rows // S5_CHUNK
    tok = lambda cols: pl.BlockSpec((batch, TILE_TOK, cols), lambda j: (0, j + tile_off, 0))
    lay3 = lambda *blk: pl.BlockSpec((None,) + blk, lambda j: (layer, 0, 0))
    n_tok = n_tiles * TILE_TOK
    return pl.pallas_call(
        functools.partial(_merge_kernel, batch=batch, n_ctx_tiles=n_ctx_tiles, tile_off=tile_off),
        out_shape=(jax.ShapeDtypeStruct((batch, n_tok, D_MODEL), F32),
                   jax.ShapeDtypeStruct((n_tiles * rows, D_MODEL), F32),
                   jax.ShapeDtypeStruct((8, n_tiles * rows), F32)),
        grid=(n_tiles,),
        in_specs=_src_specs(D_MODEL, batch, n_ctx_tiles, tile_off, lat_off) + [
            pl.BlockSpec((N_SSM_GROUPS, chunk_rows, D_SSM), lambda j: (0, j + tile_off, 0)),
            tok(D_POOL), tok(2 * D_MODEL),
            _mod_spec(batch, layer, n_ctx_tiles, tile_off),
            lay3(1, D_MODEL), lay3(D_SSM, D_SSM), lay3(1, D_SSM), lay3(D_SSM, D_MODEL),
            lay3(D_POOL, D_MODEL), lay3(D_MODEL, D_MODEL), lay3(D_MODEL, ROUTE_LANES),
            lay3(D_MODEL, ROUTE_LANES), lay3(1, ROUTE_LANES),
        ],
        out_specs=(pl.BlockSpec((batch, TILE_TOK, D_MODEL), lambda j: (0, j, 0)),
                   pl.BlockSpec((rows, D_MODEL), lambda j: (j, 0)),
                   pl.BlockSpec((8, rows), lambda j: (0, j))),
        scratch_shapes=[pltpu.VMEM((2, batch * ROW_PITCH, 128), F32)],
        compiler_params=_cparams("arbitrary"),
        name="merge",
    )(x_ctx, x_lat, y_t, yb, gates, mod_sel, norm2_g, p['w_glu'], p['b_glu'], p['w_br_a'],
      p['w_br_b'], p['w_out'], p['wr_hi'], p['wr_lo'], p['b_r'])


def _plan_kernel(cls_ref, dest_ref, src_ref, meta_ref, dest_smem, fill_scr, sem, *, n_q):
    rows_q = n_q // 128
    cls = cls_ref[...]
    r_i = lax.broadcasted_iota(jnp.int32, (128, 128), 0)
    c_i = lax.broadcasted_iota(jnp.int32, (128, 128), 1)
    upper = jnp.where(r_i <= c_i, 1.0, 0.0).astype(BF16)
    r_q = lax.broadcasted_iota(jnp.int32, (rows_q, rows_q), 0)
    c_q = lax.broadcasted_iota(jnp.int32, (rows_q, rows_q), 1)
    lower = jnp.where(c_q < r_q, 1.0, 0.0).astype(BF16)
    tile_start = (lax.broadcasted_iota(jnp.int32, (1, META_LANES), 1) * EXPERT_TILE).astype(F32)
    start = jnp.zeros((1, 128), F32)
    dest = jnp.zeros((rows_q, 128), F32)
    tile_cls = jnp.zeros((1, META_LANES), F32)
    for c in range(N_CLASSES):
        m = cls == float(c)
        incl = jnp.dot(jnp.where(m, 1.0, 0.0).astype(BF16), upper, preferred_element_type=F32)
        row_tot = jnp.broadcast_to(incl[:, 127:128], (rows_q, 128))
        before = jnp.dot(lower, row_tot.astype(BF16), preferred_element_type=F32)
        dest = jnp.where(m, start + before + incl - 1.0, dest)
        total = before[rows_q - 1:rows_q, :] + row_tot[rows_q - 1:rows_q, :]
        start = start + jnp.floor((total + (EXPERT_TILE - 1.0)) * (1.0 / EXPERT_TILE)) * EXPERT_TILE
        end2 = jnp.concatenate([start, start], axis=1)
        tile_cls = tile_cls + jnp.where(tile_start >= end2, 1.0, 0.0)
    active = jnp.where(tile_start < end2, 1.0, 0.0)
    tc = jnp.minimum(tile_cls, N_CLASSES - 1.0)
    grp = sum(jnp.where(tc >= float(N_PAIRS * k), 1.0, 0.0) for k in range(1, N_EXPERT_GROUPS))
    pair = tc - N_PAIRS * grp
    p_lo = jnp.where(pair >= 3.0, 1.0, 0.0) + jnp.where(pair >= 5.0, 1.0, 0.0)
    p_hi = jnp.where(pair == 0.0, 1.0, jnp.where(pair == 1.0, 2.0, jnp.where(pair == 3.0, 2.0, 3.0)))
    zero = jnp.zeros_like(tc)
    meta_ref[...] = jnp.concatenate(
        [EXPERTS_PER_GROUP * grp + p_lo, EXPERTS_PER_GROUP * grp + p_hi, active,
         zero, zero, zero, zero, zero], axis=0).astype(jnp.int32)
    dest_ref[...] = dest.astype(jnp.int32)

    fill_scr[...] = jnp.full(fill_scr.shape, n_q, jnp.int32)
    fill = pltpu.make_async_copy(fill_scr, src_ref, sem.at[0])
    stage = pltpu.make_async_copy(dest_ref, dest_smem, sem.at[1])
    fill.start()
    stage.start()
    fill.wait()
    stage.wait()

    def invert(i, carry):
        for k in range(128):
            d = dest_smem[i, k]
            src_ref[d >> 7, d & 127] = i * 128 + k
        return carry

    lax.fori_loop(0, rows_q, invert, 0)


def _plan(cls_q):
    n_q = cls_q.shape[0]
    rows_q = n_q // 128
    n_tiles = n_q // EXPERT_TILE + N_CLASSES
    assert n_tiles <= META_LANES
    rows_p = n_tiles * EXPERT_TILE // 128
    i32 = jnp.int32
    return pl.pallas_call(
        functools.partial(_plan_kernel, n_q=n_q),
        out_shape=(jax.ShapeDtypeStruct((rows_q, 128), i32), jax.ShapeDtypeStruct((rows_p, 128), i32),
                   jax.ShapeDtypeStruct((8, META_LANES), i32)),
        in_specs=[pl.BlockSpec(memory_space=pltpu.VMEM)],
        out_specs=(pl.BlockSpec(memory_space=pltpu.VMEM), pl.BlockSpec(memory_space=pltpu.SMEM),
                   pl.BlockSpec(memory_space=pltpu.VMEM)),
        scratch_shapes=[pltpu.SMEM((rows_q, 128), i32), pltpu.VMEM((rows_p, 128), i32),
                        pltpu.SemaphoreType.DMA((2,))],
        compiler_params=pltpu.CompilerParams(vmem_limit_bytes=V7X_VMEM_LIMIT),
        name="plan",
    )(cls_q.reshape(rows_q, 128))


def _start_row_gather(idx_ref, idx_row0, src_hbm, dst_ref, sem, max_row=None):
    for r in range(dst_ref.shape[0]):
        row = idx_ref[idx_row0 + r // 128, r % 128]
        if max_row is not None:
            row = jnp.minimum(row, max_row)
        pltpu.make_async_copy(src_hbm.at[pl.ds(row, 1)], dst_ref.at[pl.ds(r, 1)], sem).start()


def _wait_row_gather(src_hbm, dst_ref, sem):
    pltpu.make_async_copy(src_hbm.at[pl.ds(0, dst_ref.shape[0])], dst_ref, sem).wait()


def _expert_kernel(src_ref, elo_ref, ehi_ref, act_ref, h2_hbm, wt_ref, w1a_ref, w1b_ref, w3a_ref,
                   w3b_ref, w2a_ref, w2b_ref, o_ref, x_buf, sem, *, n_q):
    j = pl.program_id(0)
    last = pl.num_programs(0) - 1
    slot = j % 2
    idx_rows = EXPERT_TILE // 128

    @pl.when(j == 0)
    def _():
        _start_row_gather(src_ref, 0, h2_hbm, x_buf.at[0], sem.at[0], n_q - 1)

    _wait_row_gather(h2_hbm, x_buf.at[slot], sem.at[slot])
    _start_row_gather(src_ref, jnp.minimum(j + 1, last) * idx_rows, h2_hbm, x_buf.at[1 - slot],
                      sem.at[1 - slot], n_q - 1)

    @pl.when(act_ref[j] == 1)
    def _():
        x = x_buf[slot].astype(BF16)
        acc = None
        for k, (w1, w3, w2) in enumerate(((w1a_ref, w3a_ref, w2a_ref), (w1b_ref, w3b_ref, w2b_ref))):
            a = jnp.dot(x, w1[...], preferred_element_type=F32)
            b = jnp.dot(x, w3[...], preferred_element_type=F32)
            h = (a * jax.nn.sigmoid(a) * b * wt_ref[:, k:k + 1]).astype(BF16)
            o = jnp.dot(h, w2[...], preferred_element_type=F32)
            acc = o if acc is None else acc + o
        o_ref[...] = acc

    @pl.when(act_ref[j] == 0)
    def _():
        o_ref[...] = jnp.zeros_like(o_ref)

    @pl.when(j == last)
    def _():
        _wait_row_gather(h2_hbm, x_buf.at[1 - slot], sem.at[1 - slot])


def _experts(src, e_lo, e_hi, active, h2, wts, w1, w3, w2, layer):
    n_q = h2.shape[0]
    n_rows = wts.shape[0]
    te = EXPERT_TILE
    up = lambda sel: pl.BlockSpec((None, None, D_MODEL, D_EXPERT),
                                  lambda j, s, lo, hi, act: (layer, (lo, hi)[sel][j], 0, 0))
    down = lambda sel: pl.BlockSpec((None, None, D_EXPERT, D_MODEL),
                                    lambda j, s, lo, hi, act: (layer, (lo, hi)[sel][j], 0, 0))
    return pl.pallas_call(
        functools.partial(_expert_kernel, n_q=n_q),
        out_shape=jax.ShapeDtypeStruct((n_rows, D_MODEL), F32),
        grid_spec=pltpu.PrefetchScalarGridSpec(
            num_scalar_prefetch=4,
            grid=(n_rows // te,),
            in_specs=[
                pl.BlockSpec(memory_space=pl.ANY),
                pl.BlockSpec((te, 2), lambda j, s, lo, hi, act: (j, 0)),
                up(0), up(1), up(0), up(1), down(0), down(1),
            ],
            out_specs=pl.BlockSpec((te, D_MODEL), lambda j, s, lo, hi, act: (j, 0)),
            scratch_shapes=[pltpu.VMEM((2, te, D_MODEL), F32), pltpu.SemaphoreType.DMA((2,))],
        ),
        compiler_params=_cparams("arbitrary"),
        name="experts",
    )(src, e_lo, e_hi, active, h2, wts, w1, w1, w3, w3, w2, w2)


def _combine_kernel(dest_ref, x_ref, fs_hbm, mod_ref, g_ref, o_ref, f_buf, sem, *, batch, final):
    j = pl.program_id(0)
    last = pl.num_programs(0) - 1
    slot = j % 2
    idx_rows = batch * TILE_TOK // 128

    @pl.when(j == 0)
    def _():
        _start_row_gather(dest_ref, 0, fs_hbm, f_buf.at[0], sem.at[0])

    _wait_row_gather(fs_hbm, f_buf.at[slot], sem.at[slot])
    _start_row_gather(dest_ref, jnp.minimum(j + 1, last) * idx_rows, fs_hbm, f_buf.at[1 - slot],
                      sem.at[1 - slot])
    f = f_buf[slot].reshape(batch, TILE_TOK, D_MODEL)
    x = x_ref[...] + mod_ref[:, :, 5 * D_MODEL:6 * D_MODEL] * f
    if final:
        x = x * lax.rsqrt(jnp.mean(x * x, axis=-1, keepdims=True) + EPS) * g_ref[...]
    o_ref[...] = x

    @pl.when(j == last)
    def _():
        _wait_row_gather(fs_hbm, f_buf.at[1 - slot], sem.at[1 - slot])


def _combine(dest, x1, fs, mod_sel, final_g, layer, tile_off, n_ctx_tiles, final):
    batch, n_tok, _ = x1.shape
    rows = batch * TILE_TOK
    tok = pl.BlockSpec((batch, TILE_TOK, D_MODEL), lambda j, d: (0, j, 0))
    mod_spec = _mod_spec(batch, layer, n_ctx_tiles, tile_off)
    return pl.pallas_call(
        functools.partial(_combine_kernel, batch=batch, final=final),
        out_shape=jax.ShapeDtypeStruct((batch, n_tok, D_MODEL), F32),
        grid_spec=pltpu.PrefetchScalarGridSpec(
            num_scalar_prefetch=1,
            grid=(n_tok // TILE_TOK,),
            in_specs=[tok, pl.BlockSpec(memory_space=pl.ANY),
                      pl.BlockSpec(mod_spec.block_shape, lambda j, d: mod_spec.index_map(j)),
                      pl.BlockSpec((1, D_MODEL), lambda j, d: (0, 0))],
            out_specs=tok,
            scratch_shapes=[pltpu.VMEM((2, rows, D_MODEL), F32), pltpu.SemaphoreType.DMA((2,))],
        ),
        compiler_params=_cparams("arbitrary"),
        name="combine",
    )(dest, x1, fs, mod_sel, final_g)


def kernel(x, c, ctx, c_ctx, norm1_g, norm2_g, w_mod, b_mod, w_in, lam_re, lam_im, log_dt, b_re,
           b_im, c_re, c_im, d_skip, w_glu, b_glu, w_pool, pool_scale, w_br_a, w_br_b, w_out, w_r1,
           b_r1, w_r2, b_r2, w1, w3, w2, final_g):
    batch, seq, d = x.shape
    assert d == D_MODEL and seq % (GRID_W * 8) == 0 and ctx.shape[1] == CTX_LEN and batch % 8 == 0
    n_ctx_tiles, n_lat_tiles = CTX_LEN // TILE_TOK, seq // TILE_TOK
    n_all_tiles = n_ctx_tiles + n_lat_tiles
    n_chunks = (CTX_LEN + seq) // S5_CHUNK
    mod_rows = ((batch + 1 + 7) // 8) * 8

    cc = jnp.concatenate([c, c_ctx[None, :], jnp.zeros((mod_rows - batch - 1, d), F32)], axis=0)
    mod = _modulation(cc, w_mod, b_mod)
    mod_sel = jnp.stack([jnp.broadcast_to(mod[:, batch:batch + 1], (DEPTH, batch, 6 * d)),
                         mod[:, :batch]], axis=1)[:, :, :, None, :]

    w_in_bf = w_in.astype(BF16)
    w_r = jnp.concatenate([w_r1, w_r2.transpose(0, 2, 1, 3).reshape(DEPTH, d, N_EXPERTS),
                           jnp.zeros((DEPTH, d, ROUTE_LANES - 4 - N_EXPERTS), F32)], axis=2)
    wr_hi = w_r.astype(BF16)
    wr_lo = (w_r - wr_hi.astype(F32)).astype(BF16)
    b_r = jnp.concatenate([b_r1, b_r2.reshape(DEPTH, N_EXPERTS),
                           jnp.zeros((DEPTH, ROUTE_LANES - 4 - N_EXPERTS), F32)], axis=1)
    zeros_p = jnp.zeros((DEPTH, POOL_GROUP, POOL_GROUP), F32)
    wp2 = jnp.stack([
        jnp.concatenate([jnp.concatenate([w_pool[:, 0], zeros_p], axis=2),
                         jnp.concatenate([zeros_p, w_pool[:, 1]], axis=2)], axis=1),
        jnp.concatenate([jnp.concatenate([w_pool[:, 2], zeros_p], axis=2),
                         jnp.concatenate([zeros_p, w_pool[:, 3]], axis=2)], axis=1)],
        axis=1).astype(BF16)
    params = {
        'w_glu': w_glu.astype(BF16), 'b_glu': b_glu.reshape(DEPTH, 1, D_SSM),
        'w_br_a': w_br_a.astype(BF16), 'w_br_b': w_br_b.astype(BF16), 'w_out': w_out.astype(BF16),
        'wr_hi': wr_hi, 'wr_lo': wr_lo, 'b_r': b_r.reshape(DEPTH, 1, ROUTE_LANES),
    }
    w1_bf, w3_bf, w2_bf = w1.astype(BF16), w3.astype(BF16), w2.astype(BF16)
    norm1 = norm1_g.reshape(DEPTH, 1, d)
    norm2 = norm2_g.reshape(DEPTH, 1, d)
    pscale = pool_scale.reshape(DEPTH, 1, D_POOL)
    dskip = jnp.tile(d_skip.reshape(DEPTH, N_SSM_GROUPS, 1, SSM_GROUP), (1, 1, 1, S5_CHUNK))

    x_ctx, x_lat, lat_off = ctx, x, 0
    out = None
    for l in range(DEPTH):
        last = l == DEPTH - 1
        u_t, ub, gates = _inproj(x_ctx, x_lat, lat_off, mod_sel, norm1, w_in_bf, l, batch,
                                 n_ctx_tiles, n_all_tiles)
        mcat, wy, coef = _s5_matrices(lam_re[l], lam_im[l], log_dt[l], b_re[l], b_im[l],
                                      c_re[l], c_im[l])
        y_t = _s5(u_t, mcat, wy, coef, dskip[l], batch, n_chunks, CTX_LEN // S5_CHUNK)
        yb = _pool(ub, wp2, pscale, l, not last)

        tile_off = n_ctx_tiles if last else 0
        n_tiles = n_lat_tiles if last else n_all_tiles
        x1, h2, route = _merge(x_ctx, x_lat, lat_off, y_t, yb, gates, mod_sel, norm2, params, l,
                               batch, tile_off, n_tiles, n_ctx_tiles)
        dest, src, meta = _plan(route[0])
        n_exp_tiles = src.shape[0] * 128 // EXPERT_TILE
        wts_q = jnp.concatenate([route[1:3].T, jnp.zeros((1, 2), F32)], axis=0)
        wts = wts_q[src.reshape(-1)]
        fs = _experts(src, meta[0, :n_exp_tiles], meta[1, :n_exp_tiles], meta[2, :n_exp_tiles], h2,
                      wts, w1_bf, w3_bf, w2_bf, l)
        res = _combine(dest, x1, fs, mod_sel, final_g.reshape(1, d), l, tile_off, n_ctx_tiles, last)
        if last:
            out = res
        else:
            x_ctx, x_lat, lat_off = res, res, n_ctx_tiles
    return out
```

```python
import functools

import numpy as np
import jax
import jax.numpy as jnp
from jax import lax
from jax.experimental import pallas as pl
from jax.experimental.pallas import tpu as pltpu

F32 = jnp.float32
BF16 = jnp.bfloat16

D_MODEL = 1024
DEPTH = 2
GRID_W = 64
CTX_LEN = 256
EPS = 1e-6

D_SSM = 256
SSM_GROUP = 16
N_SSM_GROUPS = 16
SSM_STATE = 64
S5_CHUNK = 16

D_POOL = 768
POOL_WINDOWS = (2, 4, 8, 16)
POOL_GROUP = 192
POOL_SEG = (0, 128, 384, 512)
GATE_OFF = D_SSM + D_POOL
D_IN = D_SSM + D_POOL + 2 * D_MODEL

N_EXPERT_GROUPS = 4
EXPERTS_PER_GROUP = 4
N_EXPERTS = 16
D_EXPERT = 512
N_PAIRS = 6
N_CLASSES = N_EXPERT_GROUPS * N_PAIRS
ROUTE_LANES = 128

TILE_TOK = 32
ROW_PITCH = TILE_TOK + 4
POOL_BLOCK = 256
EXPERT_TILE = 256
META_LANES = 256
V7X_VMEM_LIMIT = 48 * 1024 * 1024


def _cparams(*sem):
    return pltpu.CompilerParams(dimension_semantics=sem, vmem_limit_bytes=V7X_VMEM_LIMIT)


def _split_bf16(v):
    hi = v.astype(BF16)
    lo = (v - hi.astype(F32)).astype(BF16)
    return hi, lo


def _mod_kernel(c_ref, w_ref, b_ref, o_ref):
    c = c_ref[...]
    a = c * jax.nn.sigmoid(c)
    a_hi, a_lo = _split_bf16(a)
    w_hi, w_lo = _split_bf16(w_ref[...])
    acc = jnp.dot(a_hi, w_hi, preferred_element_type=F32)
    acc += jnp.dot(a_lo, w_hi, preferred_element_type=F32)
    acc += jnp.dot(a_hi, w_lo, preferred_element_type=F32)
    o_ref[...] = acc + b_ref[...]


def _modulation(cc, w_mod, b_mod):
    rows = cc.shape[0]
    nblk = 4
    cols = 6 * D_MODEL // nblk
    return pl.pallas_call(
        _mod_kernel,
        out_shape=jax.ShapeDtypeStruct((DEPTH, rows, 6 * D_MODEL), F32),
        grid=(DEPTH, nblk),
        in_specs=[
            pl.BlockSpec((rows, D_MODEL), lambda l, j: (0, 0)),
            pl.BlockSpec((None, D_MODEL, cols), lambda l, j: (l, 0, j)),
            pl.BlockSpec((None, 1, cols), lambda l, j: (l, 0, j)),
        ],
        out_specs=pl.BlockSpec((None, rows, cols), lambda l, j: (l, 0, j)),
        compiler_params=_cparams("arbitrary", "arbitrary"),
        name="modulation",
    )(cc, w_mod, b_mod.reshape(DEPTH, 1, 6 * D_MODEL))


def _pick(ctx_ref, lat_ref, tile, n_ctx_tiles):
    return jnp.where(tile < n_ctx_tiles, ctx_ref[...], lat_ref[...])


def _src_specs(cols, batch, n_ctx_tiles, tile_off, lat_off):
    blk = (batch, TILE_TOK, cols)
    return [pl.BlockSpec(blk, lambda j: (0, jnp.minimum(j + tile_off, n_ctx_tiles - 1), 0)),
            pl.BlockSpec(blk, lambda j: (0, jnp.maximum(j + tile_off - n_ctx_tiles, 0) + lat_off, 0))]


def _mod_spec(batch, layer, n_ctx_tiles, tile_off):
    return pl.BlockSpec((None, None, batch, 1, 6 * D_MODEL),
                        lambda j: (layer, (j + tile_off >= n_ctx_tiles).astype(jnp.int32), 0, 0, 0))


def _lane_group(batch):
    return lax.broadcasted_iota(jnp.int32, (batch, 128), 1) // SSM_GROUP


def _to_chunk_layout(ua_scr, u_ref, batch):
    blk = _lane_group(batch)
    for cl in range(TILE_TOK // S5_CHUNK):
        for g in range(N_SSM_GROUPS):
            for q in range(2):
                acc = None
                for s in range(8 * q, 8 * q + 8):
                    src = ua_scr[g // 8, pl.ds(cl * S5_CHUNK + s, batch, stride=ROW_PITCH), :]
                    shift = ((s - g) % 8) * SSM_GROUP
                    v = pltpu.roll(src, shift, 1) if shift else src
                    acc = v if acc is None else jnp.where(blk == s % 8, v, acc)
                u_ref[g, cl * batch:(cl + 1) * batch, 128 * q:128 * q + 128] = acc


def _from_chunk_layout(y_ref, ya_scr, batch):
    blk = _lane_group(batch)
    for cl in range(TILE_TOK // S5_CHUNK):
        for s in range(S5_CHUNK):
            for p in range(2):
                acc = None
                for g in range(8 * p, 8 * p + 8):
                    src = y_ref[g, cl * batch:(cl + 1) * batch, 128 * (s // 8):128 * (s // 8) + 128]
                    shift = ((g - s) % 8) * SSM_GROUP
                    v = pltpu.roll(src, shift, 1) if shift else src
                    acc = v if acc is None else jnp.where(blk == g % 8, v, acc)
                ya_scr[p, pl.ds(cl * S5_CHUNK + s, batch, stride=ROW_PITCH), :] = acc


def _inproj_kernel(xc_ref, xl_ref, mod_ref, g_ref, w_ref, u_ref, ub_ref, gate_ref, h_scr, ua_scr, *,
                   batch, n_ctx_tiles):
    rows = batch * TILE_TOK
    x = _pick(xc_ref, xl_ref, pl.program_id(0), n_ctx_tiles)
    y = x * lax.rsqrt(jnp.mean(x * x, axis=-1, keepdims=True) + EPS) * g_ref[...]
    shift = mod_ref[:, :, 0:D_MODEL]
    scale = mod_ref[:, :, D_MODEL:2 * D_MODEL]
    h_scr[...] = (y * (1.0 + scale) + shift).reshape(rows, D_MODEL).astype(BF16)
    h = h_scr[...]
    ua = jnp.dot(h, w_ref[:, 0:D_SSM], preferred_element_type=F32)
    for b in range(batch):
        for p in range(2):
            ua_scr[p, b * ROW_PITCH:b * ROW_PITCH + TILE_TOK, :] = (
                ua[b * TILE_TOK:(b + 1) * TILE_TOK, 128 * p:128 * p + 128])
    ub = jnp.dot(h, w_ref[:, D_SSM:GATE_OFF], preferred_element_type=F32)
    ub_ref[...] = ub.reshape(batch, TILE_TOK, D_POOL)
    for k in range(2):
        lo = GATE_OFF + k * D_MODEL
        g = jnp.dot(h, w_ref[:, lo:lo + D_MODEL], preferred_element_type=F32)
        gate_ref[:, :, k * D_MODEL:(k + 1) * D_MODEL] = (
            jax.nn.sigmoid(g).astype(BF16).reshape(batch, TILE_TOK, D_MODEL))
    _to_chunk_layout(ua_scr, u_ref, batch)


def _inproj(x_ctx, x_lat, lat_off, mod_sel, norm_g, w_in_bf, layer, batch, n_ctx_tiles, n_tiles):
    rows = batch * TILE_TOK
    n_tok = n_tiles * TILE_TOK
    chunk_rows = batch * TILE_TOK // S5_CHUNK
    return pl.pallas_call(
        functools.partial(_inproj_kernel, batch=batch, n_ctx_tiles=n_ctx_tiles),
        out_shape=(jax.ShapeDtypeStruct((N_SSM_GROUPS, n_tiles * chunk_rows, D_SSM), F32),
                   jax.ShapeDtypeStruct((batch, n_tok, D_POOL), F32),
                   jax.ShapeDtypeStruct((batch, n_tok, 2 * D_MODEL), BF16)),
        grid=(n_tiles,),
        in_specs=_src_specs(D_MODEL, batch, n_ctx_tiles, 0, lat_off) + [
            _mod_spec(batch, layer, n_ctx_tiles, 0),
            pl.BlockSpec((None, 1, D_MODEL), lambda j: (layer, 0, 0)),
            pl.BlockSpec((None, D_MODEL, D_IN), lambda j: (layer, 0, 0)),
        ],
        out_specs=(pl.BlockSpec((N_SSM_GROUPS, chunk_rows, D_SSM), lambda j: (0, j, 0)),
                   pl.BlockSpec((batch, TILE_TOK, D_POOL), lambda j: (0, j, 0)),
                   pl.BlockSpec((batch, TILE_TOK, 2 * D_MODEL), lambda j: (0, j, 0))),
        scratch_shapes=[pltpu.VMEM((rows, D_MODEL), BF16),
                        pltpu.VMEM((2, batch * ROW_PITCH, 128), F32)],
        compiler_params=_cparams("arbitrary"),
        name="inproj",
    )(x_ctx, x_lat, mod_sel, norm_g, w_in_bf)


def _s5_matrices(lam_re, lam_im, log_dt, b_re, b_im, c_re, c_im):
    L, G, N, H = S5_CHUNK, N_SSM_GROUPS, SSM_STATE, SSM_GROUP
    lr, li = lam_re.astype(F32), lam_im.astype(F32)
    dt = jnp.exp(log_dt.astype(F32))[..., None]
    zr, zi = lr * dt, li * dt
    k = jnp.arange(L + 1, dtype=F32)[:, None, None, None]
    pm = jnp.exp(zr[None] * k)
    pr, pi = pm * jnp.cos(zi[None] * k), pm * jnp.sin(zi[None] * k)
    nr, ni = pr[1] - 1.0, pi[1]
    den = lr * lr + li * li
    fr = (nr * lr + ni * li) / den
    fi = (ni * lr - nr * li) / den
    br, bi = b_re.astype(F32), b_im.astype(F32)
    bbr = fr[..., None] * br - fi[..., None] * bi
    bbi = fr[..., None] * bi + fi[..., None] * br
    cr, ci = c_re.astype(F32), c_im.astype(F32)
    cpr = cr[None] * pr[:, :, :, None, :] - ci[None] * pi[:, :, :, None, :]
    cpi = cr[None] * pi[:, :, :, None, :] + ci[None] * pr[:, :, :, None, :]
    kern = (jnp.einsum('kdgan,dgnh->kdgah', cpr, bbr)
            - jnp.einsum('kdgan,dgnh->kdgah', cpi, bbi))
    s_idx = np.arange(L)[:, None]
    t_idx = np.arange(L)[None, :]
    lag_f = np.clip(t_idx - s_idx, 0, L - 1)
    lag_b = np.clip(s_idx - t_idx, 0, L - 1)
    mf = jnp.where((t_idx >= s_idx)[:, :, None, None, None], kern[:, 0][lag_f], 0.0)
    mb = jnp.where((s_idx >= t_idx)[:, :, None, None, None], kern[:, 1][lag_b], 0.0)
    m = (mf + mb).transpose(2, 0, 4, 1, 3).reshape(G, L * H, L * H)
    powers = (L - 1 - np.arange(L), np.arange(L))
    inj = []
    for d in range(2):
        p_r, p_i = pr[powers[d], d], pi[powers[d], d]
        wr = p_r[..., None] * bbr[d][None] - p_i[..., None] * bbi[d][None]
        wi = p_r[..., None] * bbi[d][None] + p_i[..., None] * bbr[d][None]
        inj.append((wr, wi))
    w_inj = jnp.concatenate([inj[0][0], inj[1][0], inj[0][1], inj[1][1]], axis=2)
    mcat = jnp.concatenate([m, w_inj.transpose(1, 0, 3, 2).reshape(G, L * H, 4 * N)], axis=2)
    rd = (np.arange(L) + 1, L - np.arange(L))
    w_rd = jnp.concatenate([cpr[rd[0], 0], cpr[rd[1], 1], -cpi[rd[0], 0], -cpi[rd[1], 1]],
                           axis=3)
    wy = w_rd.transpose(1, 3, 0, 2).reshape(G, 4 * N, L * H)
    a_r = jnp.concatenate([pr[L][0], pr[L][1]], axis=-1)
    a_i = jnp.concatenate([pi[L][0], pi[L][1]], axis=-1)
    coef = jnp.stack([a_r, a_i] + [jnp.zeros_like(a_r)] * 6, axis=1)
    return mcat.astype(BF16), wy.astype(BF16), coef


def _s5_kernel(u_ref, mcat_ref, wy_ref, coef_ref, dskip_ref, y_ref, s_scr, h_scr, *, batch,
               n_chunks, n_ctx_chunks):
    n = SSM_STATE
    rb = S5_CHUNK * batch
    n_rb = n_chunks // S5_CHUNK

    def inject(i, c):
        r0 = pl.multiple_of(i * rb, rb)
        u = u_ref[pl.ds(r0, rb), :]
        ub = u.astype(BF16)
        y_ref[pl.ds(r0, rb), :] = (jnp.dot(ub, mcat_ref[:, 0:D_SSM], preferred_element_type=F32)
                                   + dskip_ref[...] * u)
        s_scr[pl.ds(r0, rb), :] = jnp.dot(ub, mcat_ref[:, D_SSM:2 * D_SSM],
                                          preferred_element_type=F32)
        return c

    lax.fori_loop(0, n_rb, inject, 0)
    a_r, a_i = coef_ref[0:1, :], coef_ref[1:2, :]
    fwd_lanes = lax.broadcasted_iota(jnp.int32, (batch, 2 * n), 1) < n

    def step(k, carry):
        h_re, h_im = carry
        cb = jnp.where(k < n_ctx_chunks, n_ctx_chunks - 1 - k, n_chunks - 1 + n_ctx_chunks - k)
        rf = pl.multiple_of(k * batch, batch)
        rk = pl.multiple_of(cb * batch, batch)
        h_scr[pl.ds(rf, batch), 0:n] = h_re[:, 0:n]
        h_scr[pl.ds(rk, batch), n:2 * n] = h_re[:, n:2 * n]
        h_scr[pl.ds(rf, batch), 2 * n:3 * n] = h_im[:, 0:n]
        h_scr[pl.ds(rk, batch), 3 * n:4 * n] = h_im[:, n:2 * n]
        s_re = jnp.where(fwd_lanes, s_scr[pl.ds(rf, batch), 0:2 * n], s_scr[pl.ds(rk, batch), 0:2 * n])
        s_im = jnp.where(fwd_lanes, s_scr[pl.ds(rf, batch), 2 * n:4 * n],
                         s_scr[pl.ds(rk, batch), 2 * n:4 * n])
        return (a_r * h_re - a_i * h_im + s_re, a_r * h_im + a_i * h_re + s_im)

    zero = jnp.zeros((batch, 2 * n), F32)
    lax.fori_loop(0, n_chunks, step, (zero, zero), unroll=4)

    def readout(i, c):
        r0 = pl.multiple_of(i * rb, rb)
        y_ref[pl.ds(r0, rb), :] += jnp.dot(h_scr[pl.ds(r0, rb), :].astype(BF16), wy_ref[...],
                                           preferred_element_type=F32)
        return c

    lax.fori_loop(0, n_rb, readout, 0)


def _s5(u_t, mcat, wy, coef, dskip, batch, n_chunks, n_ctx_chunks):
    G = N_SSM_GROUPS
    rows = u_t.shape[1]
    kern = functools.partial(_s5_kernel, batch=batch, n_chunks=n_chunks, n_ctx_chunks=n_ctx_chunks)
    per_group = lambda *blk: pl.BlockSpec((None,) + blk, lambda g: (g, 0, 0))
    return pl.pallas_call(
        kern,
        out_shape=jax.ShapeDtypeStruct((G, rows, D_SSM), F32),
        grid=(G,),
        in_specs=[per_group(rows, D_SSM), per_group(D_SSM, 2 * D_SSM), per_group(D_SSM, D_SSM),
                  per_group(8, 2 * SSM_STATE), per_group(1, D_SSM)],
        out_specs=per_group(rows, D_SSM),
        scratch_shapes=[pltpu.VMEM((rows, D_SSM), F32), pltpu.VMEM((rows, D_SSM), F32)],
        compiler_params=_cparams("arbitrary"),
        name="s5",
    )(u_t, mcat, wy, coef, dskip)


def _window(w):
    return -(w // 2), w - 1 - w // 2


def _pool_constants(rows, width):
    rpb = POOL_BLOCK // width
    pm = np.zeros((4, POOL_BLOCK, POOL_BLOCK), np.float32)
    inv = np.zeros((rows * width, 4), np.float32)
    col = np.arange(width)
    row = np.arange(rows)
    for i, w in enumerate(POOL_WINDOWS):
        lo, hi = _window(w)
        c0, c1 = np.clip(col + lo, 0, width - 1), np.clip(col + hi, 0, width - 1)
        r0, r1 = np.clip(row + lo, 0, rows - 1), np.clip(row + hi, 0, rows - 1)
        band = ((col[None, :] >= c0[:, None]) & (col[None, :] <= c1[:, None])).astype(np.float32)
        for r in range(rpb):
            pm[i, r * width:(r + 1) * width, r * width:(r + 1) * width] = band
        cnt = (r1 - r0 + 1)[:, None] * (c1 - c0 + 1)[None, :]
        inv[:, i] = (1.0 / cnt).reshape(-1)
    return jnp.asarray(pm, BF16), jnp.asarray(inv, F32)


def _pool_segment(u_ref, o_ref, cs_scr, pm_ref, inv_ref, wp_ref, ps_ref, tok0, rows, width):
    n_tok = rows * width
    n_blk = n_tok // POOL_BLOCK
    pad = 8 * width if rows > 1 else 0
    if rows > 1:
        zeros = jnp.zeros((pad, 2 * 128), F32)
        for i in range(4):
            cs_scr[i, 0:pad, :] = zeros
            cs_scr[i, pad + n_tok:pad + n_tok + pad, :] = zeros
    for b in range(n_blk):
        t0 = b * POOL_BLOCK
        for i in range(4):
            seg = POOL_SEG[i]
            xb = u_ref[tok0 + t0:tok0 + t0 + POOL_BLOCK, seg:seg + 256].astype(BF16)
            cs_scr[i, pad + t0:pad + t0 + POOL_BLOCK, :] = jnp.dot(
                pm_ref[i], xb, preferred_element_type=F32)
    lane = lax.broadcasted_iota(jnp.int32, (POOL_BLOCK, 128), 1)
    low_half = lane < 64
    for b in range(n_blk):
        t0 = b * POOL_BLOCK
        win = []
        for i, w in enumerate(POOL_WINDOWS):
            lo, hi = _window(w) if rows > 1 else (0, 0)
            acc = None
            for k in range(lo, hi + 1):
                base = pad + t0 + k * width
                piece = cs_scr[i, base:base + POOL_BLOCK, :]
                acc = piece if acc is None else acc + piece
            win.append(acc * inv_ref[t0:t0 + POOL_BLOCK, i:i + 1])
        x = u_ref[tok0 + t0:tok0 + t0 + POOL_BLOCK, :]
        pooled = jnp.concatenate([
            win[0][:, 0:128],
            jnp.where(low_half, win[0][:, 128:256], win[1][:, 0:128]),
            win[1][:, 128:256],
            win[2][:, 0:128],
            jnp.where(low_half, win[2][:, 128:256], win[3][:, 0:128]),
            win[3][:, 128:256]], axis=1)
        d = (pooled - x).astype(BF16)
        y0 = jnp.dot(d[:, 0:384], wp_ref[0], preferred_element_type=F32)
        y1 = jnp.dot(d[:, 384:768], wp_ref[1], preferred_element_type=F32)
        y = jnp.concatenate([y0, y1], axis=1) * ps_ref[...]
        o_ref[tok0 + t0:tok0 + t0 + POOL_BLOCK, :] = y.astype(BF16)


def _pool_kernel(u_ref, pmc_ref, invc_ref, pml_ref, invl_ref, wp_ref, ps_ref, o_ref, cs_scr, *,
                 lat_rows, with_ctx):
    if with_ctx:
        _pool_segment(u_ref, o_ref, cs_scr, pmc_ref, invc_ref, wp_ref, ps_ref, 0, 1, CTX_LEN)
    else:
        o_ref[0:CTX_LEN, :] = jnp.zeros((CTX_LEN, D_POOL), BF16)
    _pool_segment(u_ref, o_ref, cs_scr, pml_ref, invl_ref, wp_ref, ps_ref, CTX_LEN, lat_rows, GRID_W)


def _pool(ub, wp2, pscale, layer, with_ctx):
    batch, n_tok, _ = ub.shape
    lat_rows = (n_tok - CTX_LEN) // GRID_W
    pmc, invc = _pool_constants(1, CTX_LEN)
    pml, invl = _pool_constants(lat_rows, GRID_W)
    const = lambda *blk: pl.BlockSpec(blk, lambda b: (0,) * len(blk))
    return pl.pallas_call(
        functools.partial(_pool_kernel, lat_rows=lat_rows, with_ctx=with_ctx),
        out_shape=jax.ShapeDtypeStruct((batch, n_tok, D_POOL), BF16),
        grid=(batch,),
        in_specs=[
            pl.BlockSpec((None, n_tok, D_POOL), lambda b: (b, 0, 0)),
            const(4, POOL_BLOCK, POOL_BLOCK), const(CTX_LEN, 4),
            const(4, POOL_BLOCK, POOL_BLOCK), const(lat_rows * GRID_W, 4),
            pl.BlockSpec((None, 2, 384, 384), lambda b: (layer, 0, 0, 0)),
            pl.BlockSpec((None, 1, D_POOL), lambda b: (layer, 0, 0)),
        ],
        out_specs=pl.BlockSpec((None, n_tok, D_POOL), lambda b: (b, 0, 0)),
        scratch_shapes=[pltpu.VMEM((4, lat_rows * GRID_W + 16 * GRID_W, 256), F32)],
        compiler_params=_cparams("arbitrary"),
        name="pool",
    )(ub, pmc, invc, pml, invl, wp2, pscale)


def _first_argmax(rows_):
    best, idx = rows_[0], jnp.zeros_like(rows_[0], dtype=jnp.int32)
    for k in range(1, len(rows_)):
        take = rows_[k] > best
        idx = jnp.where(take, k, idx)
        best = jnp.where(take, rows_[k], best)
    return best, idx


def _route(lt):
    g_rows = [lt[k:k + 1, :] for k in range(N_EXPERT_GROUPS)]
    best, grp = _first_argmax(g_rows)
    denom = sum(jnp.exp(r - best) for r in g_rows)
    p_grp = 1.0 / denom
    inner = []
    for e in range(EXPERTS_PER_GROUP):
        acc = jnp.zeros_like(best)
        for g in range(N_EXPERT_GROUPS):
            r = 4 + 4 * g + e
            acc = jnp.where(grp == g, lt[r:r + 1, :], acc)
        inner.append(acc)
    v1, i1 = _first_argmax(inner)
    masked = [jnp.where(i1 == e, -jnp.inf, inner[e]) for e in range(EXPERTS_PER_GROUP)]
    v2, i2 = _first_argmax(masked)
    e21 = jnp.exp(v2 - v1)
    w1 = p_grp / (1.0 + e21)
    w2 = p_grp * e21 / (1.0 + e21)
    first_low = i1 < i2
    lo = jnp.where(first_low, i1, i2)
    hi = jnp.where(first_low, i2, i1)
    w_lo = jnp.where(first_low, w1, w2)
    w_hi = jnp.where(first_low, w2, w1)
    off = jnp.where(lo == 0, 0, jnp.where(lo == 1, 3, 5))
    cls = N_PAIRS * grp + off + hi - lo - 1
    return cls.astype(F32), w_lo, w_hi


def _merge_kernel(xc_ref, xl_ref, y_ref, yb_ref, gate_ref, mod_ref, g2_ref, wglu_ref, bglu_ref,
                  wbra_ref, wbrb_ref, wout_ref, wrh_ref, wrl_ref, br_ref, x1_ref, h2_ref, route_ref,
                  rcol_ref, ya_scr, *, batch, n_ctx_tiles, tile_off):
    D = D_MODEL
    rows = batch * TILE_TOK
    tile = pl.program_id(0) + tile_off
    _from_chunk_layout(y_ref, ya_scr, batch)
    y = jnp.concatenate(
        [jnp.concatenate([ya_scr[p, b * ROW_PITCH:b * ROW_PITCH + TILE_TOK, :] for p in range(2)],
                         axis=1) for b in range(batch)], axis=0)
    z = jax.nn.gelu(y)
    glu = jax.nn.sigmoid(jnp.dot(z.astype(BF16), wglu_ref[...], preferred_element_type=F32)
                         + bglu_ref[...])
    ya = (z * glu).astype(BF16)
    gates = gate_ref[...].reshape(rows, 2 * D)
    m = gates[:, 0:D].astype(F32) * jnp.dot(ya, wbra_ref[...], preferred_element_type=F32)
    m += gates[:, D:2 * D].astype(F32) * jnp.dot(yb_ref[...].reshape(rows, D_POOL), wbrb_ref[...],
                                                 preferred_element_type=F32)
    out = jnp.dot(m.astype(BF16), wout_ref[...], preferred_element_type=F32)
    x_in = _pick(xc_ref, xl_ref, tile, n_ctx_tiles)
    x1 = x_in + mod_ref[:, :, 2 * D:3 * D] * out.reshape(batch, TILE_TOK, D)
    x1_ref[...] = x1
    yn = x1 * lax.rsqrt(jnp.mean(x1 * x1, axis=-1, keepdims=True) + EPS) * g2_ref[...]
    h2 = yn * (1.0 + mod_ref[:, :, 4 * D:5 * D]) + mod_ref[:, :, 3 * D:4 * D]
    h2_ref[...] = h2.reshape(rows, D)
    h_hi, h_lo = _split_bf16(h2.reshape(rows, D))
    logits = jnp.dot(h_hi, wrh_ref[...], preferred_element_type=F32)
    logits += jnp.dot(h_lo, wrh_ref[...], preferred_element_type=F32)
    logits += jnp.dot(h_hi, wrl_ref[...], preferred_element_type=F32)
    logits += br_ref[...]
    cls, w_lo, w_hi = _route(logits.T)
    zero = jnp.zeros_like(cls)
    route_ref[...] = jnp.concatenate([cls, w_lo, w_hi, zero, zero, zero, zero, zero], axis=0)
    cols = jnp.concatenate([cls, w_lo, w_hi, jnp.zeros((ROUTE_LANES - 3, rows), F32)], axis=0)
    rcol_ref[...] = cols.T


def _merge(x_ctx, x_lat, lat_off, y_t, yb, gates, mod_sel, norm2_g, p, layer, batch, tile_off,
           n_tiles, n_ctx_tiles):
    rows = batch * TILE_TOK
    chunk_rows = rows // S5_CHUNK
    tok = lambda cols: pl.BlockSpec((batch, TILE_TOK, cols), lambda j: (0, j + tile_off, 0))
    lay3 = lambda *blk: pl.BlockSpec((None,) + blk, lambda j: (layer, 0, 0))
    n_tok = n_tiles * TILE_TOK
    return pl.pallas_call(
        functools.partial(_merge_kernel, batch=batch, n_ctx_tiles=n_ctx_tiles, tile_off=tile_off),
        out_shape=(jax.ShapeDtypeStruct((batch, n_tok, D_MODEL), F32),
                   jax.ShapeDtypeStruct((n_tiles * rows, D_MODEL), F32),
                   jax.ShapeDtypeStruct((8, n_tiles * rows), F32),
                   jax.ShapeDtypeStruct((n_tiles * rows, ROUTE_LANES), F32)),
        grid=(n_tiles,),
        in_specs=_src_specs(D_MODEL, batch, n_ctx_tiles, tile_off, lat_off) + [
            pl.BlockSpec((N_SSM_GROUPS, chunk_rows, D_SSM), lambda j: (0, j + tile_off, 0)),
            tok(D_POOL), tok(2 * D_MODEL),
            _mod_spec(batch, layer, n_ctx_tiles, tile_off),
            lay3(1, D_MODEL), lay3(D_SSM, D_SSM), lay3(1, D_SSM), lay3(D_SSM, D_MODEL),
            lay3(D_POOL, D_MODEL), lay3(D_MODEL, D_MODEL), lay3(D_MODEL, ROUTE_LANES),
            lay3(D_MODEL, ROUTE_LANES), lay3(1, ROUTE_LANES),
        ],
        out_specs=(pl.BlockSpec((batch, TILE_TOK, D_MODEL), lambda j: (0, j, 0)),
                   pl.BlockSpec((rows, D_MODEL), lambda j: (j, 0)),
                   pl.BlockSpec((8, rows), lambda j: (0, j)),
                   pl.BlockSpec((rows, ROUTE_LANES), lambda j: (j, 0))),
        scratch_shapes=[pltpu.VMEM((2, batch * ROW_PITCH, 128), F32)],
        compiler_params=_cparams("arbitrary"),
        name="merge",
    )(x_ctx, x_lat, y_t, yb, gates, mod_sel, norm2_g, p['w_glu'], p['b_glu'], p['w_br_a'],
      p['w_br_b'], p['w_out'], p['wr_hi'], p['wr_lo'], p['b_r'])


def _plan_kernel(cls_ref, dest_ref, src_ref, meta_ref, dest_smem, fill_scr, sem, *, n_q):
    rows_q = n_q // 128
    cls = cls_ref[...]
    r_i = lax.broadcasted_iota(jnp.int32, (128, 128), 0)
    c_i = lax.broadcasted_iota(jnp.int32, (128, 128), 1)
    upper = jnp.where(r_i <= c_i, 1.0, 0.0).astype(BF16)
    r_q = lax.broadcasted_iota(jnp.int32, (rows_q, rows_q), 0)
    c_q = lax.broadcasted_iota(jnp.int32, (rows_q, rows_q), 1)
    lower = jnp.where(c_q < r_q, 1.0, 0.0).astype(BF16)
    tile_start = (lax.broadcasted_iota(jnp.int32, (1, META_LANES), 1) * EXPERT_TILE).astype(F32)
    start = jnp.zeros((1, 128), F32)
    dest = jnp.zeros((rows_q, 128), F32)
    tile_cls = jnp.zeros((1, META_LANES), F32)
    for c in range(N_CLASSES):
        m = cls == float(c)
        incl = jnp.dot(jnp.where(m, 1.0, 0.0).astype(BF16), upper, preferred_element_type=F32)
        row_tot = jnp.broadcast_to(incl[:, 127:128], (rows_q, 128))
        before = jnp.dot(lower, row_tot.astype(BF16), preferred_element_type=F32)
        dest = jnp.where(m, start + before + incl - 1.0, dest)
        total = before[rows_q - 1:rows_q, :] + row_tot[rows_q - 1:rows_q, :]
        start = start + jnp.floor((total + (EXPERT_TILE - 1.0)) * (1.0 / EXPERT_TILE)) * EXPERT_TILE
        end2 = jnp.concatenate([start, start], axis=1)
        tile_cls = tile_cls + jnp.where(tile_start >= end2, 1.0, 0.0)
    active = jnp.where(tile_start < end2, 1.0, 0.0)
    tc = jnp.minimum(tile_cls, N_CLASSES - 1.0)
    grp = sum(jnp.where(tc >= float(N_PAIRS * k), 1.0, 0.0) for k in range(1, N_EXPERT_GROUPS))
    pair = tc - N_PAIRS * grp
    p_lo = jnp.where(pair >= 3.0, 1.0, 0.0) + jnp.where(pair >= 5.0, 1.0, 0.0)
    p_hi = jnp.where(pair == 0.0, 1.0, jnp.where(pair == 1.0, 2.0, jnp.where(pair == 3.0, 2.0, 3.0)))
    lane = lax.broadcasted_iota(jnp.int32, (1, META_LANES), 1)
    fresh = jnp.where((lane == 0) | (tc != pltpu.roll(tc, 1, 1)), 1.0, 0.0)
    zero = jnp.zeros_like(tc)
    meta_ref[...] = jnp.concatenate(
        [EXPERTS_PER_GROUP * grp + p_lo, EXPERTS_PER_GROUP * grp + p_hi, active, fresh,
         zero, zero, zero, zero], axis=0).astype(jnp.int32)
    dest_ref[...] = dest.astype(jnp.int32)

    fill_scr[...] = jnp.full(fill_scr.shape, n_q, jnp.int32)
    fill = pltpu.make_async_copy(fill_scr, src_ref, sem.at[0])
    stage = pltpu.make_async_copy(dest_ref, dest_smem, sem.at[1])
    fill.start()
    stage.start()
    fill.wait()
    stage.wait()

    def invert(i, carry):
        for k in range(128):
            d = dest_smem[i, k]
            src_ref[d >> 7, d & 127] = i * 128 + k
        return carry

    lax.fori_loop(0, rows_q, invert, 0)


def _plan(cls_q):
    n_q = cls_q.shape[0]
    rows_q = n_q // 128
    n_tiles = n_q // EXPERT_TILE + N_CLASSES
    assert n_tiles <= META_LANES
    rows_p = n_tiles * EXPERT_TILE // 128
    i32 = jnp.int32
    return pl.pallas_call(
        functools.partial(_plan_kernel, n_q=n_q),
        out_shape=(jax.ShapeDtypeStruct((rows_q, 128), i32), jax.ShapeDtypeStruct((rows_p, 128), i32),
                   jax.ShapeDtypeStruct((8, META_LANES), i32)),
        in_specs=[pl.BlockSpec(memory_space=pltpu.VMEM)],
        out_specs=(pl.BlockSpec(memory_space=pltpu.VMEM), pl.BlockSpec(memory_space=pltpu.SMEM),
                   pl.BlockSpec(memory_space=pltpu.VMEM)),
        scratch_shapes=[pltpu.SMEM((rows_q, 128), i32), pltpu.VMEM((rows_p, 128), i32),
                        pltpu.SemaphoreType.DMA((2,))],
        compiler_params=pltpu.CompilerParams(vmem_limit_bytes=V7X_VMEM_LIMIT),
        name="plan",
    )(cls_q.reshape(rows_q, 128))


def _start_row_gather(idx_ref, idx_row0, src_hbm, dst_ref, sem, max_row=None):
    for r in range(dst_ref.shape[0]):
        row = idx_ref[idx_row0 + r // 128, r % 128]
        if max_row is not None:
            row = jnp.minimum(row, max_row)
        pltpu.make_async_copy(src_hbm.at[pl.ds(row, 1)], dst_ref.at[pl.ds(r, 1)],
                              sem).start(priority=r % 2)


def _wait_row_gather(src_hbm, dst_ref, sem):
    pltpu.make_async_copy(src_hbm.at[pl.ds(0, dst_ref.shape[0])], dst_ref, sem).wait()


GATHER_SLOTS = 3


def _expert_kernel(src_ref, elo_ref, ehi_ref, act_ref, fresh_ref, h2_hbm, w1a_ref, w1b_ref, w3a_ref,
                   w3b_ref, w2a_ref, w2b_ref, o_ref, x_buf, w1a_bf, w1b_bf, w3a_bf, w3b_bf, w2a_bf,
                   w2b_bf, sem, *, n_q):
    j = pl.program_id(0)
    last = pl.num_programs(0) - 1
    ahead = GATHER_SLOTS - 1
    slot = lax.rem(j, GATHER_SLOTS)
    idx_rows = EXPERT_TILE // 128

    def fetch(tile, into):
        _start_row_gather(src_ref, jnp.minimum(tile, last) * idx_rows, h2_hbm, x_buf.at[into],
                          sem.at[into], n_q - 1)

    @pl.when(j == 0)
    def _():
        for t in range(ahead):
            fetch(t, t)

    _wait_row_gather(h2_hbm, x_buf.at[slot], sem.at[slot])
    fetch(j + ahead, lax.rem(j + ahead, GATHER_SLOTS))

    @pl.when(fresh_ref[j] == 1)
    def _():
        for src_w, dst_w in ((w1a_ref, w1a_bf), (w1b_ref, w1b_bf), (w3a_ref, w3a_bf),
                             (w3b_ref, w3b_bf), (w2a_ref, w2a_bf), (w2b_ref, w2b_bf)):
            dst_w[...] = src_w[...].astype(BF16)

    @pl.when(act_ref[j] == 1)
    def _():
        x = x_buf[slot].astype(BF16)
        for k, (w1, w3, w2) in enumerate(((w1a_bf, w3a_bf, w2a_bf), (w1b_bf, w3b_bf, w2b_bf))):
            a = jnp.dot(x, w1[...], preferred_element_type=F32)
            b = jnp.dot(x, w3[...], preferred_element_type=F32)
            h = (a * jax.nn.sigmoid(a) * b).astype(BF16)
            o_ref[:, k * D_MODEL:(k + 1) * D_MODEL] = jnp.dot(h, w2[...], preferred_element_type=F32)

    @pl.when(act_ref[j] == 0)
    def _():
        o_ref[...] = jnp.zeros_like(o_ref)

    @pl.when(j == last)
    def _():
        for t in range(1, GATHER_SLOTS):
            into = lax.rem(j + t, GATHER_SLOTS)
            _wait_row_gather(h2_hbm, x_buf.at[into], sem.at[into])


def _experts(src, meta, h2, w1, w3, w2, layer):
    n_q = h2.shape[0]
    n_rows = src.shape[0] * 128
    te = EXPERT_TILE
    n_tiles = n_rows // te
    up = lambda sel: pl.BlockSpec((None, None, D_MODEL, D_EXPERT),
                                  lambda j, s, lo, hi, act, fr: (layer, (lo, hi)[sel][j], 0, 0))
    down = lambda sel: pl.BlockSpec((None, None, D_EXPERT, D_MODEL),
                                    lambda j, s, lo, hi, act, fr: (layer, (lo, hi)[sel][j], 0, 0))
    return pl.pallas_call(
        functools.partial(_expert_kernel, n_q=n_q),
        out_shape=jax.ShapeDtypeStruct((n_rows, 2 * D_MODEL), F32),
        grid_spec=pltpu.PrefetchScalarGridSpec(
            num_scalar_prefetch=5,
            grid=(n_tiles,),
            in_specs=[pl.BlockSpec(memory_space=pl.ANY),
                      up(0), up(1), up(0), up(1), down(0), down(1)],
            out_specs=pl.BlockSpec((te, 2 * D_MODEL), lambda j, s, lo, hi, act, fr: (j, 0)),
            scratch_shapes=[pltpu.VMEM((GATHER_SLOTS, te, D_MODEL), F32)]
            + [pltpu.VMEM((D_MODEL, D_EXPERT), BF16)] * 4 + [pltpu.VMEM((D_EXPERT, D_MODEL), BF16)] * 2
            + [pltpu.SemaphoreType.DMA((GATHER_SLOTS,))],
        ),
        compiler_params=_cparams("arbitrary"),
        name="experts",
    )(src, meta[0, :n_tiles], meta[1, :n_tiles], meta[2, :n_tiles], meta[3, :n_tiles],
      h2, w1, w1, w3, w3, w2, w2)


def _combine_kernel(dest_ref, x_ref, fs_hbm, rcol_ref, mod_ref, g_ref, o_ref, f_buf, sem, *, batch,
                    final):
    j = pl.program_id(0)
    last = pl.num_programs(0) - 1
    slot = j % 2
    rows = batch * TILE_TOK
    idx_rows = rows // 128

    @pl.when(j == 0)
    def _():
        _start_row_gather(dest_ref, 0, fs_hbm, f_buf.at[0], sem.at[0])

    _wait_row_gather(fs_hbm, f_buf.at[slot], sem.at[slot])
    _start_row_gather(dest_ref, jnp.minimum(j + 1, last) * idx_rows, fs_hbm, f_buf.at[1 - slot],
                      sem.at[1 - slot])
    f = (rcol_ref[:, 1:2] * f_buf[slot, :, 0:D_MODEL]
         + rcol_ref[:, 2:3] * f_buf[slot, :, D_MODEL:2 * D_MODEL])
    x = x_ref[...] + mod_ref[:, :, 5 * D_MODEL:6 * D_MODEL] * f.reshape(batch, TILE_TOK, D_MODEL)
    if final:
        x = x * lax.rsqrt(jnp.mean(x * x, axis=-1, keepdims=True) + EPS) * g_ref[...]
    o_ref[...] = x

    @pl.when(j == last)
    def _():
        _wait_row_gather(fs_hbm, f_buf.at[1 - slot], sem.at[1 - slot])


def _combine(dest, x1, fs, rcol, mod_sel, final_g, layer, tile_off, n_ctx_tiles, final):
    batch, n_tok, _ = x1.shape
    rows = batch * TILE_TOK
    tok = pl.BlockSpec((batch, TILE_TOK, D_MODEL), lambda j, d: (0, j, 0))
    mod_spec = _mod_spec(batch, layer, n_ctx_tiles, tile_off)
    return pl.pallas_call(
        functools.partial(_combine_kernel, batch=batch, final=final),
        out_shape=jax.ShapeDtypeStruct((batch, n_tok, D_MODEL), F32),
        grid_spec=pltpu.PrefetchScalarGridSpec(
            num_scalar_prefetch=1,
            grid=(n_tok // TILE_TOK,),
            in_specs=[tok, pl.BlockSpec(memory_space=pl.ANY),
                      pl.BlockSpec((rows, ROUTE_LANES), lambda j, d: (j, 0)),
                      pl.BlockSpec(mod_spec.block_shape, lambda j, d: mod_spec.index_map(j)),
                      pl.BlockSpec((1, D_MODEL), lambda j, d: (0, 0))],
            out_specs=tok,
            scratch_shapes=[pltpu.VMEM((2, rows, 2 * D_MODEL), F32), pltpu.SemaphoreType.DMA((2,))],
        ),
        compiler_params=_cparams("arbitrary"),
        name="combine",
    )(dest, x1, fs, rcol, mod_sel, final_g)


def kernel(x, c, ctx, c_ctx, norm1_g, norm2_g, w_mod, b_mod, w_in, lam_re, lam_im, log_dt, b_re,
           b_im, c_re, c_im, d_skip, w_glu, b_glu, w_pool, pool_scale, w_br_a, w_br_b, w_out, w_r1,
           b_r1, w_r2, b_r2, w1, w3, w2, final_g):
    batch, seq, d = x.shape
    assert d == D_MODEL and seq % (GRID_W * 8) == 0 and ctx.shape[1] == CTX_LEN and batch % 8 == 0
    n_ctx_tiles, n_lat_tiles = CTX_LEN // TILE_TOK, seq // TILE_TOK
    n_all_tiles = n_ctx_tiles + n_lat_tiles
    n_chunks = (CTX_LEN + seq) // S5_CHUNK
    mod_rows = ((batch + 1 + 7) // 8) * 8

    cc = jnp.concatenate([c, c_ctx[None, :], jnp.zeros((mod_rows - batch - 1, d), F32)], axis=0)
    mod = _modulation(cc, w_mod, b_mod)
    mod_sel = jnp.stack([jnp.broadcast_to(mod[:, batch:batch + 1], (DEPTH, batch, 6 * d)),
                         mod[:, :batch]], axis=1)[:, :, :, None, :]

    w_in_bf = w_in.astype(BF16)
    w_r = jnp.concatenate([w_r1, w_r2.transpose(0, 2, 1, 3).reshape(DEPTH, d, N_EXPERTS),
                           jnp.zeros((DEPTH, d, ROUTE_LANES - 4 - N_EXPERTS), F32)], axis=2)
    wr_hi = w_r.astype(BF16)
    wr_lo = (w_r - wr_hi.astype(F32)).astype(BF16)
    b_r = jnp.concatenate([b_r1, b_r2.reshape(DEPTH, N_EXPERTS),
                           jnp.zeros((DEPTH, ROUTE_LANES - 4 - N_EXPERTS), F32)], axis=1)
    zeros_p = jnp.zeros((DEPTH, POOL_GROUP, POOL_GROUP), F32)
    wp2 = jnp.stack([
        jnp.concatenate([jnp.concatenate([w_pool[:, 0], zeros_p], axis=2),
                         jnp.concatenate([zeros_p, w_pool[:, 1]], axis=2)], axis=1),
        jnp.concatenate([jnp.concatenate([w_pool[:, 2], zeros_p], axis=2),
                         jnp.concatenate([zeros_p, w_pool[:, 3]], axis=2)], axis=1)],
        axis=1).astype(BF16)
    params = {
        'w_glu': w_glu.astype(BF16), 'b_glu': b_glu.reshape(DEPTH, 1, D_SSM),
        'w_br_a': w_br_a.astype(BF16), 'w_br_b': w_br_b.astype(BF16), 'w_out': w_out.astype(BF16),
        'wr_hi': wr_hi, 'wr_lo': wr_lo, 'b_r': b_r.reshape(DEPTH, 1, ROUTE_LANES),
    }
    norm1 = norm1_g.reshape(DEPTH, 1, d)
    norm2 = norm2_g.reshape(DEPTH, 1, d)
    pscale = pool_scale.reshape(DEPTH, 1, D_POOL)
    dskip = jnp.tile(d_skip.reshape(DEPTH, N_SSM_GROUPS, 1, SSM_GROUP), (1, 1, 1, S5_CHUNK))

    x_ctx, x_lat, lat_off = ctx, x, 0
    out = None
    for l in range(DEPTH):
        last = l == DEPTH - 1
        u_t, ub, gates = _inproj(x_ctx, x_lat, lat_off, mod_sel, norm1, w_in_bf, l, batch,
                                 n_ctx_tiles, n_all_tiles)
        mcat, wy, coef = _s5_matrices(lam_re[l], lam_im[l], log_dt[l], b_re[l], b_im[l],
                                      c_re[l], c_im[l])
        y_t = _s5(u_t, mcat, wy, coef, dskip[l], batch, n_chunks, CTX_LEN // S5_CHUNK)
        yb = _pool(ub, wp2, pscale, l, not last)

        tile_off = n_ctx_tiles if last else 0
        n_tiles = n_lat_tiles if last else n_all_tiles
        x1, h2, route, rcol = _merge(x_ctx, x_lat, lat_off, y_t, yb, gates, mod_sel, norm2, params, l,
                               batch, tile_off, n_tiles, n_ctx_tiles)
        dest, src, meta = _plan(route[0])
        fs = _experts(src, meta, h2, w1, w3, w2, l)
        res = _combine(dest, x1, fs, rcol, mod_sel, final_g.reshape(1, d), l, tile_off, n_ctx_tiles,
                       last)
        if last:
            out = res
        else:
            x_ctx, x_lat, lat_off = res, res, n_ctx_tiles
    return out
```

```python
import functools

import numpy as np
import jax
import jax.numpy as jnp
from jax import lax
from jax.experimental import pallas as pl
from jax.experimental.pallas import tpu as pltpu

F32 = jnp.float32
BF16 = jnp.bfloat16

D_MODEL = 1024
DEPTH = 2
GRID_W = 64
CTX_LEN = 256
EPS = 1e-6

D_SSM = 256
SSM_GROUP = 16
N_SSM_GROUPS = 16
SSM_STATE = 64
S5_CHUNK = 16

D_POOL = 768
POOL_WINDOWS = (2, 4, 8, 16)
POOL_GROUP = 192
POOL_SEG = (0, 128, 384, 512)
GATE_OFF = D_SSM + D_POOL
D_IN = D_SSM + D_POOL + 2 * D_MODEL

N_EXPERT_GROUPS = 4
EXPERTS_PER_GROUP = 4
N_EXPERTS = 16
D_EXPERT = 512
N_PAIRS = 6
N_CLASSES = N_EXPERT_GROUPS * N_PAIRS
ROUTE_LANES = 128

TILE_TOK = 32
ROW_PITCH = TILE_TOK + 4
POOL_BLOCK = 256
EXPERT_TILE = 256
META_LANES = 256
V7X_VMEM_LIMIT = 56 * 1024 * 1024


def _cparams(*sem):
    return pltpu.CompilerParams(dimension_semantics=sem, vmem_limit_bytes=V7X_VMEM_LIMIT)


def _split_bf16(v):
    hi = v.astype(BF16)
    lo = (v - hi.astype(F32)).astype(BF16)
    return hi, lo


def _mod_kernel(c_ref, w_ref, b_ref, o_ref):
    c = c_ref[...]
    a = c * jax.nn.sigmoid(c)
    a_hi, a_lo = _split_bf16(a)
    w_hi, w_lo = _split_bf16(w_ref[...])
    acc = jnp.dot(a_hi, w_hi, preferred_element_type=F32)
    acc += jnp.dot(a_lo, w_hi, preferred_element_type=F32)
    acc += jnp.dot(a_hi, w_lo, preferred_element_type=F32)
    o_ref[...] = acc + b_ref[...]


def _modulation(cc, w_mod, b_mod):
    rows = cc.shape[0]
    nblk = 4
    cols = 6 * D_MODEL // nblk
    return pl.pallas_call(
        _mod_kernel,
        out_shape=jax.ShapeDtypeStruct((DEPTH, rows, 6 * D_MODEL), F32),
        grid=(DEPTH, nblk),
        in_specs=[
            pl.BlockSpec((rows, D_MODEL), lambda l, j: (0, 0)),
            pl.BlockSpec((None, D_MODEL, cols), lambda l, j: (l, 0, j)),
            pl.BlockSpec((None, 1, cols), lambda l, j: (l, 0, j)),
        ],
        out_specs=pl.BlockSpec((None, rows, cols), lambda l, j: (l, 0, j)),
        compiler_params=_cparams("arbitrary", "arbitrary"),
        name="modulation",
    )(cc, w_mod, b_mod.reshape(DEPTH, 1, 6 * D_MODEL))


def _pick(ctx_ref, lat_ref, tile, n_ctx_tiles):
    return jnp.where(tile < n_ctx_tiles, ctx_ref[...], lat_ref[...])


def _src_specs(cols, batch, n_ctx_tiles, tile_off, lat_off):
    blk = (batch, TILE_TOK, cols)
    return [pl.BlockSpec(blk, lambda j, *_: (0, jnp.minimum(j + tile_off, n_ctx_tiles - 1), 0)),
            pl.BlockSpec(blk, lambda j, *_: (0, jnp.maximum(j + tile_off - n_ctx_tiles, 0) + lat_off,
                                             0))]


def _mod_spec(batch, layer, n_ctx_tiles, tile_off):
    return pl.BlockSpec((None, None, batch, 1, 6 * D_MODEL),
                        lambda j, *_: (layer, (j + tile_off >= n_ctx_tiles).astype(jnp.int32), 0, 0, 0))


def _lane_group(batch):
    return lax.broadcasted_iota(jnp.int32, (batch, 128), 1) // SSM_GROUP


def _to_chunk_layout(ua_scr, u_ref, batch):
    blk = _lane_group(batch)
    for cl in range(TILE_TOK // S5_CHUNK):
        for g in range(N_SSM_GROUPS):
            for q in range(2):
                acc = None
                for s in range(8 * q, 8 * q + 8):
                    src = ua_scr[g // 8, pl.ds(cl * S5_CHUNK + s, batch, stride=ROW_PITCH), :]
                    shift = ((s - g) % 8) * SSM_GROUP
                    v = pltpu.roll(src, shift, 1) if shift else src
                    acc = v if acc is None else jnp.where(blk == s % 8, v, acc)
                u_ref[g, cl * batch:(cl + 1) * batch, 128 * q:128 * q + 128] = acc


def _from_chunk_layout(y_ref, ya_scr, batch):
    blk = _lane_group(batch)
    for cl in range(TILE_TOK // S5_CHUNK):
        for s in range(S5_CHUNK):
            for p in range(2):
                acc = None
                for g in range(8 * p, 8 * p + 8):
                    src = y_ref[g, cl * batch:(cl + 1) * batch, 128 * (s // 8):128 * (s // 8) + 128]
                    shift = ((g - s) % 8) * SSM_GROUP
                    v = pltpu.roll(src, shift, 1) if shift else src
                    acc = v if acc is None else jnp.where(blk == g % 8, v, acc)
                ya_scr[p, pl.ds(cl * S5_CHUNK + s, batch, stride=ROW_PITCH), :] = acc


def _start_row_gather(idx_ref, idx_row0, src_hbm, dst_ref, sem, part=0, n_parts=1, max_row=None):
    n = dst_ref.shape[0]
    for r in range(part * n // n_parts, (part + 1) * n // n_parts):
        row = idx_ref[idx_row0 + r // 128, r % 128]
        if max_row is not None:
            row = jnp.minimum(row, max_row)
        pltpu.make_async_copy(src_hbm.at[pl.ds(row, 1)], dst_ref.at[pl.ds(r, 1)],
                              sem).start(priority=r % 2)


def _wait_row_gather(src_hbm, dst_ref, sem):
    pltpu.make_async_copy(src_hbm.at[pl.ds(0, dst_ref.shape[0])], dst_ref, sem).wait()


def _weighted_expert_rows(rcol_ref, f_rows):
    return (rcol_ref[:, 1:2] * f_rows[:, 0:D_MODEL] + rcol_ref[:, 2:3] * f_rows[:, D_MODEL:2 * D_MODEL])


INPROJ_DOTS = 4


def _inproj_kernel(*refs, batch, n_ctx_tiles, moe_in):
    if moe_in:
        (dest_ref, xc_ref, xl_ref, fs_hbm, rcol_ref, modp_ref, mod_ref, g_ref, w_ref,
         x2_ref, u_ref, ub_ref, gate_ref, h_scr, ua_scr, f_buf, sem) = refs
    else:
        xc_ref, xl_ref, mod_ref, g_ref, w_ref, u_ref, ub_ref, gate_ref, h_scr, ua_scr = refs
    rows = batch * TILE_TOK
    j = pl.program_id(0)
    last = pl.num_programs(0) - 1
    slot = j % 2
    x = _pick(xc_ref, xl_ref, j, n_ctx_tiles)

    def fetch(part):
        if moe_in:
            _start_row_gather(dest_ref, jnp.minimum(j + 1, last) * (rows // 128), fs_hbm,
                              f_buf.at[1 - slot], sem.at[1 - slot], part, INPROJ_DOTS)

    if moe_in:
        @pl.when(j == 0)
        def _():
            _start_row_gather(dest_ref, 0, fs_hbm, f_buf.at[0], sem.at[0])

        _wait_row_gather(fs_hbm, f_buf.at[slot], sem.at[slot])
        f = _weighted_expert_rows(rcol_ref, f_buf[slot])
        x = x + modp_ref[:, :, 5 * D_MODEL:6 * D_MODEL] * f.reshape(batch, TILE_TOK, D_MODEL)
        x2_ref[...] = x
    y = x * lax.rsqrt(jnp.mean(x * x, axis=-1, keepdims=True) + EPS) * g_ref[...]
    shift = mod_ref[:, :, 0:D_MODEL]
    scale = mod_ref[:, :, D_MODEL:2 * D_MODEL]
    h_scr[...] = (y * (1.0 + scale) + shift).reshape(rows, D_MODEL).astype(BF16)
    h = h_scr[...]
    fetch(0)
    ua = jnp.dot(h, w_ref[:, 0:D_SSM], preferred_element_type=F32)
    for b in range(batch):
        for p in range(2):
            ua_scr[p, b * ROW_PITCH:b * ROW_PITCH + TILE_TOK, :] = (
                ua[b * TILE_TOK:(b + 1) * TILE_TOK, 128 * p:128 * p + 128])
    fetch(1)
    ub = jnp.dot(h, w_ref[:, D_SSM:GATE_OFF], preferred_element_type=F32)
    ub_ref[...] = ub.reshape(batch, TILE_TOK, D_POOL)
    for k in range(2):
        lo = GATE_OFF + k * D_MODEL
        fetch(2 + k)
        g = jnp.dot(h, w_ref[:, lo:lo + D_MODEL], preferred_element_type=F32)
        gate_ref[:, :, k * D_MODEL:(k + 1) * D_MODEL] = (
            jax.nn.sigmoid(g).astype(BF16).reshape(batch, TILE_TOK, D_MODEL))
    _to_chunk_layout(ua_scr, u_ref, batch)
    if moe_in:
        @pl.when(j == last)
        def _():
            _wait_row_gather(fs_hbm, f_buf.at[1 - slot], sem.at[1 - slot])


def _inproj(x_ctx, x_lat, lat_off, mod_sel, norm_g, w_in_bf, layer, batch, n_ctx_tiles, n_tiles,
            moe=None):
    rows = batch * TILE_TOK
    n_tok = n_tiles * TILE_TOK
    chunk_rows = batch * TILE_TOK // S5_CHUNK
    tok = lambda cols: pl.BlockSpec((batch, TILE_TOK, cols), lambda j, *_: (0, j, 0))
    out_shape = [jax.ShapeDtypeStruct((N_SSM_GROUPS, n_tiles * chunk_rows, D_SSM), F32),
                 jax.ShapeDtypeStruct((batch, n_tok, D_POOL), F32),
                 jax.ShapeDtypeStruct((batch, n_tok, 2 * D_MODEL), BF16)]
    out_specs = [pl.BlockSpec((N_SSM_GROUPS, chunk_rows, D_SSM), lambda j, *_: (0, j, 0)),
                 tok(D_POOL), tok(2 * D_MODEL)]
    in_specs = _src_specs(D_MODEL, batch, n_ctx_tiles, 0, lat_off)
    args = [x_ctx, x_lat]
    scratch = [pltpu.VMEM((rows, D_MODEL), BF16), pltpu.VMEM((2, batch * ROW_PITCH, 128), F32)]
    prefetch = []
    if moe is not None:
        dest, fs, rcol = moe
        prefetch = [dest]
        in_specs += [pl.BlockSpec(memory_space=pl.ANY),
                     pl.BlockSpec((rows, ROUTE_LANES), lambda j, *_: (j, 0)),
                     _mod_spec(batch, layer - 1, n_ctx_tiles, 0)]
        args += [fs, rcol, mod_sel]
        out_shape.insert(0, jax.ShapeDtypeStruct((batch, n_tok, D_MODEL), F32))
        out_specs.insert(0, tok(D_MODEL))
        scratch += [pltpu.VMEM((2, rows, 2 * D_MODEL), F32), pltpu.SemaphoreType.DMA((2,))]
    in_specs += [_mod_spec(batch, layer, n_ctx_tiles, 0),
                 pl.BlockSpec((None, 1, D_MODEL), lambda j, *_: (layer, 0, 0)),
                 pl.BlockSpec((None, D_MODEL, D_IN), lambda j, *_: (layer, 0, 0))]
    args += [mod_sel, norm_g, w_in_bf]
    return pl.pallas_call(
        functools.partial(_inproj_kernel, batch=batch, n_ctx_tiles=n_ctx_tiles, moe_in=moe is not None),
        out_shape=tuple(out_shape),
        grid_spec=pltpu.PrefetchScalarGridSpec(
            num_scalar_prefetch=len(prefetch), grid=(n_tiles,), in_specs=in_specs,
            out_specs=tuple(out_specs), scratch_shapes=scratch),
        compiler_params=_cparams("arbitrary"),
        name="inproj",
    )(*prefetch, *args)


def _s5_matrices(lam_re, lam_im, log_dt, b_re, b_im, c_re, c_im):
    L, G, N, H = S5_CHUNK, N_SSM_GROUPS, SSM_STATE, SSM_GROUP
    lr, li = lam_re.astype(F32), lam_im.astype(F32)
    dt = jnp.exp(log_dt.astype(F32))[..., None]
    zr, zi = lr * dt, li * dt
    k = jnp.arange(L + 1, dtype=F32)[:, None, None, None]
    pm = jnp.exp(zr[None] * k)
    pr, pi = pm * jnp.cos(zi[None] * k), pm * jnp.sin(zi[None] * k)
    nr, ni = pr[1] - 1.0, pi[1]
    den = lr * lr + li * li
    fr = (nr * lr + ni * li) / den
    fi = (ni * lr - nr * li) / den
    br, bi = b_re.astype(F32), b_im.astype(F32)
    bbr = fr[..., None] * br - fi[..., None] * bi
    bbi = fr[..., None] * bi + fi[..., None] * br
    cr, ci = c_re.astype(F32), c_im.astype(F32)
    cpr = cr[None] * pr[:, :, :, None, :] - ci[None] * pi[:, :, :, None, :]
    cpi = cr[None] * pi[:, :, :, None, :] + ci[None] * pr[:, :, :, None, :]
    kern = (jnp.einsum('kdgan,dgnh->kdgah', cpr, bbr)
            - jnp.einsum('kdgan,dgnh->kdgah', cpi, bbi))
    s_idx = np.arange(L)[:, None]
    t_idx = np.arange(L)[None, :]
    lag_f = np.clip(t_idx - s_idx, 0, L - 1)
    lag_b = np.clip(s_idx - t_idx, 0, L - 1)
    mf = jnp.where((t_idx >= s_idx)[:, :, None, None, None], kern[:, 0][lag_f], 0.0)
    mb = jnp.where((s_idx >= t_idx)[:, :, None, None, None], kern[:, 1][lag_b], 0.0)
    m = (mf + mb).transpose(2, 0, 4, 1, 3).reshape(G, L * H, L * H)
    powers = (L - 1 - np.arange(L), np.arange(L))
    inj = []
    for d in range(2):
        p_r, p_i = pr[powers[d], d], pi[powers[d], d]
        wr = p_r[..., None] * bbr[d][None] - p_i[..., None] * bbi[d][None]
        wi = p_r[..., None] * bbi[d][None] + p_i[..., None] * bbr[d][None]
        inj.append((wr, wi))
    w_inj = jnp.concatenate([inj[0][0], inj[1][0], inj[0][1], inj[1][1]], axis=2)
    mcat = jnp.concatenate([m, w_inj.transpose(1, 0, 3, 2).reshape(G, L * H, 4 * N)], axis=2)
    rd = (np.arange(L) + 1, L - np.arange(L))
    w_rd = jnp.concatenate([cpr[rd[0], 0], cpr[rd[1], 1], -cpi[rd[0], 0], -cpi[rd[1], 1]],
                           axis=3)
    wy = w_rd.transpose(1, 3, 0, 2).reshape(G, 4 * N, L * H)
    a_r = jnp.concatenate([pr[L][0], pr[L][1]], axis=-1)
    a_i = jnp.concatenate([pi[L][0], pi[L][1]], axis=-1)
    coef = jnp.stack([a_r, a_i] + [jnp.zeros_like(a_r)] * 6, axis=1)
    return mcat.astype(BF16), wy.astype(BF16), coef


def _s5_kernel(u_ref, mcat_ref, wy_ref, coef_ref, dskip_ref, y_ref, s_scr, h_scr, *, batch,
               n_chunks, n_ctx_chunks):
    n = SSM_STATE
    rb = S5_CHUNK * batch
    n_rb = n_chunks // S5_CHUNK

    def inject(i, c):
        r0 = pl.multiple_of(i * rb, rb)
        u = u_ref[pl.ds(r0, rb), :]
        ub = u.astype(BF16)
        y_ref[pl.ds(r0, rb), :] = (jnp.dot(ub, mcat_ref[:, 0:D_SSM], preferred_element_type=F32)
                                   + dskip_ref[...] * u)
        s_scr[pl.ds(r0, rb), :] = jnp.dot(ub, mcat_ref[:, D_SSM:2 * D_SSM],
                                          preferred_element_type=F32)
        return c

    lax.fori_loop(0, n_rb, inject, 0)
    a_r, a_i = coef_ref[0:1, :], coef_ref[1:2, :]
    fwd_lanes = lax.broadcasted_iota(jnp.int32, (batch, 2 * n), 1) < n

    def step(k, carry):
        h_re, h_im = carry
        cb = jnp.where(k < n_ctx_chunks, n_ctx_chunks - 1 - k, n_chunks - 1 + n_ctx_chunks - k)
        rf = pl.multiple_of(k * batch, batch)
        rk = pl.multiple_of(cb * batch, batch)
        h_scr[pl.ds(rf, batch), 0:n] = h_re[:, 0:n]
        h_scr[pl.ds(rk, batch), n:2 * n] = h_re[:, n:2 * n]
        h_scr[pl.ds(rf, batch), 2 * n:3 * n] = h_im[:, 0:n]
        h_scr[pl.ds(rk, batch), 3 * n:4 * n] = h_im[:, n:2 * n]
        s_re = jnp.where(fwd_lanes, s_scr[pl.ds(rf, batch), 0:2 * n], s_scr[pl.ds(rk, batch), 0:2 * n])
        s_im = jnp.where(fwd_lanes, s_scr[pl.ds(rf, batch), 2 * n:4 * n],
                         s_scr[pl.ds(rk, batch), 2 * n:4 * n])
        return (a_r * h_re - a_i * h_im + s_re, a_r * h_im + a_i * h_re + s_im)

    zero = jnp.zeros((batch, 2 * n), F32)
    lax.fori_loop(0, n_chunks, step, (zero, zero), unroll=4)

    def readout(i, c):
        r0 = pl.multiple_of(i * rb, rb)
        y_ref[pl.ds(r0, rb), :] += jnp.dot(h_scr[pl.ds(r0, rb), :].astype(BF16), wy_ref[...],
                                           preferred_element_type=F32)
        return c

    lax.fori_loop(0, n_rb, readout, 0)


def _s5(u_t, mcat, wy, coef, dskip, batch, n_chunks, n_ctx_chunks):
    G = N_SSM_GROUPS
    rows = u_t.shape[1]
    kern = functools.partial(_s5_kernel, batch=batch, n_chunks=n_chunks, n_ctx_chunks=n_ctx_chunks)
    per_group = lambda *blk: pl.BlockSpec((None,) + blk, lambda g: (g, 0, 0))
    return pl.pallas_call(
        kern,
        out_shape=jax.ShapeDtypeStruct((G, rows, D_SSM), F32),
        grid=(G,),
        in_specs=[per_group(rows, D_SSM), per_group(D_SSM, 2 * D_SSM), per_group(D_SSM, D_SSM),
                  per_group(8, 2 * SSM_STATE), per_group(1, D_SSM)],
        out_specs=per_group(rows, D_SSM),
        scratch_shapes=[pltpu.VMEM((rows, D_SSM), F32), pltpu.VMEM((rows, D_SSM), F32)],
        compiler_params=_cparams("arbitrary"),
        name="s5",
    )(u_t, mcat, wy, coef, dskip)


def _window(w):
    return -(w // 2), w - 1 - w // 2


def _pool_constants(rows, width):
    rpb = POOL_BLOCK // width
    pm = np.zeros((4, POOL_BLOCK, POOL_BLOCK), np.float32)
    inv = np.zeros((rows * width, 4), np.float32)
    col = np.arange(width)
    row = np.arange(rows)
    for i, w in enumerate(POOL_WINDOWS):
        lo, hi = _window(w)
        c0, c1 = np.clip(col + lo, 0, width - 1), np.clip(col + hi, 0, width - 1)
        r0, r1 = np.clip(row + lo, 0, rows - 1), np.clip(row + hi, 0, rows - 1)
        band = ((col[None, :] >= c0[:, None]) & (col[None, :] <= c1[:, None])).astype(np.float32)
        for r in range(rpb):
            pm[i, r * width:(r + 1) * width, r * width:(r + 1) * width] = band
        cnt = (r1 - r0 + 1)[:, None] * (c1 - c0 + 1)[None, :]
        inv[:, i] = (1.0 / cnt).reshape(-1)
    return jnp.asarray(pm, BF16), jnp.asarray(inv, F32)


def _pool_segment(u_ref, o_ref, cs_scr, pm_ref, inv_ref, wp_ref, ps_ref, tok0, rows, width):
    n_tok = rows * width
    n_blk = n_tok // POOL_BLOCK
    pad = 8 * width if rows > 1 else 0
    if rows > 1:
        zeros = jnp.zeros((pad, 2 * 128), F32)
        for i in range(4):
            cs_scr[i, 0:pad, :] = zeros
            cs_scr[i, pad + n_tok:pad + n_tok + pad, :] = zeros
    for b in range(n_blk):
        t0 = b * POOL_BLOCK
        for i in range(4):
            seg = POOL_SEG[i]
            xb = u_ref[tok0 + t0:tok0 + t0 + POOL_BLOCK, seg:seg + 256].astype(BF16)
            cs_scr[i, pad + t0:pad + t0 + POOL_BLOCK, :] = jnp.dot(
                pm_ref[i], xb, preferred_element_type=F32)
    lane = lax.broadcasted_iota(jnp.int32, (POOL_BLOCK, 128), 1)
    low_half = lane < 64
    for b in range(n_blk):
        t0 = b * POOL_BLOCK
        win = []
        for i, w in enumerate(POOL_WINDOWS):
            lo, hi = _window(w) if rows > 1 else (0, 0)
            acc = None
            for k in range(lo, hi + 1):
                base = pad + t0 + k * width
                piece = cs_scr[i, base:base + POOL_BLOCK, :]
                acc = piece if acc is None else acc + piece
            win.append(acc * inv_ref[t0:t0 + POOL_BLOCK, i:i + 1])
        x = u_ref[tok0 + t0:tok0 + t0 + POOL_BLOCK, :]
        pooled = jnp.concatenate([
            win[0][:, 0:128],
            jnp.where(low_half, win[0][:, 128:256], win[1][:, 0:128]),
            win[1][:, 128:256],
            win[2][:, 0:128],
            jnp.where(low_half, win[2][:, 128:256], win[3][:, 0:128]),
            win[3][:, 128:256]], axis=1)
        d = (pooled - x).astype(BF16)
        y0 = jnp.dot(d[:, 0:384], wp_ref[0], preferred_element_type=F32)
        y1 = jnp.dot(d[:, 384:768], wp_ref[1], preferred_element_type=F32)
        y = jnp.concatenate([y0, y1], axis=1) * ps_ref[...]
        o_ref[tok0 + t0:tok0 + t0 + POOL_BLOCK, :] = y.astype(BF16)


def _pool_kernel(u_ref, pmc_ref, invc_ref, pml_ref, invl_ref, wp_ref, ps_ref, o_ref, cs_scr, *,
                 lat_rows, with_ctx):
    if with_ctx:
        _pool_segment(u_ref, o_ref, cs_scr, pmc_ref, invc_ref, wp_ref, ps_ref, 0, 1, CTX_LEN)
    else:
        o_ref[0:CTX_LEN, :] = jnp.zeros((CTX_LEN, D_POOL), BF16)
    _pool_segment(u_ref, o_ref, cs_scr, pml_ref, invl_ref, wp_ref, ps_ref, CTX_LEN, lat_rows, GRID_W)


def _pool(ub, wp2, pscale, layer, with_ctx):
    batch, n_tok, _ = ub.shape
    lat_rows = (n_tok - CTX_LEN) // GRID_W
    pmc, invc = _pool_constants(1, CTX_LEN)
    pml, invl = _pool_constants(lat_rows, GRID_W)
    const = lambda *blk: pl.BlockSpec(blk, lambda b: (0,) * len(blk))
    return pl.pallas_call(
        functools.partial(_pool_kernel, lat_rows=lat_rows, with_ctx=with_ctx),
        out_shape=jax.ShapeDtypeStruct((batch, n_tok, D_POOL), BF16),
        grid=(batch,),
        in_specs=[
            pl.BlockSpec((None, n_tok, D_POOL), lambda b: (b, 0, 0)),
            const(4, POOL_BLOCK, POOL_BLOCK), const(CTX_LEN, 4),
            const(4, POOL_BLOCK, POOL_BLOCK), const(lat_rows * GRID_W, 4),
            pl.BlockSpec((None, 2, 384, 384), lambda b: (layer, 0, 0, 0)),
            pl.BlockSpec((None, 1, D_POOL), lambda b: (layer, 0, 0)),
        ],
        out_specs=pl.BlockSpec((None, n_tok, D_POOL), lambda b: (b, 0, 0)),
        scratch_shapes=[pltpu.VMEM((4, lat_rows * GRID_W + 16 * GRID_W, 256), F32)],
        compiler_params=_cparams("arbitrary"),
        name="pool",
    )(ub, pmc, invc, pml, invl, wp2, pscale)


def _first_argmax(rows_):
    best, idx = rows_[0], jnp.zeros_like(rows_[0], dtype=jnp.int32)
    for k in range(1, len(rows_)):
        take = rows_[k] > best
        idx = jnp.where(take, k, idx)
        best = jnp.where(take, rows_[k], best)
    return best, idx


def _route(lt):
    g_rows = [lt[k:k + 1, :] for k in range(N_EXPERT_GROUPS)]
    best, grp = _first_argmax(g_rows)
    denom = sum(jnp.exp(r - best) for r in g_rows)
    p_grp = 1.0 / denom
    inner = []
    for e in range(EXPERTS_PER_GROUP):
        acc = jnp.zeros_like(best)
        for g in range(N_EXPERT_GROUPS):
            r = 4 + 4 * g + e
            acc = jnp.where(grp == g, lt[r:r + 1, :], acc)
        inner.append(acc)
    v1, i1 = _first_argmax(inner)
    masked = [jnp.where(i1 == e, -jnp.inf, inner[e]) for e in range(EXPERTS_PER_GROUP)]
    v2, i2 = _first_argmax(masked)
    e21 = jnp.exp(v2 - v1)
    w1 = p_grp / (1.0 + e21)
    w2 = p_grp * e21 / (1.0 + e21)
    first_low = i1 < i2
    lo = jnp.where(first_low, i1, i2)
    hi = jnp.where(first_low, i2, i1)
    w_lo = jnp.where(first_low, w1, w2)
    w_hi = jnp.where(first_low, w2, w1)
    off = jnp.where(lo == 0, 0, jnp.where(lo == 1, 3, 5))
    cls = N_PAIRS * grp + off + hi - lo - 1
    return cls.astype(F32), w_lo, w_hi


def _merge_kernel(xc_ref, xl_ref, y_ref, yb_ref, gate_ref, mod_ref, g2_ref, wglu_ref, bglu_ref,
                  wbra_ref, wbrb_ref, wout_ref, wrh_ref, wrl_ref, br_ref, x1_ref, h2_ref, route_ref,
                  rcol_ref, ya_scr, *, batch, n_ctx_tiles, tile_off):
    D = D_MODEL
    rows = batch * TILE_TOK
    tile = pl.program_id(0) + tile_off
    _from_chunk_layout(y_ref, ya_scr, batch)
    y = jnp.concatenate(
        [jnp.concatenate([ya_scr[p, b * ROW_PITCH:b * ROW_PITCH + TILE_TOK, :] for p in range(2)],
                         axis=1) for b in range(batch)], axis=0)
    z = jax.nn.gelu(y)
    glu = jax.nn.sigmoid(jnp.dot(z.astype(BF16), wglu_ref[...], preferred_element_type=F32)
                         + bglu_ref[...])
    ya = (z * glu).astype(BF16)
    gates = gate_ref[...].reshape(rows, 2 * D)
    m = gates[:, 0:D].astype(F32) * jnp.dot(ya, wbra_ref[...], preferred_element_type=F32)
    m += gates[:, D:2 * D].astype(F32) * jnp.dot(yb_ref[...].reshape(rows, D_POOL), wbrb_ref[...],
                                                 preferred_element_type=F32)
    out = jnp.dot(m.astype(BF16), wout_ref[...], preferred_element_type=F32)
    x_in = _pick(xc_ref, xl_ref, tile, n_ctx_tiles)
    x1 = x_in + mod_ref[:, :, 2 * D:3 * D] * out.reshape(batch, TILE_TOK, D)
    x1_ref[...] = x1
    yn = x1 * lax.rsqrt(jnp.mean(x1 * x1, axis=-1, keepdims=True) + EPS) * g2_ref[...]
    h2 = yn * (1.0 + mod_ref[:, :, 4 * D:5 * D]) + mod_ref[:, :, 3 * D:4 * D]
    h2_ref[...] = h2.reshape(rows, D)
    h_hi, h_lo = _split_bf16(h2.reshape(rows, D))
    logits = jnp.dot(h_hi, wrh_ref[...], preferred_element_type=F32)
    logits += jnp.dot(h_lo, wrh_ref[...], preferred_element_type=F32)
    logits += jnp.dot(h_hi, wrl_ref[...], preferred_element_type=F32)
    logits += br_ref[...]
    cls, w_lo, w_hi = _route(logits.T)
    zero = jnp.zeros_like(cls)
    route_ref[...] = jnp.concatenate([cls, w_lo, w_hi, zero, zero, zero, zero, zero], axis=0)
    cols = jnp.concatenate([cls, w_lo, w_hi, jnp.zeros((ROUTE_LANES - 3, rows), F32)], axis=0)
    rcol_ref[...] = cols.T


def _merge(x_ctx, x_lat, lat_off, y_t, yb, gates, mod_sel, norm2_g, p, layer, batch, tile_off,
           n_tiles, n_ctx_tiles):
    rows = batch * TILE_TOK
    chunk_rows = rows // S5_CHUNK
    tok = lambda cols: pl.BlockSpec((batch, TILE_TOK, cols), lambda j: (0, j + tile_off, 0))
    lay3 = lambda *blk: pl.BlockSpec((None,) + blk, lambda j: (layer, 0, 0))
    n_tok = n_tiles * TILE_TOK
    return pl.pallas_call(
        functools.partial(_merge_kernel, batch=batch, n_ctx_tiles=n_ctx_tiles, tile_off=tile_off),
        out_shape=(jax.ShapeDtypeStruct((batch, n_tok, D_MODEL), F32),
                   jax.ShapeDtypeStruct((n_tiles * rows, D_MODEL), F32),
                   jax.ShapeDtypeStruct((8, n_tiles * rows), F32),
                   jax.ShapeDtypeStruct((n_tiles * rows, ROUTE_LANES), F32)),
        grid=(n_tiles,),
        in_specs=_src_specs(D_MODEL, batch, n_ctx_tiles, tile_off, lat_off) + [
            pl.BlockSpec((N_SSM_GROUPS, chunk_rows, D_SSM), lambda j: (0, j + tile_off, 0)),
            tok(D_POOL), tok(2 * D_MODEL),
            _mod_spec(batch, layer, n_ctx_tiles, tile_off),
            lay3(1, D_MODEL), lay3(D_SSM, D_SSM), lay3(1, D_SSM), lay3(D_SSM, D_MODEL),
            lay3(D_POOL, D_MODEL), lay3(D_MODEL, D_MODEL), lay3(D_MODEL, ROUTE_LANES),
            lay3(D_MODEL, ROUTE_LANES), lay3(1, ROUTE_LANES),
        ],
        out_specs=(pl.BlockSpec((batch, TILE_TOK, D_MODEL), lambda j: (0, j, 0)),
                   pl.BlockSpec((rows, D_MODEL), lambda j: (j, 0)),
                   pl.BlockSpec((8, rows), lambda j: (0, j)),
                   pl.BlockSpec((rows, ROUTE_LANES), lambda j: (j, 0))),
        scratch_shapes=[pltpu.VMEM((2, batch * ROW_PITCH, 128), F32)],
        compiler_params=_cparams("arbitrary"),
        name="merge",
    )(x_ctx, x_lat, y_t, yb, gates, mod_sel, norm2_g, p['w_glu'], p['b_glu'], p['w_br_a'],
      p['w_br_b'], p['w_out'], p['wr_hi'], p['wr_lo'], p['b_r'])


def _plan_kernel(cls_ref, dest_ref, src_ref, meta_ref, dest_smem, fill_scr, sem, *, n_q):
    rows_q = n_q // 128
    cls = cls_ref[...]
    r_i = lax.broadcasted_iota(jnp.int32, (128, 128), 0)
    c_i = lax.broadcasted_iota(jnp.int32, (128, 128), 1)
    upper = jnp.where(r_i <= c_i, 1.0, 0.0).astype(BF16)
    r_q = lax.broadcasted_iota(jnp.int32, (rows_q, rows_q), 0)
    c_q = lax.broadcasted_iota(jnp.int32, (rows_q, rows_q), 1)
    lower = jnp.where(c_q < r_q, 1.0, 0.0).astype(BF16)
    tile_start = (lax.broadcasted_iota(jnp.int32, (1, META_LANES), 1) * EXPERT_TILE).astype(F32)
    start = jnp.zeros((1, 128), F32)
    dest = jnp.zeros((rows_q, 128), F32)
    tile_cls = jnp.zeros((1, META_LANES), F32)
    for c in range(N_CLASSES):
        m = cls == float(c)
        incl = jnp.dot(jnp.where(m, 1.0, 0.0).astype(BF16), upper, preferred_element_type=F32)
        row_tot = jnp.broadcast_to(incl[:, 127:128], (rows_q, 128))
        before = jnp.dot(lower, row_tot.astype(BF16), preferred_element_type=F32)
        dest = jnp.where(m, start + before + incl - 1.0, dest)
        total = before[rows_q - 1:rows_q, :] + row_tot[rows_q - 1:rows_q, :]
        start = start + jnp.floor((total + (EXPERT_TILE - 1.0)) * (1.0 / EXPERT_TILE)) * EXPERT_TILE
        end2 = jnp.concatenate([start, start], axis=1)
        tile_cls = tile_cls + jnp.where(tile_start >= end2, 1.0, 0.0)
    active = jnp.where(tile_start < end2, 1.0, 0.0)
    tc = jnp.minimum(tile_cls, N_CLASSES - 1.0)
    grp = sum(jnp.where(tc >= float(N_PAIRS * k), 1.0, 0.0) for k in range(1, N_EXPERT_GROUPS))
    pair = tc - N_PAIRS * grp
    p_lo = jnp.where(pair >= 3.0, 1.0, 0.0) + jnp.where(pair >= 5.0, 1.0, 0.0)
    p_hi = jnp.where(pair == 0.0, 1.0, jnp.where(pair == 1.0, 2.0, jnp.where(pair == 3.0, 2.0, 3.0)))
    lane = lax.broadcasted_iota(jnp.int32, (1, META_LANES), 1)
    fresh = jnp.where((lane == 0) | (tc != pltpu.roll(tc, 1, 1)), 1.0, 0.0)
    zero = jnp.zeros_like(tc)
    meta_ref[...] = jnp.concatenate(
        [EXPERTS_PER_GROUP * grp + p_lo, EXPERTS_PER_GROUP * grp + p_hi, active, fresh,
         zero, zero, zero, zero], axis=0).astype(jnp.int32)
    dest_ref[...] = dest.astype(jnp.int32)

    fill_scr[...] = jnp.full(fill_scr.shape, n_q, jnp.int32)
    fill = pltpu.make_async_copy(fill_scr, src_ref, sem.at[0])
    stage = pltpu.make_async_copy(dest_ref, dest_smem, sem.at[1])
    fill.start()
    stage.start()
    fill.wait()
    stage.wait()

    def invert(i, carry):
        for k in range(128):
            d = dest_smem[i, k]
            src_ref[d >> 7, d & 127] = i * 128 + k
        return carry

    lax.fori_loop(0, rows_q, invert, 0)


def _plan(cls_q):
    n_q = cls_q.shape[0]
    rows_q = n_q // 128
    n_tiles = n_q // EXPERT_TILE + N_CLASSES
    assert n_tiles <= META_LANES
    rows_p = n_tiles * EXPERT_TILE // 128
    i32 = jnp.int32
    return pl.pallas_call(
        functools.partial(_plan_kernel, n_q=n_q),
        out_shape=(jax.ShapeDtypeStruct((rows_q, 128), i32), jax.ShapeDtypeStruct((rows_p, 128), i32),
                   jax.ShapeDtypeStruct((8, META_LANES), i32)),
        in_specs=[pl.BlockSpec(memory_space=pltpu.VMEM)],
        out_specs=(pl.BlockSpec(memory_space=pltpu.VMEM), pl.BlockSpec(memory_space=pltpu.SMEM),
                   pl.BlockSpec(memory_space=pltpu.VMEM)),
        scratch_shapes=[pltpu.SMEM((rows_q, 128), i32), pltpu.VMEM((rows_p, 128), i32),
                        pltpu.SemaphoreType.DMA((2,))],
        compiler_params=pltpu.CompilerParams(vmem_limit_bytes=V7X_VMEM_LIMIT),
        name="plan",
    )(cls_q.reshape(rows_q, 128))


GATHER_SLOTS = 3
EXPERT_DOTS = 6


def _expert_kernel(src_ref, elo_ref, ehi_ref, act_ref, fresh_ref, h2_hbm, w1a_ref, w1b_ref, w3a_ref,
                   w3b_ref, w2a_ref, w2b_ref, o_ref, x_buf, w1a_bf, w1b_bf, w3a_bf, w3b_bf, w2a_bf,
                   w2b_bf, sem, *, n_q):
    j = pl.program_id(0)
    last = pl.num_programs(0) - 1
    ahead = GATHER_SLOTS - 1
    slot = lax.rem(j, GATHER_SLOTS)
    idx_rows = EXPERT_TILE // 128

    def fetch(tile, into, part=0, n_parts=1):
        _start_row_gather(src_ref, jnp.minimum(tile, last) * idx_rows, h2_hbm, x_buf.at[into],
                          sem.at[into], part, n_parts, n_q - 1)

    @pl.when(j == 0)
    def _():
        for t in range(ahead):
            fetch(t, t)

    _wait_row_gather(h2_hbm, x_buf.at[slot], sem.at[slot])
    nxt, nxt_slot = j + ahead, lax.rem(j + ahead, GATHER_SLOTS)

    @pl.when(fresh_ref[j] == 1)
    def _():
        for src_w, dst_w in ((w1a_ref, w1a_bf), (w1b_ref, w1b_bf), (w3a_ref, w3a_bf),
                             (w3b_ref, w3b_bf), (w2a_ref, w2a_bf), (w2b_ref, w2b_bf)):
            dst_w[...] = src_w[...].astype(BF16)

    @pl.when(act_ref[j] == 1)
    def _():
        x = x_buf[slot].astype(BF16)
        for k, (w1, w3, w2) in enumerate(((w1a_bf, w3a_bf, w2a_bf), (w1b_bf, w3b_bf, w2b_bf))):
            fetch(nxt, nxt_slot, 3 * k, EXPERT_DOTS)
            a = jnp.dot(x, w1[...], preferred_element_type=F32)
            fetch(nxt, nxt_slot, 3 * k + 1, EXPERT_DOTS)
            b = jnp.dot(x, w3[...], preferred_element_type=F32)
            h = (a * jax.nn.sigmoid(a) * b).astype(BF16)
            fetch(nxt, nxt_slot, 3 * k + 2, EXPERT_DOTS)
            o_ref[:, k * D_MODEL:(k + 1) * D_MODEL] = jnp.dot(h, w2[...], preferred_element_type=F32)

    @pl.when(act_ref[j] == 0)
    def _():
        fetch(nxt, nxt_slot)
        o_ref[...] = jnp.zeros_like(o_ref)

    @pl.when(j == last)
    def _():
        for t in range(1, GATHER_SLOTS):
            into = lax.rem(j + t, GATHER_SLOTS)
            _wait_row_gather(h2_hbm, x_buf.at[into], sem.at[into])


def _experts(src, meta, h2, w1, w3, w2, layer):
    n_q = h2.shape[0]
    n_rows = src.shape[0] * 128
    te = EXPERT_TILE
    n_tiles = n_rows // te
    up = lambda sel: pl.BlockSpec((None, None, D_MODEL, D_EXPERT),
                                  lambda j, s, lo, hi, act, fr: (layer, (lo, hi)[sel][j], 0, 0))
    down = lambda sel: pl.BlockSpec((None, None, D_EXPERT, D_MODEL),
                                    lambda j, s, lo, hi, act, fr: (layer, (lo, hi)[sel][j], 0, 0))
    return pl.pallas_call(
        functools.partial(_expert_kernel, n_q=n_q),
        out_shape=jax.ShapeDtypeStruct((n_rows, 2 * D_MODEL), F32),
        grid_spec=pltpu.PrefetchScalarGridSpec(
            num_scalar_prefetch=5,
            grid=(n_tiles,),
            in_specs=[pl.BlockSpec(memory_space=pl.ANY),
                      up(0), up(1), up(0), up(1), down(0), down(1)],
            out_specs=pl.BlockSpec((te, 2 * D_MODEL), lambda j, s, lo, hi, act, fr: (j, 0)),
            scratch_shapes=[pltpu.VMEM((GATHER_SLOTS, te, D_MODEL), F32)]
            + [pltpu.VMEM((D_MODEL, D_EXPERT), BF16)] * 4 + [pltpu.VMEM((D_EXPERT, D_MODEL), BF16)] * 2
            + [pltpu.SemaphoreType.DMA((GATHER_SLOTS,))],
        ),
        compiler_params=_cparams("arbitrary"),
        name="experts",
    )(src, meta[0, :n_tiles], meta[1, :n_tiles], meta[2, :n_tiles], meta[3, :n_tiles],
      h2, w1, w1, w3, w3, w2, w2)


def _combine_kernel(dest_ref, x_ref, fs_hbm, rcol_ref, mod_ref, g_ref, o_ref, f_buf, sem, *, batch,
                    final):
    j = pl.program_id(0)
    last = pl.num_programs(0) - 1
    slot = j % 2
    rows = batch * TILE_TOK
    idx_rows = rows // 128

    @pl.when(j == 0)
    def _():
        _start_row_gather(dest_ref, 0, fs_hbm, f_buf.at[0], sem.at[0])

    _wait_row_gather(fs_hbm, f_buf.at[slot], sem.at[slot])
    _start_row_gather(dest_ref, jnp.minimum(j + 1, last) * idx_rows, fs_hbm, f_buf.at[1 - slot],
                      sem.at[1 - slot])
    f = _weighted_expert_rows(rcol_ref, f_buf[slot])
    x = x_ref[...] + mod_ref[:, :, 5 * D_MODEL:6 * D_MODEL] * f.reshape(batch, TILE_TOK, D_MODEL)
    if final:
        x = x * lax.rsqrt(jnp.mean(x * x, axis=-1, keepdims=True) + EPS) * g_ref[...]
    o_ref[...] = x

    @pl.when(j == last)
    def _():
        _wait_row_gather(fs_hbm, f_buf.at[1 - slot], sem.at[1 - slot])


def _combine(dest, x1, fs, rcol, mod_sel, final_g, layer, tile_off, n_ctx_tiles, final):
    batch, n_tok, _ = x1.shape
    rows = batch * TILE_TOK
    tok = pl.BlockSpec((batch, TILE_TOK, D_MODEL), lambda j, d: (0, j, 0))
    mod_spec = _mod_spec(batch, layer, n_ctx_tiles, tile_off)
    return pl.pallas_call(
        functools.partial(_combine_kernel, batch=batch, final=final),
        out_shape=jax.ShapeDtypeStruct((batch, n_tok, D_MODEL), F32),
        grid_spec=pltpu.PrefetchScalarGridSpec(
            num_scalar_prefetch=1,
            grid=(n_tok // TILE_TOK,),
            in_specs=[tok, pl.BlockSpec(memory_space=pl.ANY),
                      pl.BlockSpec((rows, ROUTE_LANES), lambda j, d: (j, 0)),
                      pl.BlockSpec(mod_spec.block_shape, lambda j, d: mod_spec.index_map(j)),
                      pl.BlockSpec((1, D_MODEL), lambda j, d: (0, 0))],
            out_specs=tok,
            scratch_shapes=[pltpu.VMEM((2, rows, 2 * D_MODEL), F32), pltpu.SemaphoreType.DMA((2,))],
        ),
        compiler_params=_cparams("arbitrary"),
        name="combine",
    )(dest, x1, fs, rcol, mod_sel, final_g)


def kernel(x, c, ctx, c_ctx, norm1_g, norm2_g, w_mod, b_mod, w_in, lam_re, lam_im, log_dt, b_re,
           b_im, c_re, c_im, d_skip, w_glu, b_glu, w_pool, pool_scale, w_br_a, w_br_b, w_out, w_r1,
           b_r1, w_r2, b_r2, w1, w3, w2, final_g):
    batch, seq, d = x.shape
    assert d == D_MODEL and seq % (GRID_W * 8) == 0 and ctx.shape[1] == CTX_LEN and batch % 8 == 0
    n_ctx_tiles, n_lat_tiles = CTX_LEN // TILE_TOK, seq // TILE_TOK
    n_all_tiles = n_ctx_tiles + n_lat_tiles
    n_chunks = (CTX_LEN + seq) // S5_CHUNK
    mod_rows = ((batch + 1 + 7) // 8) * 8

    cc = jnp.concatenate([c, c_ctx[None, :], jnp.zeros((mod_rows - batch - 1, d), F32)], axis=0)
    mod = _modulation(cc, w_mod, b_mod)
    mod_sel = jnp.stack([jnp.broadcast_to(mod[:, batch:batch + 1], (DEPTH, batch, 6 * d)),
                         mod[:, :batch]], axis=1)[:, :, :, None, :]

    w_in_bf = w_in.astype(BF16)
    w_r = jnp.concatenate([w_r1, w_r2.transpose(0, 2, 1, 3).reshape(DEPTH, d, N_EXPERTS),
                           jnp.zeros((DEPTH, d, ROUTE_LANES - 4 - N_EXPERTS), F32)], axis=2)
    wr_hi = w_r.astype(BF16)
    wr_lo = (w_r - wr_hi.astype(F32)).astype(BF16)
    b_r = jnp.concatenate([b_r1, b_r2.reshape(DEPTH, N_EXPERTS),
                           jnp.zeros((DEPTH, ROUTE_LANES - 4 - N_EXPERTS), F32)], axis=1)
    zeros_p = jnp.zeros((DEPTH, POOL_GROUP, POOL_GROUP), F32)
    wp2 = jnp.stack([
        jnp.concatenate([jnp.concatenate([w_pool[:, 0], zeros_p], axis=2),
                         jnp.concatenate([zeros_p, w_pool[:, 1]], axis=2)], axis=1),
        jnp.concatenate([jnp.concatenate([w_pool[:, 2], zeros_p], axis=2),
                         jnp.concatenate([zeros_p, w_pool[:, 3]], axis=2)], axis=1)],
        axis=1).astype(BF16)
    params = {
        'w_glu': w_glu.astype(BF16), 'b_glu': b_glu.reshape(DEPTH, 1, D_SSM),
        'w_br_a': w_br_a.astype(BF16), 'w_br_b': w_br_b.astype(BF16), 'w_out': w_out.astype(BF16),
        'wr_hi': wr_hi, 'wr_lo': wr_lo, 'b_r': b_r.reshape(DEPTH, 1, ROUTE_LANES),
    }
    norm1 = norm1_g.reshape(DEPTH, 1, d)
    norm2 = norm2_g.reshape(DEPTH, 1, d)
    pscale = pool_scale.reshape(DEPTH, 1, D_POOL)
    dskip = jnp.tile(d_skip.reshape(DEPTH, N_SSM_GROUPS, 1, SSM_GROUP), (1, 1, 1, S5_CHUNK))

    x_ctx, x_lat, lat_off = ctx, x, 0
    moe = None
    for l in range(DEPTH):
        last = l == DEPTH - 1
        outs = _inproj(x_ctx, x_lat, lat_off, mod_sel, norm1, w_in_bf, l, batch, n_ctx_tiles,
                       n_all_tiles, moe)
        if moe is not None:
            x_ctx = x_lat = outs[0]
            outs = outs[1:]
        u_t, ub, gates = outs
        mcat, wy, coef = _s5_matrices(lam_re[l], lam_im[l], log_dt[l], b_re[l], b_im[l],
                                      c_re[l], c_im[l])
        y_t = _s5(u_t, mcat, wy, coef, dskip[l], batch, n_chunks, CTX_LEN // S5_CHUNK)
        yb = _pool(ub, wp2, pscale, l, not last)

        tile_off = n_ctx_tiles if last else 0
        n_tiles = n_lat_tiles if last else n_all_tiles
        x1, h2, route, rcol = _merge(x_ctx, x_lat, lat_off, y_t, yb, gates, mod_sel, norm2, params, l,
                                     batch, tile_off, n_tiles, n_ctx_tiles)
        dest, src, meta = _plan(route[0])
        fs = _experts(src, meta, h2, w1, w3, w2, l)
        if last:
            return _combine(dest, x1, fs, rcol, mod_sel, final_g.reshape(1, d), l, tile_off,
                            n_ctx_tiles, True)
        x_ctx, x_lat, lat_off, moe = x1, x1, n_ctx_tiles, (dest, fs, rcol)
```

```python
import functools

import numpy as np
import jax
import jax.numpy as jnp
from jax import lax
from jax.experimental import pallas as pl
from jax.experimental.pallas import tpu as pltpu

F32 = jnp.float32
BF16 = jnp.bfloat16

D_MODEL = 1024
DEPTH = 2
GRID_W = 64
CTX_LEN = 256
EPS = 1e-6

D_SSM = 256
SSM_GROUP = 16
N_SSM_GROUPS = 16
SSM_STATE = 64
S5_CHUNK = 16

D_POOL = 768
POOL_WINDOWS = (2, 4, 8, 16)
POOL_GROUP = 192
POOL_SEG = (0, 128, 384, 512)
GATE_OFF = D_SSM + D_POOL
D_IN = D_SSM + D_POOL + 2 * D_MODEL

N_EXPERT_GROUPS = 4
EXPERTS_PER_GROUP = 4
N_EXPERTS = 16
D_EXPERT = 512
N_PAIRS = 6
N_CLASSES = N_EXPERT_GROUPS * N_PAIRS
ROUTE_LANES = 128

TILE_TOK = 32
ROW_PITCH = TILE_TOK + 4
POOL_BLOCK = 256
EXPERT_TILE = 256
META_LANES = 256
V7X_VMEM_LIMIT = 56 * 1024 * 1024


def _cparams(*sem):
    return pltpu.CompilerParams(dimension_semantics=sem, vmem_limit_bytes=V7X_VMEM_LIMIT)


def _split_bf16(v):
    hi = v.astype(BF16)
    lo = (v - hi.astype(F32)).astype(BF16)
    return hi, lo


def _mod_kernel(c_ref, w_ref, b_ref, o_ref):
    c = c_ref[...]
    a = c * jax.nn.sigmoid(c)
    a_hi, a_lo = _split_bf16(a)
    w_hi, w_lo = _split_bf16(w_ref[...])
    acc = jnp.dot(a_hi, w_hi, preferred_element_type=F32)
    acc += jnp.dot(a_lo, w_hi, preferred_element_type=F32)
    acc += jnp.dot(a_hi, w_lo, preferred_element_type=F32)
    o_ref[...] = acc + b_ref[...]


def _modulation(cc, w_mod, b_mod):
    rows = cc.shape[0]
    nblk = 4
    cols = 6 * D_MODEL // nblk
    return pl.pallas_call(
        _mod_kernel,
        out_shape=jax.ShapeDtypeStruct((DEPTH, rows, 6 * D_MODEL), F32),
        grid=(DEPTH, nblk),
        in_specs=[
            pl.BlockSpec((rows, D_MODEL), lambda l, j: (0, 0)),
            pl.BlockSpec((None, D_MODEL, cols), lambda l, j: (l, 0, j)),
            pl.BlockSpec((None, 1, cols), lambda l, j: (l, 0, j)),
        ],
        out_specs=pl.BlockSpec((None, rows, cols), lambda l, j: (l, 0, j)),
        compiler_params=_cparams("arbitrary", "arbitrary"),
        name="modulation",
    )(cc, w_mod, b_mod.reshape(DEPTH, 1, 6 * D_MODEL))


def _pick(ctx_ref, lat_ref, tile, n_ctx_tiles):
    return jnp.where(tile < n_ctx_tiles, ctx_ref[...], lat_ref[...])


def _src_specs(cols, batch, n_ctx_tiles, tile_off, lat_off):
    blk = (batch, TILE_TOK, cols)
    return [pl.BlockSpec(blk, lambda j, *_: (0, jnp.minimum(j + tile_off, n_ctx_tiles - 1), 0)),
            pl.BlockSpec(blk, lambda j, *_: (0, jnp.maximum(j + tile_off - n_ctx_tiles, 0) + lat_off,
                                             0))]


def _mod_spec(batch, layer, n_ctx_tiles, tile_off):
    return pl.BlockSpec((None, None, batch, 1, 6 * D_MODEL),
                        lambda j, *_: (layer, (j + tile_off >= n_ctx_tiles).astype(jnp.int32), 0, 0, 0))


def _lane_group(batch):
    return lax.broadcasted_iota(jnp.int32, (batch, 128), 1) // SSM_GROUP


def _to_chunk_layout(ua_scr, u_ref, batch):
    blk = _lane_group(batch)
    for cl in range(TILE_TOK // S5_CHUNK):
        for g in range(N_SSM_GROUPS):
            for q in range(2):
                acc = None
                for s in range(8 * q, 8 * q + 8):
                    src = ua_scr[g // 8, pl.ds(cl * S5_CHUNK + s, batch, stride=ROW_PITCH), :]
                    shift = ((s - g) % 8) * SSM_GROUP
                    v = pltpu.roll(src, shift, 1) if shift else src
                    acc = v if acc is None else jnp.where(blk == s % 8, v, acc)
                u_ref[g, cl * batch:(cl + 1) * batch, 128 * q:128 * q + 128] = acc


def _from_chunk_layout(y_ref, ya_scr, batch):
    blk = _lane_group(batch)
    for cl in range(TILE_TOK // S5_CHUNK):
        for s in range(S5_CHUNK):
            for p in range(2):
                acc = None
                for g in range(8 * p, 8 * p + 8):
                    src = y_ref[g, cl * batch:(cl + 1) * batch, 128 * (s // 8):128 * (s // 8) + 128]
                    shift = ((g - s) % 8) * SSM_GROUP
                    v = pltpu.roll(src, shift, 1) if shift else src
                    acc = v if acc is None else jnp.where(blk == g % 8, v, acc)
                ya_scr[p, pl.ds(cl * S5_CHUNK + s, batch, stride=ROW_PITCH), :] = acc


def _start_row_gather(idx_ref, idx_row0, src_hbm, dst_ref, sem, part=0, n_parts=1, max_row=None):
    n = dst_ref.shape[0]
    for r in range(part * n // n_parts, (part + 1) * n // n_parts):
        row = idx_ref[idx_row0 + r // 128, r % 128]
        if max_row is not None:
            row = jnp.minimum(row, max_row)
        pltpu.make_async_copy(src_hbm.at[pl.ds(row, 1)], dst_ref.at[pl.ds(r, 1)],
                              sem).start(priority=r % 2)


def _wait_row_gather(src_hbm, dst_ref, sem):
    pltpu.make_async_copy(src_hbm.at[pl.ds(0, dst_ref.shape[0])], dst_ref, sem).wait()


def _weighted_expert_rows(rcol_ref, f_rows):
    return (rcol_ref[:, 1:2] * f_rows[:, 0:D_MODEL] + rcol_ref[:, 2:3] * f_rows[:, D_MODEL:2 * D_MODEL])


INPROJ_DOTS = 4


def _inproj_kernel(*refs, batch, n_ctx_tiles, moe_in):
    if moe_in:
        (dest_ref, xc_ref, xl_ref, fs_hbm, rcol_ref, modp_ref, mod_ref, g_ref, w_ref,
         x2_ref, u_ref, ub_ref, gate_ref, h_scr, ua_scr, f_buf, sem) = refs
    else:
        xc_ref, xl_ref, mod_ref, g_ref, w_ref, u_ref, ub_ref, gate_ref, h_scr, ua_scr = refs
    rows = batch * TILE_TOK
    j = pl.program_id(0)
    last = pl.num_programs(0) - 1
    slot = j % 2
    x = _pick(xc_ref, xl_ref, j, n_ctx_tiles)

    def fetch(part):
        if moe_in:
            _start_row_gather(dest_ref, jnp.minimum(j + 1, last) * (rows // 128), fs_hbm,
                              f_buf.at[1 - slot], sem.at[1 - slot], part, INPROJ_DOTS)

    if moe_in:
        @pl.when(j == 0)
        def _():
            _start_row_gather(dest_ref, 0, fs_hbm, f_buf.at[0], sem.at[0])

        _wait_row_gather(fs_hbm, f_buf.at[slot], sem.at[slot])
        f = _weighted_expert_rows(rcol_ref, f_buf[slot])
        x = x + modp_ref[:, :, 5 * D_MODEL:6 * D_MODEL] * f.reshape(batch, TILE_TOK, D_MODEL)
        x2_ref[...] = x
    y = x * lax.rsqrt(jnp.mean(x * x, axis=-1, keepdims=True) + EPS) * g_ref[...]
    shift = mod_ref[:, :, 0:D_MODEL]
    scale = mod_ref[:, :, D_MODEL:2 * D_MODEL]
    h_scr[...] = (y * (1.0 + scale) + shift).reshape(rows, D_MODEL).astype(BF16)
    h = h_scr[...]
    fetch(0)
    ua = jnp.dot(h, w_ref[:, 0:D_SSM], preferred_element_type=F32)
    for b in range(batch):
        for p in range(2):
            ua_scr[p, b * ROW_PITCH:b * ROW_PITCH + TILE_TOK, :] = (
                ua[b * TILE_TOK:(b + 1) * TILE_TOK, 128 * p:128 * p + 128])
    fetch(1)
    ub = jnp.dot(h, w_ref[:, D_SSM:GATE_OFF], preferred_element_type=F32)
    ub_ref[...] = ub.reshape(batch, TILE_TOK, D_POOL)
    for k in range(2):
        lo = GATE_OFF + k * D_MODEL
        fetch(2 + k)
        g = jnp.dot(h, w_ref[:, lo:lo + D_MODEL], preferred_element_type=F32)
        gate_ref[:, :, k * D_MODEL:(k + 1) * D_MODEL] = (
            jax.nn.sigmoid(g).astype(BF16).reshape(batch, TILE_TOK, D_MODEL))
    _to_chunk_layout(ua_scr, u_ref, batch)
    if moe_in:
        @pl.when(j == last)
        def _():
            _wait_row_gather(fs_hbm, f_buf.at[1 - slot], sem.at[1 - slot])


def _inproj(x_ctx, x_lat, lat_off, mod_sel, norm_g, w_in_bf, layer, batch, n_ctx_tiles, n_tiles,
            moe=None):
    rows = batch * TILE_TOK
    n_tok = n_tiles * TILE_TOK
    chunk_rows = batch * TILE_TOK // S5_CHUNK
    tok = lambda cols: pl.BlockSpec((batch, TILE_TOK, cols), lambda j, *_: (0, j, 0))
    out_shape = [jax.ShapeDtypeStruct((N_SSM_GROUPS, n_tiles * chunk_rows, D_SSM), F32),
                 jax.ShapeDtypeStruct((batch, n_tok, D_POOL), F32),
                 jax.ShapeDtypeStruct((batch, n_tok, 2 * D_MODEL), BF16)]
    out_specs = [pl.BlockSpec((N_SSM_GROUPS, chunk_rows, D_SSM), lambda j, *_: (0, j, 0)),
                 tok(D_POOL), tok(2 * D_MODEL)]
    in_specs = _src_specs(D_MODEL, batch, n_ctx_tiles, 0, lat_off)
    args = [x_ctx, x_lat]
    scratch = [pltpu.VMEM((rows, D_MODEL), BF16), pltpu.VMEM((2, batch * ROW_PITCH, 128), F32)]
    prefetch = []
    if moe is not None:
        dest, fs, rcol = moe
        prefetch = [dest]
        in_specs += [pl.BlockSpec(memory_space=pl.ANY),
                     pl.BlockSpec((rows, ROUTE_LANES), lambda j, *_: (j, 0)),
                     _mod_spec(batch, layer - 1, n_ctx_tiles, 0)]
        args += [fs, rcol, mod_sel]
        out_shape.insert(0, jax.ShapeDtypeStruct((batch, n_tok, D_MODEL), F32))
        out_specs.insert(0, tok(D_MODEL))
        scratch += [pltpu.VMEM((2, rows, 2 * D_MODEL), F32), pltpu.SemaphoreType.DMA((2,))]
    in_specs += [_mod_spec(batch, layer, n_ctx_tiles, 0),
                 pl.BlockSpec((None, 1, D_MODEL), lambda j, *_: (layer, 0, 0)),
                 pl.BlockSpec((None, D_MODEL, D_IN), lambda j, *_: (layer, 0, 0))]
    args += [mod_sel, norm_g, w_in_bf]
    return pl.pallas_call(
        functools.partial(_inproj_kernel, batch=batch, n_ctx_tiles=n_ctx_tiles, moe_in=moe is not None),
        out_shape=tuple(out_shape),
        grid_spec=pltpu.PrefetchScalarGridSpec(
            num_scalar_prefetch=len(prefetch), grid=(n_tiles,), in_specs=in_specs,
            out_specs=tuple(out_specs), scratch_shapes=scratch),
        compiler_params=_cparams("arbitrary"),
        name="inproj",
    )(*prefetch, *args)


def _s5_matrices(lam_re, lam_im, log_dt, b_re, b_im, c_re, c_im):
    L, G, N, H = S5_CHUNK, N_SSM_GROUPS, SSM_STATE, SSM_GROUP
    lr, li = lam_re.astype(F32), lam_im.astype(F32)
    dt = jnp.exp(log_dt.astype(F32))[..., None]
    zr, zi = lr * dt, li * dt
    k = jnp.arange(L + 1, dtype=F32)[:, None, None, None]
    pm = jnp.exp(zr[None] * k)
    pr, pi = pm * jnp.cos(zi[None] * k), pm * jnp.sin(zi[None] * k)
    nr, ni = pr[1] - 1.0, pi[1]
    den = lr * lr + li * li
    fr = (nr * lr + ni * li) / den
    fi = (ni * lr - nr * li) / den
    br, bi = b_re.astype(F32), b_im.astype(F32)
    bbr = fr[..., None] * br - fi[..., None] * bi
    bbi = fr[..., None] * bi + fi[..., None] * br
    cr, ci = c_re.astype(F32), c_im.astype(F32)
    cpr = cr[None] * pr[:, :, :, None, :] - ci[None] * pi[:, :, :, None, :]
    cpi = cr[None] * pi[:, :, :, None, :] + ci[None] * pr[:, :, :, None, :]
    kern = (jnp.einsum('kdgan,dgnh->kdgah', cpr, bbr)
            - jnp.einsum('kdgan,dgnh->kdgah', cpi, bbi))
    s_idx = np.arange(L)[:, None]
    t_idx = np.arange(L)[None, :]
    lag_f = np.clip(t_idx - s_idx, 0, L - 1)
    lag_b = np.clip(s_idx - t_idx, 0, L - 1)
    mf = jnp.where((t_idx >= s_idx)[:, :, None, None, None], kern[:, 0][lag_f], 0.0)
    mb = jnp.where((s_idx >= t_idx)[:, :, None, None, None], kern[:, 1][lag_b], 0.0)
    m = (mf + mb).transpose(2, 0, 4, 1, 3).reshape(G, L * H, L * H)
    powers = (L - 1 - np.arange(L), np.arange(L))
    inj = []
    for d in range(2):
        p_r, p_i = pr[powers[d], d], pi[powers[d], d]
        wr = p_r[..., None] * bbr[d][None] - p_i[..., None] * bbi[d][None]
        wi = p_r[..., None] * bbi[d][None] + p_i[..., None] * bbr[d][None]
        inj.append((wr, wi))
    w_inj = jnp.concatenate([inj[0][0], inj[1][0], inj[0][1], inj[1][1]], axis=2)
    mcat = jnp.concatenate([m, w_inj.transpose(1, 0, 3, 2).reshape(G, L * H, 4 * N)], axis=2)
    rd = (np.arange(L) + 1, L - np.arange(L))
    w_rd = jnp.concatenate([cpr[rd[0], 0], cpr[rd[1], 1], -cpi[rd[0], 0], -cpi[rd[1], 1]],
                           axis=3)
    wy = w_rd.transpose(1, 3, 0, 2).reshape(G, 4 * N, L * H)
    a_r = jnp.concatenate([pr[L][0], pr[L][1]], axis=-1)
    a_i = jnp.concatenate([pi[L][0], pi[L][1]], axis=-1)
    coef = jnp.stack([a_r, a_i] + [jnp.zeros_like(a_r)] * 6, axis=1)
    return mcat.astype(BF16), wy.astype(BF16), coef


def _s5_kernel(u_ref, mcat_ref, wy_ref, coef_ref, dskip_ref, y_ref, s_scr, h_scr, *, batch,
               n_chunks, n_ctx_chunks):
    n = SSM_STATE
    rb = S5_CHUNK * batch
    n_rb = n_chunks // S5_CHUNK

    def inject(i, c):
        r0 = pl.multiple_of(i * rb, rb)
        u = u_ref[pl.ds(r0, rb), :]
        ub = u.astype(BF16)
        y_ref[pl.ds(r0, rb), :] = (jnp.dot(ub, mcat_ref[:, 0:D_SSM], preferred_element_type=F32)
                                   + dskip_ref[...] * u)
        s_scr[pl.ds(r0, rb), :] = jnp.dot(ub, mcat_ref[:, D_SSM:2 * D_SSM],
                                          preferred_element_type=F32)
        return c

    lax.fori_loop(0, n_rb, inject, 0)
    a_r, a_i = coef_ref[0:1, :], coef_ref[1:2, :]
    fwd_lanes = lax.broadcasted_iota(jnp.int32, (batch, 2 * n), 1) < n

    def step(k, carry):
        h_re, h_im = carry
        cb = jnp.where(k < n_ctx_chunks, n_ctx_chunks - 1 - k, n_chunks - 1 + n_ctx_chunks - k)
        rf = pl.multiple_of(k * batch, batch)
        rk = pl.multiple_of(cb * batch, batch)
        h_scr[pl.ds(rf, batch), 0:n] = h_re[:, 0:n]
        h_scr[pl.ds(rk, batch), n:2 * n] = h_re[:, n:2 * n]
        h_scr[pl.ds(rf, batch), 2 * n:3 * n] = h_im[:, 0:n]
        h_scr[pl.ds(rk, batch), 3 * n:4 * n] = h_im[:, n:2 * n]
        s_re = jnp.where(fwd_lanes, s_scr[pl.ds(rf, batch), 0:2 * n], s_scr[pl.ds(rk, batch), 0:2 * n])
        s_im = jnp.where(fwd_lanes, s_scr[pl.ds(rf, batch), 2 * n:4 * n],
                         s_scr[pl.ds(rk, batch), 2 * n:4 * n])
        return (a_r * h_re - a_i * h_im + s_re, a_r * h_im + a_i * h_re + s_im)

    zero = jnp.zeros((batch, 2 * n), F32)
    lax.fori_loop(0, n_chunks, step, (zero, zero), unroll=4)

    def readout(i, c):
        r0 = pl.multiple_of(i * rb, rb)
        y_ref[pl.ds(r0, rb), :] += jnp.dot(h_scr[pl.ds(r0, rb), :].astype(BF16), wy_ref[...],
                                           preferred_element_type=F32)
        return c

    lax.fori_loop(0, n_rb, readout, 0)


def _s5(u_t, mcat, wy, coef, dskip, batch, n_chunks, n_ctx_chunks):
    G = N_SSM_GROUPS
    rows = u_t.shape[1]
    kern = functools.partial(_s5_kernel, batch=batch, n_chunks=n_chunks, n_ctx_chunks=n_ctx_chunks)
    per_group = lambda *blk: pl.BlockSpec((None,) + blk, lambda g: (g, 0, 0))
    return pl.pallas_call(
        kern,
        out_shape=jax.ShapeDtypeStruct((G, rows, D_SSM), F32),
        grid=(G,),
        in_specs=[per_group(rows, D_SSM), per_group(D_SSM, 2 * D_SSM), per_group(D_SSM, D_SSM),
                  per_group(8, 2 * SSM_STATE), per_group(1, D_SSM)],
        out_specs=per_group(rows, D_SSM),
        scratch_shapes=[pltpu.VMEM((rows, D_SSM), F32), pltpu.VMEM((rows, D_SSM), F32)],
        compiler_params=_cparams("arbitrary"),
        name="s5",
    )(u_t, mcat, wy, coef, dskip)


def _window(w):
    return -(w // 2), w - 1 - w // 2


def _pool_constants(rows, width):
    rpb = POOL_BLOCK // width
    pm = np.zeros((4, POOL_BLOCK, POOL_BLOCK), np.float32)
    inv = np.zeros((rows * width, 4), np.float32)
    col = np.arange(width)
    row = np.arange(rows)
    for i, w in enumerate(POOL_WINDOWS):
        lo, hi = _window(w)
        c0, c1 = np.clip(col + lo, 0, width - 1), np.clip(col + hi, 0, width - 1)
        r0, r1 = np.clip(row + lo, 0, rows - 1), np.clip(row + hi, 0, rows - 1)
        band = ((col[None, :] >= c0[:, None]) & (col[None, :] <= c1[:, None])).astype(np.float32)
        for r in range(rpb):
            pm[i, r * width:(r + 1) * width, r * width:(r + 1) * width] = band
        cnt = (r1 - r0 + 1)[:, None] * (c1 - c0 + 1)[None, :]
        inv[:, i] = (1.0 / cnt).reshape(-1)
    return jnp.asarray(pm, BF16), jnp.asarray(inv, F32)


def _pool_segment(u_ref, o_ref, cs_scr, pm_ref, inv_ref, wp_ref, ps_ref, tok0, rows, width):
    n_tok = rows * width
    n_blk = n_tok // POOL_BLOCK
    pad = 8 * width if rows > 1 else 0
    if rows > 1:
        zeros = jnp.zeros((pad, 2 * 128), F32)
        for i in range(4):
            cs_scr[i, 0:pad, :] = zeros
            cs_scr[i, pad + n_tok:pad + n_tok + pad, :] = zeros
    for b in range(n_blk):
        t0 = b * POOL_BLOCK
        for i in range(4):
            seg = POOL_SEG[i]
            xb = u_ref[tok0 + t0:tok0 + t0 + POOL_BLOCK, seg:seg + 256].astype(BF16)
            cs_scr[i, pad + t0:pad + t0 + POOL_BLOCK, :] = jnp.dot(
                pm_ref[i], xb, preferred_element_type=F32)
    lane = lax.broadcasted_iota(jnp.int32, (POOL_BLOCK, 128), 1)
    low_half = lane < 64
    for b in range(n_blk):
        t0 = b * POOL_BLOCK
        win = []
        for i, w in enumerate(POOL_WINDOWS):
            lo, hi = _window(w) if rows > 1 else (0, 0)
            acc = None
            for k in range(lo, hi + 1):
                base = pad + t0 + k * width
                piece = cs_scr[i, base:base + POOL_BLOCK, :]
                acc = piece if acc is None else acc + piece
            win.append(acc * inv_ref[t0:t0 + POOL_BLOCK, i:i + 1])
        x = u_ref[tok0 + t0:tok0 + t0 + POOL_BLOCK, :]
        pooled = jnp.concatenate([
            win[0][:, 0:128],
            jnp.where(low_half, win[0][:, 128:256], win[1][:, 0:128]),
            win[1][:, 128:256],
            win[2][:, 0:128],
            jnp.where(low_half, win[2][:, 128:256], win[3][:, 0:128]),
            win[3][:, 128:256]], axis=1)
        d = (pooled - x).astype(BF16)
        y0 = jnp.dot(d[:, 0:384], wp_ref[0], preferred_element_type=F32)
        y1 = jnp.dot(d[:, 384:768], wp_ref[1], preferred_element_type=F32)
        y = jnp.concatenate([y0, y1], axis=1) * ps_ref[...]
        o_ref[tok0 + t0:tok0 + t0 + POOL_BLOCK, :] = y.astype(BF16)


def _pool_kernel(u_ref, pmc_ref, invc_ref, pml_ref, invl_ref, wp_ref, ps_ref, o_ref, cs_scr, *,
                 lat_rows, with_ctx):
    if with_ctx:
        _pool_segment(u_ref, o_ref, cs_scr, pmc_ref, invc_ref, wp_ref, ps_ref, 0, 1, CTX_LEN)
    else:
        o_ref[0:CTX_LEN, :] = jnp.zeros((CTX_LEN, D_POOL), BF16)
    _pool_segment(u_ref, o_ref, cs_scr, pml_ref, invl_ref, wp_ref, ps_ref, CTX_LEN, lat_rows, GRID_W)


def _pool(ub, wp2, pscale, layer, with_ctx):
    batch, n_tok, _ = ub.shape
    lat_rows = (n_tok - CTX_LEN) // GRID_W
    pmc, invc = _pool_constants(1, CTX_LEN)
    pml, invl = _pool_constants(lat_rows, GRID_W)
    const = lambda *blk: pl.BlockSpec(blk, lambda b: (0,) * len(blk))
    return pl.pallas_call(
        functools.partial(_pool_kernel, lat_rows=lat_rows, with_ctx=with_ctx),
        out_shape=jax.ShapeDtypeStruct((batch, n_tok, D_POOL), BF16),
        grid=(batch,),
        in_specs=[
            pl.BlockSpec((None, n_tok, D_POOL), lambda b: (b, 0, 0)),
            const(4, POOL_BLOCK, POOL_BLOCK), const(CTX_LEN, 4),
            const(4, POOL_BLOCK, POOL_BLOCK), const(lat_rows * GRID_W, 4),
            pl.BlockSpec((None, 2, 384, 384), lambda b: (layer, 0, 0, 0)),
            pl.BlockSpec((None, 1, D_POOL), lambda b: (layer, 0, 0)),
        ],
        out_specs=pl.BlockSpec((None, n_tok, D_POOL), lambda b: (b, 0, 0)),
        scratch_shapes=[pltpu.VMEM((4, lat_rows * GRID_W + 16 * GRID_W, 256), F32)],
        compiler_params=_cparams("arbitrary"),
        name="pool",
    )(ub, pmc, invc, pml, invl, wp2, pscale)


def _first_argmax(rows_):
    best, idx = rows_[0], jnp.zeros_like(rows_[0], dtype=jnp.int32)
    for k in range(1, len(rows_)):
        take = rows_[k] > best
        idx = jnp.where(take, k, idx)
        best = jnp.where(take, rows_[k], best)
    return best, idx


def _route(lt):
    g_rows = [lt[k:k + 1, :] for k in range(N_EXPERT_GROUPS)]
    best, grp = _first_argmax(g_rows)
    denom = sum(jnp.exp(r - best) for r in g_rows)
    p_grp = 1.0 / denom
    inner = []
    for e in range(EXPERTS_PER_GROUP):
        acc = jnp.zeros_like(best)
        for g in range(N_EXPERT_GROUPS):
            r = 4 + 4 * g + e
            acc = jnp.where(grp == g, lt[r:r + 1, :], acc)
        inner.append(acc)
    v1, i1 = _first_argmax(inner)
    masked = [jnp.where(i1 == e, -jnp.inf, inner[e]) for e in range(EXPERTS_PER_GROUP)]
    v2, i2 = _first_argmax(masked)
    e21 = jnp.exp(v2 - v1)
    w1 = p_grp / (1.0 + e21)
    w2 = p_grp * e21 / (1.0 + e21)
    first_low = i1 < i2
    lo = jnp.where(first_low, i1, i2)
    hi = jnp.where(first_low, i2, i1)
    w_lo = jnp.where(first_low, w1, w2)
    w_hi = jnp.where(first_low, w2, w1)
    off = jnp.where(lo == 0, 0, jnp.where(lo == 1, 3, 5))
    cls = N_PAIRS * grp + off + hi - lo - 1
    return cls.astype(F32), w_lo, w_hi


def _merge_kernel(xc_ref, xl_ref, y_ref, yb_ref, gate_ref, mod_ref, g2_ref, wglu_ref, bglu_ref,
                  wbra_ref, wbrb_ref, wout_ref, wrh_ref, wrl_ref, br_ref, x1_ref, h2_ref, route_ref,
                  rcol_ref, ya_scr, *, batch, n_ctx_tiles, tile_off):
    D = D_MODEL
    rows = batch * TILE_TOK
    tile = pl.program_id(0) + tile_off
    _from_chunk_layout(y_ref, ya_scr, batch)
    y = jnp.concatenate(
        [jnp.concatenate([ya_scr[p, b * ROW_PITCH:b * ROW_PITCH + TILE_TOK, :] for p in range(2)],
                         axis=1) for b in range(batch)], axis=0)
    z = jax.nn.gelu(y)
    glu = jax.nn.sigmoid(jnp.dot(z.astype(BF16), wglu_ref[...], preferred_element_type=F32)
                         + bglu_ref[...])
    ya = (z * glu).astype(BF16)
    gates = gate_ref[...].reshape(rows, 2 * D)
    m = gates[:, 0:D].astype(F32) * jnp.dot(ya, wbra_ref[...], preferred_element_type=F32)
    m += gates[:, D:2 * D].astype(F32) * jnp.dot(yb_ref[...].reshape(rows, D_POOL), wbrb_ref[...],
                                                 preferred_element_type=F32)
    out = jnp.dot(m.astype(BF16), wout_ref[...], preferred_element_type=F32)
    x_in = _pick(xc_ref, xl_ref, tile, n_ctx_tiles)
    x1 = x_in + mod_ref[:, :, 2 * D:3 * D] * out.reshape(batch, TILE_TOK, D)
    x1_ref[...] = x1
    yn = x1 * lax.rsqrt(jnp.mean(x1 * x1, axis=-1, keepdims=True) + EPS) * g2_ref[...]
    h2 = yn * (1.0 + mod_ref[:, :, 4 * D:5 * D]) + mod_ref[:, :, 3 * D:4 * D]
    h2_ref[...] = h2.reshape(rows, D)
    h_hi, h_lo = _split_bf16(h2.reshape(rows, D))
    logits = jnp.dot(h_hi, wrh_ref[...], preferred_element_type=F32)
    logits += jnp.dot(h_lo, wrh_ref[...], preferred_element_type=F32)
    logits += jnp.dot(h_hi, wrl_ref[...], preferred_element_type=F32)
    logits += br_ref[...]
    cls, w_lo, w_hi = _route(logits.T)
    zero = jnp.zeros_like(cls)
    route_ref[...] = jnp.concatenate([cls, w_lo, w_hi, zero, zero, zero, zero, zero], axis=0)
    cols = jnp.concatenate([cls, w_lo, w_hi, jnp.zeros((ROUTE_LANES - 3, rows), F32)], axis=0)
    rcol_ref[...] = cols.T


def _merge(x_ctx, x_lat, lat_off, y_t, yb, gates, mod_sel, norm2_g, p, layer, batch, tile_off,
           n_tiles, n_ctx_tiles):
    rows = batch * TILE_TOK
    chunk_rows = rows // S5_CHUNK
    tok = lambda cols: pl.BlockSpec((batch, TILE_TOK, cols), lambda j: (0, j + tile_off, 0))
    lay3 = lambda *blk: pl.BlockSpec((None,) + blk, lambda j: (layer, 0, 0))
    n_tok = n_tiles * TILE_TOK
    return pl.pallas_call(
        functools.partial(_merge_kernel, batch=batch, n_ctx_tiles=n_ctx_tiles, tile_off=tile_off),
        out_shape=(jax.ShapeDtypeStruct((batch, n_tok, D_MODEL), F32),
                   jax.ShapeDtypeStruct((n_tiles * rows, D_MODEL), F32),
                   jax.ShapeDtypeStruct((8, n_tiles * rows), F32),
                   jax.ShapeDtypeStruct((n_tiles * rows, ROUTE_LANES), F32)),
        grid=(n_tiles,),
        in_specs=_src_specs(D_MODEL, batch, n_ctx_tiles, tile_off, lat_off) + [
            pl.BlockSpec((N_SSM_GROUPS, chunk_rows, D_SSM), lambda j: (0, j + tile_off, 0)),
            tok(D_POOL), tok(2 * D_MODEL),
            _mod_spec(batch, layer, n_ctx_tiles, tile_off),
            lay3(1, D_MODEL), lay3(D_SSM, D_SSM), lay3(1, D_SSM), lay3(D_SSM, D_MODEL),
            lay3(D_POOL, D_MODEL), lay3(D_MODEL, D_MODEL), lay3(D_MODEL, ROUTE_LANES),
            lay3(D_MODEL, ROUTE_LANES), lay3(1, ROUTE_LANES),
        ],
        out_specs=(pl.BlockSpec((batch, TILE_TOK, D_MODEL), lambda j: (0, j, 0)),
                   pl.BlockSpec((rows, D_MODEL), lambda j: (j, 0)),
                   pl.BlockSpec((8, rows), lambda j: (0, j)),
                   pl.BlockSpec((rows, ROUTE_LANES), lambda j: (j, 0))),
        scratch_shapes=[pltpu.VMEM((2, batch * ROW_PITCH, 128), F32)],
        compiler_params=_cparams("arbitrary"),
        name="merge",
    )(x_ctx, x_lat, y_t, yb, gates, mod_sel, norm2_g, p['w_glu'], p['b_glu'], p['w_br_a'],
      p['w_br_b'], p['w_out'], p['wr_hi'], p['wr_lo'], p['b_r'])


DISPATCH_GROUP = 128
ZERO_ROWS = EXPERT_TILE + 8


def _plan_kernel(cls_ref, h2_hbm, dest_ref, meta_ref, xs_hbm, dest_smem, ends_scr, ends_smem,
                 zeros_scr, sem, *, n_q, n_rows):
    rows_q = n_q // 128
    cls = cls_ref[...]
    r_i = lax.broadcasted_iota(jnp.int32, (128, 128), 0)
    c_i = lax.broadcasted_iota(jnp.int32, (128, 128), 1)
    upper = jnp.where(r_i <= c_i, 1.0, 0.0).astype(BF16)
    r_q = lax.broadcasted_iota(jnp.int32, (rows_q, rows_q), 0)
    c_q = lax.broadcasted_iota(jnp.int32, (rows_q, rows_q), 1)
    lower = jnp.where(c_q < r_q, 1.0, 0.0).astype(BF16)
    tile_start = (lax.broadcasted_iota(jnp.int32, (1, META_LANES), 1) * EXPERT_TILE).astype(F32)
    lane128 = lax.broadcasted_iota(jnp.int32, (1, 128), 1)
    start = jnp.zeros((1, 128), F32)
    dest = jnp.zeros((rows_q, 128), F32)
    tile_cls = jnp.zeros((1, META_LANES), F32)
    valid_end = jnp.zeros((1, 128), F32)
    for c in range(N_CLASSES):
        m = cls == float(c)
        incl = jnp.dot(jnp.where(m, 1.0, 0.0).astype(BF16), upper, preferred_element_type=F32)
        row_tot = jnp.broadcast_to(incl[:, 127:128], (rows_q, 128))
        before = jnp.dot(lower, row_tot.astype(BF16), preferred_element_type=F32)
        dest = jnp.where(m, start + before + incl - 1.0, dest)
        total = before[rows_q - 1:rows_q, :] + row_tot[rows_q - 1:rows_q, :]
        valid_end = jnp.where(lane128 == c, start + total, valid_end)
        start = start + jnp.floor((total + (EXPERT_TILE - 1.0)) * (1.0 / EXPERT_TILE)) * EXPERT_TILE
        end2 = jnp.concatenate([start, start], axis=1)
        tile_cls = tile_cls + jnp.where(tile_start >= end2, 1.0, 0.0)
    active = jnp.where(tile_start < end2, 1.0, 0.0)
    tc = jnp.minimum(tile_cls, N_CLASSES - 1.0)
    grp = sum(jnp.where(tc >= float(N_PAIRS * k), 1.0, 0.0) for k in range(1, N_EXPERT_GROUPS))
    pair = tc - N_PAIRS * grp
    p_lo = jnp.where(pair >= 3.0, 1.0, 0.0) + jnp.where(pair >= 5.0, 1.0, 0.0)
    p_hi = jnp.where(pair == 0.0, 1.0, jnp.where(pair == 1.0, 2.0, jnp.where(pair == 3.0, 2.0, 3.0)))
    lane = lax.broadcasted_iota(jnp.int32, (1, META_LANES), 1)
    fresh = jnp.where((lane == 0) | (tc != pltpu.roll(tc, 1, 1)), 1.0, 0.0)
    zero = jnp.zeros_like(tc)
    meta_ref[...] = jnp.concatenate(
        [EXPERTS_PER_GROUP * grp + p_lo, EXPERTS_PER_GROUP * grp + p_hi, active, fresh,
         zero, zero, zero, zero], axis=0).astype(jnp.int32)
    dest_ref[...] = dest.astype(jnp.int32)
    ends_scr[...] = jnp.broadcast_to(valid_end, (8, 128)).astype(jnp.int32)
    zeros_scr[...] = jnp.zeros(zeros_scr.shape, F32)

    stage = pltpu.make_async_copy(dest_ref, dest_smem, sem.at[0])
    stage_ends = pltpu.make_async_copy(ends_scr, ends_smem, sem.at[1])
    stage.start()
    stage_ends.start()
    stage.wait()
    stage_ends.wait()

    def tail_fill(t):
        return pltpu.make_async_copy(zeros_scr.at[pl.ds(0, EXPERT_TILE)],
                                     xs_hbm.at[pl.ds(n_q + t * EXPERT_TILE, EXPERT_TILE)], sem.at[2])

    def pad_fill(c):
        first = jnp.minimum((ends_smem[0, c] >> 3) << 3, n_rows - ZERO_ROWS)
        return pltpu.make_async_copy(zeros_scr, xs_hbm.at[pl.ds(pl.multiple_of(first, 8), ZERO_ROWS)],
                                     sem.at[2])

    tail = [tail_fill(t) for t in range((n_rows - n_q) // EXPERT_TILE)]
    for f in tail:
        f.start()
    for f in tail:
        f.wait()
    for c in range(N_CLASSES):
        f = pad_fill(c)
        f.start()
        f.wait()

    def group_wait(turn):
        pltpu.make_async_copy(h2_hbm.at[pl.ds(0, DISPATCH_GROUP)],
                              xs_hbm.at[pl.ds(0, DISPATCH_GROUP)], sem.at[turn]).wait()

    def dispatch(i, carry):
        turn = i % 2
        for k in range(DISPATCH_GROUP):
            pltpu.make_async_copy(h2_hbm.at[pl.ds(i * DISPATCH_GROUP + k, 1)],
                                  xs_hbm.at[pl.ds(dest_smem[i, k], 1)],
                                  sem.at[turn]).start(priority=k % 2)

        @pl.when(i > 0)
        def _():
            group_wait(1 - turn)

        return carry

    lax.fori_loop(0, rows_q, dispatch, 0)
    group_wait((rows_q - 1) % 2)


def _plan(cls_q, h2):
    n_q = cls_q.shape[0]
    rows_q = n_q // 128
    n_tiles = n_q // EXPERT_TILE + N_CLASSES
    assert n_tiles <= META_LANES and DISPATCH_GROUP == 128
    n_rows = n_tiles * EXPERT_TILE
    i32 = jnp.int32
    return pl.pallas_call(
        functools.partial(_plan_kernel, n_q=n_q, n_rows=n_rows),
        out_shape=(jax.ShapeDtypeStruct((rows_q, 128), i32), jax.ShapeDtypeStruct((8, META_LANES), i32),
                   jax.ShapeDtypeStruct((n_rows, D_MODEL), F32)),
        in_specs=[pl.BlockSpec(memory_space=pltpu.VMEM), pl.BlockSpec(memory_space=pl.ANY)],
        out_specs=(pl.BlockSpec(memory_space=pltpu.VMEM), pl.BlockSpec(memory_space=pltpu.VMEM),
                   pl.BlockSpec(memory_space=pl.ANY)),
        scratch_shapes=[pltpu.SMEM((rows_q, 128), i32), pltpu.VMEM((8, 128), i32),
                        pltpu.SMEM((8, 128), i32), pltpu.VMEM((ZERO_ROWS, D_MODEL), F32),
                        pltpu.SemaphoreType.DMA((3,))],
        compiler_params=pltpu.CompilerParams(vmem_limit_bytes=V7X_VMEM_LIMIT),
        name="plan",
    )(cls_q.reshape(rows_q, 128), h2)


def _expert_kernel(elo_ref, ehi_ref, act_ref, fresh_ref, x_ref, w1a_ref, w1b_ref, w3a_ref, w3b_ref,
                   w2a_ref, w2b_ref, o_ref, w1a_bf, w1b_bf, w3a_bf, w3b_bf, w2a_bf, w2b_bf):
    j = pl.program_id(0)

    @pl.when(fresh_ref[j] == 1)
    def _():
        for src_w, dst_w in ((w1a_ref, w1a_bf), (w1b_ref, w1b_bf), (w3a_ref, w3a_bf),
                             (w3b_ref, w3b_bf), (w2a_ref, w2a_bf), (w2b_ref, w2b_bf)):
            dst_w[...] = src_w[...].astype(BF16)

    @pl.when(act_ref[j] == 1)
    def _():
        x = x_ref[...].astype(BF16)
        for k, (w1, w3, w2) in enumerate(((w1a_bf, w3a_bf, w2a_bf), (w1b_bf, w3b_bf, w2b_bf))):
            a = jnp.dot(x, w1[...], preferred_element_type=F32)
            b = jnp.dot(x, w3[...], preferred_element_type=F32)
            h = (a * jax.nn.sigmoid(a) * b).astype(BF16)
            o_ref[:, k * D_MODEL:(k + 1) * D_MODEL] = jnp.dot(h, w2[...], preferred_element_type=F32)

    @pl.when(act_ref[j] == 0)
    def _():
        o_ref[...] = jnp.zeros_like(o_ref)


def _experts(xs, meta, w1, w3, w2, layer):
    n_rows = xs.shape[0]
    te = EXPERT_TILE
    n_tiles = n_rows // te
    up = lambda sel: pl.BlockSpec((None, None, D_MODEL, D_EXPERT),
                                  lambda j, lo, hi, act, fr: (layer, (lo, hi)[sel][j], 0, 0))
    down = lambda sel: pl.BlockSpec((None, None, D_EXPERT, D_MODEL),
                                    lambda j, lo, hi, act, fr: (layer, (lo, hi)[sel][j], 0, 0))
    return pl.pallas_call(
        _expert_kernel,
        out_shape=jax.ShapeDtypeStruct((n_rows, 2 * D_MODEL), F32),
        grid_spec=pltpu.PrefetchScalarGridSpec(
            num_scalar_prefetch=4,
            grid=(n_tiles,),
            in_specs=[pl.BlockSpec((te, D_MODEL), lambda j, lo, hi, act, fr: (j, 0)),
                      up(0), up(1), up(0), up(1), down(0), down(1)],
            out_specs=pl.BlockSpec((te, 2 * D_MODEL), lambda j, lo, hi, act, fr: (j, 0)),
            scratch_shapes=[pltpu.VMEM((D_MODEL, D_EXPERT), BF16)] * 4
            + [pltpu.VMEM((D_EXPERT, D_MODEL), BF16)] * 2,
        ),
        compiler_params=_cparams("arbitrary"),
        name="experts",
    )(meta[0, :n_tiles], meta[1, :n_tiles], meta[2, :n_tiles], meta[3, :n_tiles],
      xs, w1, w1, w3, w3, w2, w2)


def _combine_kernel(dest_ref, x_ref, fs_hbm, rcol_ref, mod_ref, g_ref, o_ref, f_buf, sem, *, batch,
                    final):
    j = pl.program_id(0)
    last = pl.num_programs(0) - 1
    slot = j % 2
    rows = batch * TILE_TOK
    idx_rows = rows // 128

    @pl.when(j == 0)
    def _():
        _start_row_gather(dest_ref, 0, fs_hbm, f_buf.at[0], sem.at[0])

    _wait_row_gather(fs_hbm, f_buf.at[slot], sem.at[slot])
    _start_row_gather(dest_ref, jnp.minimum(j + 1, last) * idx_rows, fs_hbm, f_buf.at[1 - slot],
                      sem.at[1 - slot])
    f = _weighted_expert_rows(rcol_ref, f_buf[slot])
    x = x_ref[...] + mod_ref[:, :, 5 * D_MODEL:6 * D_MODEL] * f.reshape(batch, TILE_TOK, D_MODEL)
    if final:
        x = x * lax.rsqrt(jnp.mean(x * x, axis=-1, keepdims=True) + EPS) * g_ref[...]
    o_ref[...] = x

    @pl.when(j == last)
    def _():
        _wait_row_gather(fs_hbm, f_buf.at[1 - slot], sem.at[1 - slot])


def _combine(dest, x1, fs, rcol, mod_sel, final_g, layer, tile_off, n_ctx_tiles, final):
    batch, n_tok, _ = x1.shape
    rows = batch * TILE_TOK
    tok = pl.BlockSpec((batch, TILE_TOK, D_MODEL), lambda j, d: (0, j, 0))
    mod_spec = _mod_spec(batch, layer, n_ctx_tiles, tile_off)
    return pl.pallas_call(
        functools.partial(_combine_kernel, batch=batch, final=final),
        out_shape=jax.ShapeDtypeStruct((batch, n_tok, D_MODEL), F32),
        grid_spec=pltpu.PrefetchScalarGridSpec(
            num_scalar_prefetch=1,
            grid=(n_tok // TILE_TOK,),
            in_specs=[tok, pl.BlockSpec(memory_space=pl.ANY),
                      pl.BlockSpec((rows, ROUTE_LANES), lambda j, d: (j, 0)),
                      pl.BlockSpec(mod_spec.block_shape, lambda j, d: mod_spec.index_map(j)),
                      pl.BlockSpec((1, D_MODEL), lambda j, d: (0, 0))],
            out_specs=tok,
            scratch_shapes=[pltpu.VMEM((2, rows, 2 * D_MODEL), F32), pltpu.SemaphoreType.DMA((2,))],
        ),
        compiler_params=_cparams("arbitrary"),
        name="combine",
    )(dest, x1, fs, rcol, mod_sel, final_g)


def kernel(x, c, ctx, c_ctx, norm1_g, norm2_g, w_mod, b_mod, w_in, lam_re, lam_im, log_dt, b_re,
           b_im, c_re, c_im, d_skip, w_glu, b_glu, w_pool, pool_scale, w_br_a, w_br_b, w_out, w_r1,
           b_r1, w_r2, b_r2, w1, w3, w2, final_g):
    batch, seq, d = x.shape
    assert d == D_MODEL and seq % (GRID_W * 8) == 0 and ctx.shape[1] == CTX_LEN and batch % 8 == 0
    n_ctx_tiles, n_lat_tiles = CTX_LEN // TILE_TOK, seq // TILE_TOK
    n_all_tiles = n_ctx_tiles + n_lat_tiles
    n_chunks = (CTX_LEN + seq) // S5_CHUNK
    mod_rows = ((batch + 1 + 7) // 8) * 8

    cc = jnp.concatenate([c, c_ctx[None, :], jnp.zeros((mod_rows - batch - 1, d), F32)], axis=0)
    mod = _modulation(cc, w_mod, b_mod)
    mod_sel = jnp.stack([jnp.broadcast_to(mod[:, batch:batch + 1], (DEPTH, batch, 6 * d)),
                         mod[:, :batch]], axis=1)[:, :, :, None, :]

    w_in_bf = w_in.astype(BF16)
    w_r = jnp.concatenate([w_r1, w_r2.transpose(0, 2, 1, 3).reshape(DEPTH, d, N_EXPERTS),
                           jnp.zeros((DEPTH, d, ROUTE_LANES - 4 - N_EXPERTS), F32)], axis=2)
    wr_hi = w_r.astype(BF16)
    wr_lo = (w_r - wr_hi.astype(F32)).astype(BF16)
    b_r = jnp.concatenate([b_r1, b_r2.reshape(DEPTH, N_EXPERTS),
                           jnp.zeros((DEPTH, ROUTE_LANES - 4 - N_EXPERTS), F32)], axis=1)
    zeros_p = jnp.zeros((DEPTH, POOL_GROUP, POOL_GROUP), F32)
    wp2 = jnp.stack([
        jnp.concatenate([jnp.concatenate([w_pool[:, 0], zeros_p], axis=2),
                         jnp.concatenate([zeros_p, w_pool[:, 1]], axis=2)], axis=1),
        jnp.concatenate([jnp.concatenate([w_pool[:, 2], zeros_p], axis=2),
                         jnp.concatenate([zeros_p, w_pool[:, 3]], axis=2)], axis=1)],
        axis=1).astype(BF16)
    params = {
        'w_glu': w_glu.astype(BF16), 'b_glu': b_glu.reshape(DEPTH, 1, D_SSM),
        'w_br_a': w_br_a.astype(BF16), 'w_br_b': w_br_b.astype(BF16), 'w_out': w_out.astype(BF16),
        'wr_hi': wr_hi, 'wr_lo': wr_lo, 'b_r': b_r.reshape(DEPTH, 1, ROUTE_LANES),
    }
    norm1 = norm1_g.reshape(DEPTH, 1, d)
    norm2 = norm2_g.reshape(DEPTH, 1, d)
    pscale = pool_scale.reshape(DEPTH, 1, D_POOL)
    dskip = jnp.tile(d_skip.reshape(DEPTH, N_SSM_GROUPS, 1, SSM_GROUP), (1, 1, 1, S5_CHUNK))

    s5_mats = jax.vmap(_s5_matrices)(lam_re, lam_im, log_dt, b_re, b_im, c_re, c_im)
    x_ctx, x_lat, lat_off = ctx, x, 0
    moe = None
    for l in range(DEPTH):
        last = l == DEPTH - 1
        outs = _inproj(x_ctx, x_lat, lat_off, mod_sel, norm1, w_in_bf, l, batch, n_ctx_tiles,
                       n_all_tiles, moe)
        if moe is not None:
            x_ctx = x_lat = outs[0]
            outs = outs[1:]
        u_t, ub, gates = outs
        y_t = _s5(u_t, s5_mats[0][l], s5_mats[1][l], s5_mats[2][l], dskip[l], batch, n_chunks,
                  CTX_LEN // S5_CHUNK)
        yb = _pool(ub, wp2, pscale, l, not last)

        tile_off = n_ctx_tiles if last else 0
        n_tiles = n_lat_tiles if last else n_all_tiles
        x1, h2, route, rcol = _merge(x_ctx, x_lat, lat_off, y_t, yb, gates, mod_sel, norm2, params, l,
                                     batch, tile_off, n_tiles, n_ctx_tiles)
        dest, meta, xs = _plan(route[0], h2)
        fs = _experts(xs, meta, w1, w3, w2, l)
        if last:
            return _combine(dest, x1, fs, rcol, mod_sel, final_g.reshape(1, d), l, tile_off,
                            n_ctx_tiles, True)
        x_ctx, x_lat, lat_off, moe = x1, x1, n_ctx_tiles, (dest, fs, rcol)
```

```python
import functools

import numpy as np
import jax
import jax.numpy as jnp
from jax import lax
from jax.experimental import pallas as pl
from jax.experimental.pallas import tpu as pltpu

F32 = jnp.float32
BF16 = jnp.bfloat16

D_MODEL = 1024
DEPTH = 2
GRID_W = 64
CTX_LEN = 256
EPS = 1e-6

D_SSM = 256
SSM_GROUP = 16
N_SSM_GROUPS = 16
SSM_STATE = 64
S5_CHUNK = 16

D_POOL = 768
POOL_WINDOWS = (2, 4, 8, 16)
POOL_GROUP = 192
POOL_SEG = (0, 128, 384, 512)
GATE_OFF = D_SSM + D_POOL
D_IN = D_SSM + D_POOL + 2 * D_MODEL

N_EXPERT_GROUPS = 4
EXPERTS_PER_GROUP = 4
N_EXPERTS = 16
D_EXPERT = 512
N_PAIRS = 6
N_CLASSES = N_EXPERT_GROUPS * N_PAIRS
ROUTE_LANES = 128

TILE_TOK = 32
ROW_PITCH = TILE_TOK + 4
POOL_BLOCK = 256
EXPERT_TILE = 256
META_LANES = 256
V7X_VMEM_LIMIT = 56 * 1024 * 1024


def _cparams(*sem):
    return pltpu.CompilerParams(dimension_semantics=sem, vmem_limit_bytes=V7X_VMEM_LIMIT)


def _split_bf16(v):
    hi = v.astype(BF16)
    lo = (v - hi.astype(F32)).astype(BF16)
    return hi, lo


def _mod_kernel(c_ref, w_ref, b_ref, o_ref):
    c = c_ref[...]
    a = c * jax.nn.sigmoid(c)
    a_hi, a_lo = _split_bf16(a)
    w_hi, w_lo = _split_bf16(w_ref[...])
    acc = jnp.dot(a_hi, w_hi, preferred_element_type=F32)
    acc += jnp.dot(a_lo, w_hi, preferred_element_type=F32)
    acc += jnp.dot(a_hi, w_lo, preferred_element_type=F32)
    o_ref[...] = acc + b_ref[...]


def _modulation(cc, w_mod, b_mod):
    rows = cc.shape[0]
    nblk = 4
    cols = 6 * D_MODEL // nblk
    return pl.pallas_call(
        _mod_kernel,
        out_shape=jax.ShapeDtypeStruct((DEPTH, rows, 6 * D_MODEL), F32),
        grid=(DEPTH, nblk),
        in_specs=[
            pl.BlockSpec((rows, D_MODEL), lambda l, j: (0, 0)),
            pl.BlockSpec((None, D_MODEL, cols), lambda l, j: (l, 0, j)),
            pl.BlockSpec((None, 1, cols), lambda l, j: (l, 0, j)),
        ],
        out_specs=pl.BlockSpec((None, rows, cols), lambda l, j: (l, 0, j)),
        compiler_params=_cparams("arbitrary", "arbitrary"),
        name="modulation",
    )(cc, w_mod, b_mod.reshape(DEPTH, 1, 6 * D_MODEL))


def _pick(ctx_ref, lat_ref, tile, n_ctx_tiles):
    return jnp.where(tile < n_ctx_tiles, ctx_ref[...], lat_ref[...])


def _src_specs(cols, batch, n_ctx_tiles, tile_off, lat_off):
    blk = (batch, TILE_TOK, cols)
    return [pl.BlockSpec(blk, lambda j, *_: (0, jnp.minimum(j + tile_off, n_ctx_tiles - 1), 0)),
            pl.BlockSpec(blk, lambda j, *_: (0, jnp.maximum(j + tile_off - n_ctx_tiles, 0) + lat_off,
                                             0))]


def _mod_spec(batch, layer, n_ctx_tiles, tile_off):
    return pl.BlockSpec((None, None, batch, 1, 6 * D_MODEL),
                        lambda j, *_: (layer, (j + tile_off >= n_ctx_tiles).astype(jnp.int32), 0, 0, 0))


def _lane_group(batch):
    return lax.broadcasted_iota(jnp.int32, (batch, 128), 1) // SSM_GROUP


def _to_chunk_layout(ua_scr, u_ref, batch):
    blk = _lane_group(batch)
    for cl in range(TILE_TOK // S5_CHUNK):
        for g in range(N_SSM_GROUPS):
            for q in range(2):
                acc = None
                for s in range(8 * q, 8 * q + 8):
                    src = ua_scr[g // 8, pl.ds(cl * S5_CHUNK + s, batch, stride=ROW_PITCH), :]
                    shift = ((s - g) % 8) * SSM_GROUP
                    v = pltpu.roll(src, shift, 1) if shift else src
                    acc = v if acc is None else jnp.where(blk == s % 8, v, acc)
                u_ref[g, cl * batch:(cl + 1) * batch, 128 * q:128 * q + 128] = acc


def _from_chunk_layout(y_ref, ya_scr, batch):
    blk = _lane_group(batch)
    for cl in range(TILE_TOK // S5_CHUNK):
        for s in range(S5_CHUNK):
            for p in range(2):
                acc = None
                for g in range(8 * p, 8 * p + 8):
                    src = y_ref[g, cl * batch:(cl + 1) * batch, 128 * (s // 8):128 * (s // 8) + 128]
                    shift = ((g - s) % 8) * SSM_GROUP
                    v = pltpu.roll(src, shift, 1) if shift else src
                    acc = v if acc is None else jnp.where(blk == g % 8, v, acc)
                ya_scr[p, pl.ds(cl * S5_CHUNK + s, batch, stride=ROW_PITCH), :] = acc


def _start_row_gather(idx_at, src_hbm, dst_ref, sem, part=0, n_parts=1, max_row=None):
    n = dst_ref.shape[0]
    for r in range(part * n // n_parts, (part + 1) * n // n_parts):
        row = idx_at(r)
        if max_row is not None:
            row = jnp.minimum(row, max_row)
        pltpu.make_async_copy(src_hbm.at[pl.ds(row, 1)], dst_ref.at[pl.ds(r, 1)],
                              sem).start(priority=r % 2)


def _rows_of(idx_ref, row0):
    return lambda r: idx_ref[row0 + r // 128, r % 128]


def _wait_row_gather(src_hbm, dst_ref, sem):
    pltpu.make_async_copy(src_hbm.at[pl.ds(0, dst_ref.shape[0])], dst_ref, sem).wait()


def _weighted_expert_rows(rcol_ref, f_rows):
    return (rcol_ref[:, 1:2] * f_rows[:, 0:D_MODEL] + rcol_ref[:, 2:3] * f_rows[:, D_MODEL:2 * D_MODEL])


INPROJ_DOTS = 4


def _inproj_kernel(*refs, batch, n_ctx_tiles, moe_in):
    if moe_in:
        (dest_ref, xc_ref, xl_ref, fs_hbm, rcol_ref, modp_ref, mod_ref, g_ref, w_ref,
         x2_ref, u_ref, ub_ref, gate_ref, h_scr, ua_scr, f_buf, sem) = refs
    else:
        xc_ref, xl_ref, mod_ref, g_ref, w_ref, u_ref, ub_ref, gate_ref, h_scr, ua_scr = refs
    rows = batch * TILE_TOK
    j = pl.program_id(0)
    last = pl.num_programs(0) - 1
    slot = j % 2
    x = _pick(xc_ref, xl_ref, j, n_ctx_tiles)

    def fetch(part):
        if moe_in:
            _start_row_gather(_rows_of(dest_ref, jnp.minimum(j + 1, last) * (rows // 128)), fs_hbm,
                              f_buf.at[1 - slot], sem.at[1 - slot], part, INPROJ_DOTS)

    if moe_in:
        @pl.when(j == 0)
        def _():
            _start_row_gather(_rows_of(dest_ref, 0), fs_hbm, f_buf.at[0], sem.at[0])

        _wait_row_gather(fs_hbm, f_buf.at[slot], sem.at[slot])
        f = _weighted_expert_rows(rcol_ref, f_buf[slot])
        x = x + modp_ref[:, :, 5 * D_MODEL:6 * D_MODEL] * f.reshape(batch, TILE_TOK, D_MODEL)
        x2_ref[...] = x
    y = x * lax.rsqrt(jnp.mean(x * x, axis=-1, keepdims=True) + EPS) * g_ref[...]
    shift = mod_ref[:, :, 0:D_MODEL]
    scale = mod_ref[:, :, D_MODEL:2 * D_MODEL]
    h_scr[...] = (y * (1.0 + scale) + shift).reshape(rows, D_MODEL).astype(BF16)
    h = h_scr[...]
    fetch(0)
    ua = jnp.dot(h, w_ref[:, 0:D_SSM], preferred_element_type=F32)
    for b in range(batch):
        for p in range(2):
            ua_scr[p, b * ROW_PITCH:b * ROW_PITCH + TILE_TOK, :] = (
                ua[b * TILE_TOK:(b + 1) * TILE_TOK, 128 * p:128 * p + 128])
    fetch(1)
    ub = jnp.dot(h, w_ref[:, D_SSM:GATE_OFF], preferred_element_type=F32)
    ub_ref[...] = ub.reshape(batch, TILE_TOK, D_POOL)
    for k in range(2):
        lo = GATE_OFF + k * D_MODEL
        fetch(2 + k)
        g = jnp.dot(h, w_ref[:, lo:lo + D_MODEL], preferred_element_type=F32)
        gate_ref[:, :, k * D_MODEL:(k + 1) * D_MODEL] = (
            jax.nn.sigmoid(g).astype(BF16).reshape(batch, TILE_TOK, D_MODEL))
    _to_chunk_layout(ua_scr, u_ref, batch)
    if moe_in:
        @pl.when(j == last)
        def _():
            _wait_row_gather(fs_hbm, f_buf.at[1 - slot], sem.at[1 - slot])


def _inproj(x_ctx, x_lat, lat_off, mod_sel, norm_g, w_in_bf, layer, batch, n_ctx_tiles, n_tiles,
            moe=None):
    rows = batch * TILE_TOK
    n_tok = n_tiles * TILE_TOK
    chunk_rows = batch * TILE_TOK // S5_CHUNK
    tok = lambda cols: pl.BlockSpec((batch, TILE_TOK, cols), lambda j, *_: (0, j, 0))
    out_shape = [jax.ShapeDtypeStruct((N_SSM_GROUPS, n_tiles * chunk_rows, D_SSM), F32),
                 jax.ShapeDtypeStruct((batch, n_tok, D_POOL), F32),
                 jax.ShapeDtypeStruct((batch, n_tok, 2 * D_MODEL), BF16)]
    out_specs = [pl.BlockSpec((N_SSM_GROUPS, chunk_rows, D_SSM), lambda j, *_: (0, j, 0)),
                 tok(D_POOL), tok(2 * D_MODEL)]
    in_specs = _src_specs(D_MODEL, batch, n_ctx_tiles, 0, lat_off)
    args = [x_ctx, x_lat]
    scratch = [pltpu.VMEM((rows, D_MODEL), BF16), pltpu.VMEM((2, batch * ROW_PITCH, 128), F32)]
    prefetch = []
    if moe is not None:
        dest, fs, rcol = moe
        prefetch = [dest]
        in_specs += [pl.BlockSpec(memory_space=pl.ANY),
                     pl.BlockSpec((rows, ROUTE_LANES), lambda j, *_: (j, 0)),
                     _mod_spec(batch, layer - 1, n_ctx_tiles, 0)]
        args += [fs, rcol, mod_sel]
        out_shape.insert(0, jax.ShapeDtypeStruct((batch, n_tok, D_MODEL), F32))
        out_specs.insert(0, tok(D_MODEL))
        scratch += [pltpu.VMEM((2, rows, 2 * D_MODEL), F32), pltpu.SemaphoreType.DMA((2,))]
    in_specs += [_mod_spec(batch, layer, n_ctx_tiles, 0),
                 pl.BlockSpec((None, 1, D_MODEL), lambda j, *_: (layer, 0, 0)),
                 pl.BlockSpec((None, D_MODEL, D_IN), lambda j, *_: (layer, 0, 0))]
    args += [mod_sel, norm_g, w_in_bf]
    return pl.pallas_call(
        functools.partial(_inproj_kernel, batch=batch, n_ctx_tiles=n_ctx_tiles, moe_in=moe is not None),
        out_shape=tuple(out_shape),
        grid_spec=pltpu.PrefetchScalarGridSpec(
            num_scalar_prefetch=len(prefetch), grid=(n_tiles,), in_specs=in_specs,
            out_specs=tuple(out_specs), scratch_shapes=scratch),
        compiler_params=_cparams("arbitrary"),
        name="inproj",
    )(*prefetch, *args)


def _s5_matrices(lam_re, lam_im, log_dt, b_re, b_im, c_re, c_im):
    L, G, N, H = S5_CHUNK, N_SSM_GROUPS, SSM_STATE, SSM_GROUP
    lr, li = lam_re.astype(F32), lam_im.astype(F32)
    dt = jnp.exp(log_dt.astype(F32))[..., None]
    zr, zi = lr * dt, li * dt
    k = jnp.arange(L + 1, dtype=F32)[:, None, None, None]
    pm = jnp.exp(zr[None] * k)
    pr, pi = pm * jnp.cos(zi[None] * k), pm * jnp.sin(zi[None] * k)
    nr, ni = pr[1] - 1.0, pi[1]
    den = lr * lr + li * li
    fr = (nr * lr + ni * li) / den
    fi = (ni * lr - nr * li) / den
    br, bi = b_re.astype(F32), b_im.astype(F32)
    bbr = fr[..., None] * br - fi[..., None] * bi
    bbi = fr[..., None] * bi + fi[..., None] * br
    cr, ci = c_re.astype(F32), c_im.astype(F32)
    cpr = cr[None] * pr[:, :, :, None, :] - ci[None] * pi[:, :, :, None, :]
    cpi = cr[None] * pi[:, :, :, None, :] + ci[None] * pr[:, :, :, None, :]
    kern = (jnp.einsum('kdgan,dgnh->kdgah', cpr, bbr)
            - jnp.einsum('kdgan,dgnh->kdgah', cpi, bbi))
    s_idx = np.arange(L)[:, None]
    t_idx = np.arange(L)[None, :]
    lag_f = np.clip(t_idx - s_idx, 0, L - 1)
    lag_b = np.clip(s_idx - t_idx, 0, L - 1)
    mf = jnp.where((t_idx >= s_idx)[:, :, None, None, None], kern[:, 0][lag_f], 0.0)
    mb = jnp.where((s_idx >= t_idx)[:, :, None, None, None], kern[:, 1][lag_b], 0.0)
    m = (mf + mb).transpose(2, 0, 4, 1, 3).reshape(G, L * H, L * H)
    powers = (L - 1 - np.arange(L), np.arange(L))
    inj = []
    for d in range(2):
        p_r, p_i = pr[powers[d], d], pi[powers[d], d]
        wr = p_r[..., None] * bbr[d][None] - p_i[..., None] * bbi[d][None]
        wi = p_r[..., None] * bbi[d][None] + p_i[..., None] * bbr[d][None]
        inj.append((wr, wi))
    w_inj = jnp.concatenate([inj[0][0], inj[1][0], inj[0][1], inj[1][1]], axis=2)
    mcat = jnp.concatenate([m, w_inj.transpose(1, 0, 3, 2).reshape(G, L * H, 4 * N)], axis=2)
    rd = (np.arange(L) + 1, L - np.arange(L))
    w_rd = jnp.concatenate([cpr[rd[0], 0], cpr[rd[1], 1], -cpi[rd[0], 0], -cpi[rd[1], 1]],
                           axis=3)
    wy = w_rd.transpose(1, 3, 0, 2).reshape(G, 4 * N, L * H)
    a_r = jnp.concatenate([pr[L][0], pr[L][1]], axis=-1)
    a_i = jnp.concatenate([pi[L][0], pi[L][1]], axis=-1)
    coef = jnp.stack([a_r, a_i] + [jnp.zeros_like(a_r)] * 6, axis=1)
    return mcat.astype(BF16), wy.astype(BF16), coef


def _s5_kernel(u_ref, mcat_ref, wy_ref, coef_ref, dskip_ref, y_ref, s_scr, h_scr, *, batch,
               n_chunks, n_ctx_chunks):
    n = SSM_STATE
    rb = S5_CHUNK * batch
    n_rb = n_chunks // S5_CHUNK

    def inject(i, c):
        r0 = pl.multiple_of(i * rb, rb)
        u = u_ref[pl.ds(r0, rb), :]
        ub = u.astype(BF16)
        y_ref[pl.ds(r0, rb), :] = (jnp.dot(ub, mcat_ref[:, 0:D_SSM], preferred_element_type=F32)
                                   + dskip_ref[...] * u)
        s_scr[pl.ds(r0, rb), :] = jnp.dot(ub, mcat_ref[:, D_SSM:2 * D_SSM],
                                          preferred_element_type=F32)
        return c

    lax.fori_loop(0, n_rb, inject, 0)
    a_r, a_i = coef_ref[0:1, :], coef_ref[1:2, :]
    fwd_lanes = lax.broadcasted_iota(jnp.int32, (batch, 2 * n), 1) < n

    def step(k, carry):
        h_re, h_im = carry
        cb = jnp.where(k < n_ctx_chunks, n_ctx_chunks - 1 - k, n_chunks - 1 + n_ctx_chunks - k)
        rf = pl.multiple_of(k * batch, batch)
        rk = pl.multiple_of(cb * batch, batch)
        h_scr[pl.ds(rf, batch), 0:n] = h_re[:, 0:n]
        h_scr[pl.ds(rk, batch), n:2 * n] = h_re[:, n:2 * n]
        h_scr[pl.ds(rf, batch), 2 * n:3 * n] = h_im[:, 0:n]
        h_scr[pl.ds(rk, batch), 3 * n:4 * n] = h_im[:, n:2 * n]
        s_re = jnp.where(fwd_lanes, s_scr[pl.ds(rf, batch), 0:2 * n], s_scr[pl.ds(rk, batch), 0:2 * n])
        s_im = jnp.where(fwd_lanes, s_scr[pl.ds(rf, batch), 2 * n:4 * n],
                         s_scr[pl.ds(rk, batch), 2 * n:4 * n])
        return (a_r * h_re - a_i * h_im + s_re, a_r * h_im + a_i * h_re + s_im)

    zero = jnp.zeros((batch, 2 * n), F32)
    lax.fori_loop(0, n_chunks, step, (zero, zero), unroll=4)

    def readout(i, c):
        r0 = pl.multiple_of(i * rb, rb)
        y_ref[pl.ds(r0, rb), :] += jnp.dot(h_scr[pl.ds(r0, rb), :].astype(BF16), wy_ref[...],
                                           preferred_element_type=F32)
        return c

    lax.fori_loop(0, n_rb, readout, 0)


def _s5(u_t, mcat, wy, coef, dskip, batch, n_chunks, n_ctx_chunks):
    G = N_SSM_GROUPS
    rows = u_t.shape[1]
    kern = functools.partial(_s5_kernel, batch=batch, n_chunks=n_chunks, n_ctx_chunks=n_ctx_chunks)
    per_group = lambda *blk: pl.BlockSpec((None,) + blk, lambda g: (g, 0, 0))
    return pl.pallas_call(
        kern,
        out_shape=jax.ShapeDtypeStruct((G, rows, D_SSM), F32),
        grid=(G,),
        in_specs=[per_group(rows, D_SSM), per_group(D_SSM, 2 * D_SSM), per_group(D_SSM, D_SSM),
                  per_group(8, 2 * SSM_STATE), per_group(1, D_SSM)],
        out_specs=per_group(rows, D_SSM),
        scratch_shapes=[pltpu.VMEM((rows, D_SSM), F32), pltpu.VMEM((rows, D_SSM), F32)],
        compiler_params=_cparams("arbitrary"),
        name="s5",
    )(u_t, mcat, wy, coef, dskip)


def _window(w):
    return -(w // 2), w - 1 - w // 2


def _pool_constants(rows, width):
    rpb = POOL_BLOCK // width
    pm = np.zeros((4, POOL_BLOCK, POOL_BLOCK), np.float32)
    inv = np.zeros((rows * width, 4), np.float32)
    col = np.arange(width)
    row = np.arange(rows)
    for i, w in enumerate(POOL_WINDOWS):
        lo, hi = _window(w)
        c0, c1 = np.clip(col + lo, 0, width - 1), np.clip(col + hi, 0, width - 1)
        r0, r1 = np.clip(row + lo, 0, rows - 1), np.clip(row + hi, 0, rows - 1)
        band = ((col[None, :] >= c0[:, None]) & (col[None, :] <= c1[:, None])).astype(np.float32)
        for r in range(rpb):
            pm[i, r * width:(r + 1) * width, r * width:(r + 1) * width] = band
        cnt = (r1 - r0 + 1)[:, None] * (c1 - c0 + 1)[None, :]
        inv[:, i] = (1.0 / cnt).reshape(-1)
    return jnp.asarray(pm, BF16), jnp.asarray(inv, F32)


def _pool_segment(u_ref, o_ref, cs_scr, pm_ref, inv_ref, wp_ref, ps_ref, tok0, rows, width):
    n_tok = rows * width
    n_blk = n_tok // POOL_BLOCK
    pad = 8 * width if rows > 1 else 0
    if rows > 1:
        zeros = jnp.zeros((pad, 2 * 128), F32)
        for i in range(4):
            cs_scr[i, 0:pad, :] = zeros
            cs_scr[i, pad + n_tok:pad + n_tok + pad, :] = zeros
    for b in range(n_blk):
        t0 = b * POOL_BLOCK
        for i in range(4):
            seg = POOL_SEG[i]
            xb = u_ref[tok0 + t0:tok0 + t0 + POOL_BLOCK, seg:seg + 256].astype(BF16)
            cs_scr[i, pad + t0:pad + t0 + POOL_BLOCK, :] = jnp.dot(
                pm_ref[i], xb, preferred_element_type=F32)
    lane = lax.broadcasted_iota(jnp.int32, (POOL_BLOCK, 128), 1)
    low_half = lane < 64
    for b in range(n_blk):
        t0 = b * POOL_BLOCK
        win = []
        for i, w in enumerate(POOL_WINDOWS):
            lo, hi = _window(w) if rows > 1 else (0, 0)
            acc = None
            for k in range(lo, hi + 1):
                base = pad + t0 + k * width
                piece = cs_scr[i, base:base + POOL_BLOCK, :]
                acc = piece if acc is None else acc + piece
            win.append(acc * inv_ref[t0:t0 + POOL_BLOCK, i:i + 1])
        x = u_ref[tok0 + t0:tok0 + t0 + POOL_BLOCK, :]
        pooled = jnp.concatenate([
            win[0][:, 0:128],
            jnp.where(low_half, win[0][:, 128:256], win[1][:, 0:128]),
            win[1][:, 128:256],
            win[2][:, 0:128],
            jnp.where(low_half, win[2][:, 128:256], win[3][:, 0:128]),
            win[3][:, 128:256]], axis=1)
        d = (pooled - x).astype(BF16)
        y0 = jnp.dot(d[:, 0:384], wp_ref[0], preferred_element_type=F32)
        y1 = jnp.dot(d[:, 384:768], wp_ref[1], preferred_element_type=F32)
        y = jnp.concatenate([y0, y1], axis=1) * ps_ref[...]
        o_ref[tok0 + t0:tok0 + t0 + POOL_BLOCK, :] = y.astype(BF16)


def _pool_kernel(u_ref, pmc_ref, invc_ref, pml_ref, invl_ref, wp_ref, ps_ref, o_ref, cs_scr, *,
                 lat_rows, with_ctx):
    if with_ctx:
        _pool_segment(u_ref, o_ref, cs_scr, pmc_ref, invc_ref, wp_ref, ps_ref, 0, 1, CTX_LEN)
    else:
        o_ref[0:CTX_LEN, :] = jnp.zeros((CTX_LEN, D_POOL), BF16)
    _pool_segment(u_ref, o_ref, cs_scr, pml_ref, invl_ref, wp_ref, ps_ref, CTX_LEN, lat_rows, GRID_W)


def _pool(ub, wp2, pscale, layer, with_ctx):
    batch, n_tok, _ = ub.shape
    lat_rows = (n_tok - CTX_LEN) // GRID_W
    pmc, invc = _pool_constants(1, CTX_LEN)
    pml, invl = _pool_constants(lat_rows, GRID_W)
    const = lambda *blk: pl.BlockSpec(blk, lambda b: (0,) * len(blk))
    return pl.pallas_call(
        functools.partial(_pool_kernel, lat_rows=lat_rows, with_ctx=with_ctx),
        out_shape=jax.ShapeDtypeStruct((batch, n_tok, D_POOL), BF16),
        grid=(batch,),
        in_specs=[
            pl.BlockSpec((None, n_tok, D_POOL), lambda b: (b, 0, 0)),
            const(4, POOL_BLOCK, POOL_BLOCK), const(CTX_LEN, 4),
            const(4, POOL_BLOCK, POOL_BLOCK), const(lat_rows * GRID_W, 4),
            pl.BlockSpec((None, 2, 384, 384), lambda b: (layer, 0, 0, 0)),
            pl.BlockSpec((None, 1, D_POOL), lambda b: (layer, 0, 0)),
        ],
        out_specs=pl.BlockSpec((None, n_tok, D_POOL), lambda b: (b, 0, 0)),
        scratch_shapes=[pltpu.VMEM((4, lat_rows * GRID_W + 16 * GRID_W, 256), F32)],
        compiler_params=_cparams("arbitrary"),
        name="pool",
    )(ub, pmc, invc, pml, invl, wp2, pscale)


def _first_argmax(rows_):
    best, idx = rows_[0], jnp.zeros_like(rows_[0], dtype=jnp.int32)
    for k in range(1, len(rows_)):
        take = rows_[k] > best
        idx = jnp.where(take, k, idx)
        best = jnp.where(take, rows_[k], best)
    return best, idx


def _route(lt):
    g_rows = [lt[k:k + 1, :] for k in range(N_EXPERT_GROUPS)]
    best, grp = _first_argmax(g_rows)
    denom = sum(jnp.exp(r - best) for r in g_rows)
    p_grp = 1.0 / denom
    inner = []
    for e in range(EXPERTS_PER_GROUP):
        acc = jnp.zeros_like(best)
        for g in range(N_EXPERT_GROUPS):
            r = 4 + 4 * g + e
            acc = jnp.where(grp == g, lt[r:r + 1, :], acc)
        inner.append(acc)
    v1, i1 = _first_argmax(inner)
    masked = [jnp.where(i1 == e, -jnp.inf, inner[e]) for e in range(EXPERTS_PER_GROUP)]
    v2, i2 = _first_argmax(masked)
    e21 = jnp.exp(v2 - v1)
    w1 = p_grp / (1.0 + e21)
    w2 = p_grp * e21 / (1.0 + e21)
    first_low = i1 < i2
    lo = jnp.where(first_low, i1, i2)
    hi = jnp.where(first_low, i2, i1)
    w_lo = jnp.where(first_low, w1, w2)
    w_hi = jnp.where(first_low, w2, w1)
    off = jnp.where(lo == 0, 0, jnp.where(lo == 1, 3, 5))
    cls = N_PAIRS * grp + off + hi - lo - 1
    return cls.astype(F32), w_lo, w_hi


def _merge_kernel(xc_ref, xl_ref, y_ref, yb_ref, gate_ref, mod_ref, g2_ref, wglu_ref, bglu_ref,
                  wbra_ref, wbrb_ref, wout_ref, wrh_ref, wrl_ref, br_ref, x1_ref, h2_ref, route_ref,
                  rcol_ref, ya_scr, *, batch, n_ctx_tiles, tile_off):
    D = D_MODEL
    rows = batch * TILE_TOK
    tile = pl.program_id(0) + tile_off
    _from_chunk_layout(y_ref, ya_scr, batch)
    y = jnp.concatenate(
        [jnp.concatenate([ya_scr[p, b * ROW_PITCH:b * ROW_PITCH + TILE_TOK, :] for p in range(2)],
                         axis=1) for b in range(batch)], axis=0)
    z = jax.nn.gelu(y)
    glu = jax.nn.sigmoid(jnp.dot(z.astype(BF16), wglu_ref[...], preferred_element_type=F32)
                         + bglu_ref[...])
    ya = (z * glu).astype(BF16)
    gates = gate_ref[...].reshape(rows, 2 * D)
    m = gates[:, 0:D].astype(F32) * jnp.dot(ya, wbra_ref[...], preferred_element_type=F32)
    m += gates[:, D:2 * D].astype(F32) * jnp.dot(yb_ref[...].reshape(rows, D_POOL), wbrb_ref[...],
                                                 preferred_element_type=F32)
    out = jnp.dot(m.astype(BF16), wout_ref[...], preferred_element_type=F32)
    x_in = _pick(xc_ref, xl_ref, tile, n_ctx_tiles)
    x1 = x_in + mod_ref[:, :, 2 * D:3 * D] * out.reshape(batch, TILE_TOK, D)
    x1_ref[...] = x1
    yn = x1 * lax.rsqrt(jnp.mean(x1 * x1, axis=-1, keepdims=True) + EPS) * g2_ref[...]
    h2 = yn * (1.0 + mod_ref[:, :, 4 * D:5 * D]) + mod_ref[:, :, 3 * D:4 * D]
    h2_ref[...] = h2.reshape(rows, D)
    h_hi, h_lo = _split_bf16(h2.reshape(rows, D))
    logits = jnp.dot(h_hi, wrh_ref[...], preferred_element_type=F32)
    logits += jnp.dot(h_lo, wrh_ref[...], preferred_element_type=F32)
    logits += jnp.dot(h_hi, wrl_ref[...], preferred_element_type=F32)
    logits += br_ref[...]
    cls, w_lo, w_hi = _route(logits.T)
    zero = jnp.zeros_like(cls)
    route_ref[...] = jnp.concatenate([cls, w_lo, w_hi, zero, zero, zero, zero, zero], axis=0)
    cols = jnp.concatenate([cls, w_lo, w_hi, jnp.zeros((ROUTE_LANES - 3, rows), F32)], axis=0)
    rcol_ref[...] = cols.T


def _merge(x_ctx, x_lat, lat_off, y_t, yb, gates, mod_sel, norm2_g, p, layer, batch, tile_off,
           n_tiles, n_ctx_tiles):
    rows = batch * TILE_TOK
    chunk_rows = rows // S5_CHUNK
    tok = lambda cols: pl.BlockSpec((batch, TILE_TOK, cols), lambda j: (0, j + tile_off, 0))
    lay3 = lambda *blk: pl.BlockSpec((None,) + blk, lambda j: (layer, 0, 0))
    n_tok = n_tiles * TILE_TOK
    return pl.pallas_call(
        functools.partial(_merge_kernel, batch=batch, n_ctx_tiles=n_ctx_tiles, tile_off=tile_off),
        out_shape=(jax.ShapeDtypeStruct((batch, n_tok, D_MODEL), F32),
                   jax.ShapeDtypeStruct((n_tiles * rows, D_MODEL), F32),
                   jax.ShapeDtypeStruct((8, n_tiles * rows), F32),
                   jax.ShapeDtypeStruct((n_tiles * rows, ROUTE_LANES), F32)),
        grid=(n_tiles,),
        in_specs=_src_specs(D_MODEL, batch, n_ctx_tiles, tile_off, lat_off) + [
            pl.BlockSpec((N_SSM_GROUPS, chunk_rows, D_SSM), lambda j: (0, j + tile_off, 0)),
            tok(D_POOL), tok(2 * D_MODEL),
            _mod_spec(batch, layer, n_ctx_tiles, tile_off),
            lay3(1, D_MODEL), lay3(D_SSM, D_SSM), lay3(1, D_SSM), lay3(D_SSM, D_MODEL),
            lay3(D_POOL, D_MODEL), lay3(D_MODEL, D_MODEL), lay3(D_MODEL, ROUTE_LANES),
            lay3(D_MODEL, ROUTE_LANES), lay3(1, ROUTE_LANES),
        ],
        out_specs=(pl.BlockSpec((batch, TILE_TOK, D_MODEL), lambda j: (0, j, 0)),
                   pl.BlockSpec((rows, D_MODEL), lambda j: (j, 0)),
                   pl.BlockSpec((8, rows), lambda j: (0, j)),
                   pl.BlockSpec((rows, ROUTE_LANES), lambda j: (j, 0))),
        scratch_shapes=[pltpu.VMEM((2, batch * ROW_PITCH, 128), F32)],
        compiler_params=_cparams("arbitrary"),
        name="merge",
    )(x_ctx, x_lat, y_t, yb, gates, mod_sel, norm2_g, p['w_glu'], p['b_glu'], p['w_br_a'],
      p['w_br_b'], p['w_out'], p['wr_hi'], p['wr_lo'], p['b_r'])


def _plan_kernel(cls_ref, dest_ref, src_ref, meta_ref, dest_smem, fill_scr, sem, *, n_q):
    rows_q = n_q // 128
    cls = cls_ref[...]
    r_i = lax.broadcasted_iota(jnp.int32, (128, 128), 0)
    c_i = lax.broadcasted_iota(jnp.int32, (128, 128), 1)
    upper = jnp.where(r_i <= c_i, 1.0, 0.0).astype(BF16)
    r_q = lax.broadcasted_iota(jnp.int32, (rows_q, rows_q), 0)
    c_q = lax.broadcasted_iota(jnp.int32, (rows_q, rows_q), 1)
    lower = jnp.where(c_q < r_q, 1.0, 0.0).astype(BF16)
    tile_start = (lax.broadcasted_iota(jnp.int32, (1, META_LANES), 1) * EXPERT_TILE).astype(F32)
    start = jnp.zeros((1, 128), F32)
    dest = jnp.zeros((rows_q, 128), F32)
    tile_cls = jnp.zeros((1, META_LANES), F32)
    for c in range(N_CLASSES):
        m = cls == float(c)
        incl = jnp.dot(jnp.where(m, 1.0, 0.0).astype(BF16), upper, preferred_element_type=F32)
        row_tot = jnp.broadcast_to(incl[:, 127:128], (rows_q, 128))
        before = jnp.dot(lower, row_tot.astype(BF16), preferred_element_type=F32)
        dest = jnp.where(m, start + before + incl - 1.0, dest)
        total = before[rows_q - 1:rows_q, :] + row_tot[rows_q - 1:rows_q, :]
        start = start + jnp.floor((total + (EXPERT_TILE - 1.0)) * (1.0 / EXPERT_TILE)) * EXPERT_TILE
        end2 = jnp.concatenate([start, start], axis=1)
        tile_cls = tile_cls + jnp.where(tile_start >= end2, 1.0, 0.0)
    active = jnp.where(tile_start < end2, 1.0, 0.0)
    tc = jnp.minimum(tile_cls, N_CLASSES - 1.0)
    grp = sum(jnp.where(tc >= float(N_PAIRS * k), 1.0, 0.0) for k in range(1, N_EXPERT_GROUPS))
    pair = tc - N_PAIRS * grp
    p_lo = jnp.where(pair >= 3.0, 1.0, 0.0) + jnp.where(pair >= 5.0, 1.0, 0.0)
    p_hi = jnp.where(pair == 0.0, 1.0, jnp.where(pair == 1.0, 2.0, jnp.where(pair == 3.0, 2.0, 3.0)))
    lane = lax.broadcasted_iota(jnp.int32, (1, META_LANES), 1)
    fresh = jnp.where((lane == 0) | (tc != pltpu.roll(tc, 1, 1)), 1.0, 0.0)
    zero = jnp.zeros_like(tc)
    meta_ref[...] = jnp.concatenate(
        [EXPERTS_PER_GROUP * grp + p_lo, EXPERTS_PER_GROUP * grp + p_hi, active, fresh,
         zero, zero, zero, zero], axis=0).astype(jnp.int32)
    dest_ref[...] = dest.astype(jnp.int32)

    fill_scr[...] = jnp.full(fill_scr.shape, n_q, jnp.int32)
    fill = pltpu.make_async_copy(fill_scr, src_ref, sem.at[0])
    stage = pltpu.make_async_copy(dest_ref, dest_smem, sem.at[1])
    fill.start()
    stage.start()
    fill.wait()
    stage.wait()

    def invert(i, carry):
        for k in range(128):
            src_ref[dest_smem[i, k]] = i * 128 + k
        return carry

    lax.fori_loop(0, rows_q, invert, 0)


def _plan(cls_q):
    n_q = cls_q.shape[0]
    rows_q = n_q // 128
    n_tiles = n_q // EXPERT_TILE + N_CLASSES
    assert n_tiles <= META_LANES
    n_rows = n_tiles * EXPERT_TILE
    i32 = jnp.int32
    return pl.pallas_call(
        functools.partial(_plan_kernel, n_q=n_q),
        out_shape=(jax.ShapeDtypeStruct((rows_q, 128), i32), jax.ShapeDtypeStruct((n_rows,), i32),
                   jax.ShapeDtypeStruct((8, META_LANES), i32)),
        in_specs=[pl.BlockSpec(memory_space=pltpu.VMEM)],
        out_specs=(pl.BlockSpec(memory_space=pltpu.VMEM), pl.BlockSpec(memory_space=pltpu.SMEM),
                   pl.BlockSpec(memory_space=pltpu.VMEM)),
        scratch_shapes=[pltpu.SMEM((rows_q, 128), i32), pltpu.VMEM((n_rows,), i32),
                        pltpu.SemaphoreType.DMA((2,))],
        compiler_params=pltpu.CompilerParams(vmem_limit_bytes=V7X_VMEM_LIMIT),
        name="plan",
    )(cls_q.reshape(rows_q, 128))


GATHER_SLOTS = 3
EXPERT_DOTS = 6


def _expert_kernel(src_ref, elo_ref, ehi_ref, act_ref, fresh_ref, h2_hbm, w1a_ref, w1b_ref, w3a_ref,
                   w3b_ref, w2a_ref, w2b_ref, o_ref, x_buf, w1a_bf, w1b_bf, w3a_bf, w3b_bf, w2a_bf,
                   w2b_bf, sem, *, n_q):
    j = pl.program_id(0)
    last = pl.num_programs(0) - 1
    ahead = GATHER_SLOTS - 1
    slot = lax.rem(j, GATHER_SLOTS)

    def fetch(tile, into, part=0, n_parts=1):
        base = jnp.minimum(tile, last) * EXPERT_TILE
        _start_row_gather(lambda r: src_ref[base + r], h2_hbm, x_buf.at[into], sem.at[into], part,
                          n_parts, n_q - 1)

    @pl.when(j == 0)
    def _():
        for t in range(ahead):
            fetch(t, t)

    _wait_row_gather(h2_hbm, x_buf.at[slot], sem.at[slot])
    nxt, nxt_slot = j + ahead, lax.rem(j + ahead, GATHER_SLOTS)

    @pl.when(fresh_ref[j] == 1)
    def _():
        for src_w, dst_w in ((w1a_ref, w1a_bf), (w1b_ref, w1b_bf), (w3a_ref, w3a_bf),
                             (w3b_ref, w3b_bf), (w2a_ref, w2a_bf), (w2b_ref, w2b_bf)):
            dst_w[...] = src_w[...].astype(BF16)

    @pl.when(act_ref[j] == 1)
    def _():
        x = x_buf[slot].astype(BF16)
        for k, (w1, w3, w2) in enumerate(((w1a_bf, w3a_bf, w2a_bf), (w1b_bf, w3b_bf, w2b_bf))):
            fetch(nxt, nxt_slot, 3 * k, EXPERT_DOTS)
            a = jnp.dot(x, w1[...], preferred_element_type=F32)
            fetch(nxt, nxt_slot, 3 * k + 1, EXPERT_DOTS)
            b = jnp.dot(x, w3[...], preferred_element_type=F32)
            h = (a * jax.nn.sigmoid(a) * b).astype(BF16)
            fetch(nxt, nxt_slot, 3 * k + 2, EXPERT_DOTS)
            o_ref[:, k * D_MODEL:(k + 1) * D_MODEL] = jnp.dot(h, w2[...], preferred_element_type=F32)

    @pl.when(act_ref[j] == 0)
    def _():
        fetch(nxt, nxt_slot)
        o_ref[...] = jnp.zeros_like(o_ref)

    @pl.when(j == last)
    def _():
        for t in range(1, GATHER_SLOTS):
            into = lax.rem(j + t, GATHER_SLOTS)
            _wait_row_gather(h2_hbm, x_buf.at[into], sem.at[into])


def _experts(src, meta, h2, w1, w3, w2, layer):
    n_q = h2.shape[0]
    n_rows = src.shape[0]
    te = EXPERT_TILE
    n_tiles = n_rows // te
    up = lambda sel: pl.BlockSpec((None, None, D_MODEL, D_EXPERT),
                                  lambda j, s, lo, hi, act, fr: (layer, (lo, hi)[sel][j], 0, 0))
    down = lambda sel: pl.BlockSpec((None, None, D_EXPERT, D_MODEL),
                                    lambda j, s, lo, hi, act, fr: (layer, (lo, hi)[sel][j], 0, 0))
    return pl.pallas_call(
        functools.partial(_expert_kernel, n_q=n_q),
        out_shape=jax.ShapeDtypeStruct((n_rows, 2 * D_MODEL), F32),
        grid_spec=pltpu.PrefetchScalarGridSpec(
            num_scalar_prefetch=5,
            grid=(n_tiles,),
            in_specs=[pl.BlockSpec(memory_space=pl.ANY),
                      up(0), up(1), up(0), up(1), down(0), down(1)],
            out_specs=pl.BlockSpec((te, 2 * D_MODEL), lambda j, s, lo, hi, act, fr: (j, 0)),
            scratch_shapes=[pltpu.VMEM((GATHER_SLOTS, te, D_MODEL), F32)]
            + [pltpu.VMEM((D_MODEL, D_EXPERT), BF16)] * 4 + [pltpu.VMEM((D_EXPERT, D_MODEL), BF16)] * 2
            + [pltpu.SemaphoreType.DMA((GATHER_SLOTS,))],
        ),
        compiler_params=_cparams("arbitrary"),
        name="experts",
    )(src, meta[0, :n_tiles], meta[1, :n_tiles], meta[2, :n_tiles], meta[3, :n_tiles],
      h2, w1, w1, w3, w3, w2, w2)


def _combine_kernel(dest_ref, x_ref, fs_hbm, rcol_ref, mod_ref, g_ref, o_ref, f_buf, sem, *, batch,
                    final):
    j = pl.program_id(0)
    last = pl.num_programs(0) - 1
    slot = j % 2
    rows = batch * TILE_TOK
    idx_rows = rows // 128

    @pl.when(j == 0)
    def _():
        _start_row_gather(_rows_of(dest_ref, 0), fs_hbm, f_buf.at[0], sem.at[0])

    _wait_row_gather(fs_hbm, f_buf.at[slot], sem.at[slot])
    _start_row_gather(_rows_of(dest_ref, jnp.minimum(j + 1, last) * idx_rows), fs_hbm,
                      f_buf.at[1 - slot], sem.at[1 - slot])
    f = _weighted_expert_rows(rcol_ref, f_buf[slot])
    x = x_ref[...] + mod_ref[:, :, 5 * D_MODEL:6 * D_MODEL] * f.reshape(batch, TILE_TOK, D_MODEL)
    if final:
        x = x * lax.rsqrt(jnp.mean(x * x, axis=-1, keepdims=True) + EPS) * g_ref[...]
    o_ref[...] = x

    @pl.when(j == last)
    def _():
        _wait_row_gather(fs_hbm, f_buf.at[1 - slot], sem.at[1 - slot])


def _combine(dest, x1, fs, rcol, mod_sel, final_g, layer, tile_off, n_ctx_tiles, final):
    batch, n_tok, _ = x1.shape
    rows = batch * TILE_TOK
    tok = pl.BlockSpec((batch, TILE_TOK, D_MODEL), lambda j, d: (0, j, 0))
    mod_spec = _mod_spec(batch, layer, n_ctx_tiles, tile_off)
    return pl.pallas_call(
        functools.partial(_combine_kernel, batch=batch, final=final),
        out_shape=jax.ShapeDtypeStruct((batch, n_tok, D_MODEL), F32),
        grid_spec=pltpu.PrefetchScalarGridSpec(
            num_scalar_prefetch=1,
            grid=(n_tok // TILE_TOK,),
            in_specs=[tok, pl.BlockSpec(memory_space=pl.ANY),
                      pl.BlockSpec((rows, ROUTE_LANES), lambda j, d: (j, 0)),
                      pl.BlockSpec(mod_spec.block_shape, lambda j, d: mod_spec.index_map(j)),
                      pl.BlockSpec((1, D_MODEL), lambda j, d: (0, 0))],
            out_specs=tok,
            scratch_shapes=[pltpu.VMEM((2, rows, 2 * D_MODEL), F32), pltpu.SemaphoreType.DMA((2,))],
        ),
        compiler_params=_cparams("arbitrary"),
        name="combine",
    )(dest, x1, fs, rcol, mod_sel, final_g)


def kernel(x, c, ctx, c_ctx, norm1_g, norm2_g, w_mod, b_mod, w_in, lam_re, lam_im, log_dt, b_re,
           b_im, c_re, c_im, d_skip, w_glu, b_glu, w_pool, pool_scale, w_br_a, w_br_b, w_out, w_r1,
           b_r1, w_r2, b_r2, w1, w3, w2, final_g):
    batch, seq, d = x.shape
    assert d == D_MODEL and seq % (GRID_W * 8) == 0 and ctx.shape[1] == CTX_LEN and batch % 8 == 0
    n_ctx_tiles, n_lat_tiles = CTX_LEN // TILE_TOK, seq // TILE_TOK
    n_all_tiles = n_ctx_tiles + n_lat_tiles
    n_chunks = (CTX_LEN + seq) // S5_CHUNK
    mod_rows = ((batch + 1 + 7) // 8) * 8

    cc = jnp.concatenate([c, c_ctx[None, :], jnp.zeros((mod_rows - batch - 1, d), F32)], axis=0)
    mod = _modulation(cc, w_mod, b_mod)
    mod_sel = jnp.stack([jnp.broadcast_to(mod[:, batch:batch + 1], (DEPTH, batch, 6 * d)),
                         mod[:, :batch]], axis=1)[:, :, :, None, :]

    w_in_bf = w_in.astype(BF16)
    w_r = jnp.concatenate([w_r1, w_r2.transpose(0, 2, 1, 3).reshape(DEPTH, d, N_EXPERTS),
                           jnp.zeros((DEPTH, d, ROUTE_LANES - 4 - N_EXPERTS), F32)], axis=2)
    wr_hi = w_r.astype(BF16)
    wr_lo = (w_r - wr_hi.astype(F32)).astype(BF16)
    b_r = jnp.concatenate([b_r1, b_r2.reshape(DEPTH, N_EXPERTS),
                           jnp.zeros((DEPTH, ROUTE_LANES - 4 - N_EXPERTS), F32)], axis=1)
    zeros_p = jnp.zeros((DEPTH, POOL_GROUP, POOL_GROUP), F32)
    wp2 = jnp.stack([
        jnp.concatenate([jnp.concatenate([w_pool[:, 0], zeros_p], axis=2),
                         jnp.concatenate([zeros_p, w_pool[:, 1]], axis=2)], axis=1),
        jnp.concatenate([jnp.concatenate([w_pool[:, 2], zeros_p], axis=2),
                         jnp.concatenate([zeros_p, w_pool[:, 3]], axis=2)], axis=1)],
        axis=1).astype(BF16)
    params = {
        'w_glu': w_glu.astype(BF16), 'b_glu': b_glu.reshape(DEPTH, 1, D_SSM),
        'w_br_a': w_br_a.astype(BF16), 'w_br_b': w_br_b.astype(BF16), 'w_out': w_out.astype(BF16),
        'wr_hi': wr_hi, 'wr_lo': wr_lo, 'b_r': b_r.reshape(DEPTH, 1, ROUTE_LANES),
    }
    norm1 = norm1_g.reshape(DEPTH, 1, d)
    norm2 = norm2_g.reshape(DEPTH, 1, d)
    pscale = pool_scale.reshape(DEPTH, 1, D_POOL)
    dskip = jnp.tile(d_skip.reshape(DEPTH, N_SSM_GROUPS, 1, SSM_GROUP), (1, 1, 1, S5_CHUNK))

    s5_mats = jax.vmap(_s5_matrices)(lam_re, lam_im, log_dt, b_re, b_im, c_re, c_im)
    x_ctx, x_lat, lat_off = ctx, x, 0
    moe = None
    for l in range(DEPTH):
        last = l == DEPTH - 1
        outs = _inproj(x_ctx, x_lat, lat_off, mod_sel, norm1, w_in_bf, l, batch, n_ctx_tiles,
                       n_all_tiles, moe)
        if moe is not None:
            x_ctx = x_lat = outs[0]
            outs = outs[1:]
        u_t, ub, gates = outs
        y_t = _s5(u_t, s5_mats[0][l], s5_mats[1][l], s5_mats[2][l], dskip[l], batch, n_chunks,
                  CTX_LEN // S5_CHUNK)
        yb = _pool(ub, wp2, pscale, l, not last)

        tile_off = n_ctx_tiles if last else 0
        n_tiles = n_lat_tiles if last else n_all_tiles
        x1, h2, route, rcol = _merge(x_ctx, x_lat, lat_off, y_t, yb, gates, mod_sel, norm2, params, l,
                                     batch, tile_off, n_tiles, n_ctx_tiles)
        dest, src, meta = _plan(route[0])
        fs = _experts(src, meta, h2, w1, w3, w2, l)
        if last:
            return _combine(dest, x1, fs, rcol, mod_sel, final_g.reshape(1, d), l, tile_off,
                            n_ctx_tiles, True)
        x_ctx, x_lat, lat_off, moe = x1, x1, n_ctx_tiles, (dest, fs, rcol)
```

```python
import functools

import numpy as np
import jax
import jax.numpy as jnp
from jax import lax
from jax.experimental import pallas as pl
from jax.experimental.pallas import tpu as pltpu

F32 = jnp.float32
BF16 = jnp.bfloat16

D_MODEL = 1024
DEPTH = 2
GRID_W = 64
CTX_LEN = 256
EPS = 1e-6

D_SSM = 256
SSM_GROUP = 16
N_SSM_GROUPS = 16
SSM_STATE = 64
S5_CHUNK = 16

D_POOL = 768
POOL_WINDOWS = (2, 4, 8, 16)
POOL_GROUP = 192
POOL_SEG = (0, 128, 384, 512)
GATE_OFF = D_SSM + D_POOL
D_IN = D_SSM + D_POOL + 2 * D_MODEL

N_EXPERT_GROUPS = 4
EXPERTS_PER_GROUP = 4
N_EXPERTS = 16
D_EXPERT = 512
N_PAIRS = 6
N_CLASSES = N_EXPERT_GROUPS * N_PAIRS
ROUTE_LANES = 128

TILE_TOK = 32
ROW_PITCH = TILE_TOK + 4
POOL_BLOCK = 256
EXPERT_TILE = 256
META_LANES = 256
V7X_VMEM_LIMIT = 56 * 1024 * 1024


def _cparams(*sem):
    return pltpu.CompilerParams(dimension_semantics=sem, vmem_limit_bytes=V7X_VMEM_LIMIT)


def _split_bf16(v):
    hi = v.astype(BF16)
    lo = (v - hi.astype(F32)).astype(BF16)
    return hi, lo


def _mod_kernel(c_ref, w_ref, b_ref, o_ref):
    c = c_ref[...]
    a = c * jax.nn.sigmoid(c)
    a_hi, a_lo = _split_bf16(a)
    w_hi, w_lo = _split_bf16(w_ref[...])
    acc = jnp.dot(a_hi, w_hi, preferred_element_type=F32)
    acc += jnp.dot(a_lo, w_hi, preferred_element_type=F32)
    acc += jnp.dot(a_hi, w_lo, preferred_element_type=F32)
    o_ref[...] = acc + b_ref[...]


def _modulation(cc, w_mod, b_mod):
    rows = cc.shape[0]
    nblk = 4
    cols = 6 * D_MODEL // nblk
    return pl.pallas_call(
        _mod_kernel,
        out_shape=jax.ShapeDtypeStruct((DEPTH, rows, 6 * D_MODEL), F32),
        grid=(DEPTH, nblk),
        in_specs=[
            pl.BlockSpec((rows, D_MODEL), lambda l, j: (0, 0)),
            pl.BlockSpec((None, D_MODEL, cols), lambda l, j: (l, 0, j)),
            pl.BlockSpec((None, 1, cols), lambda l, j: (l, 0, j)),
        ],
        out_specs=pl.BlockSpec((None, rows, cols), lambda l, j: (l, 0, j)),
        compiler_params=_cparams("arbitrary", "arbitrary"),
        name="modulation",
    )(cc, w_mod, b_mod.reshape(DEPTH, 1, 6 * D_MODEL))


def _pick(ctx_ref, lat_ref, tile, n_ctx_tiles):
    return jnp.where(tile < n_ctx_tiles, ctx_ref[...], lat_ref[...])


def _src_specs(cols, batch, n_ctx_tiles, tile_off, lat_off):
    blk = (batch, TILE_TOK, cols)
    return [pl.BlockSpec(blk, lambda j, *_: (0, jnp.minimum(j + tile_off, n_ctx_tiles - 1), 0)),
            pl.BlockSpec(blk, lambda j, *_: (0, jnp.maximum(j + tile_off - n_ctx_tiles, 0) + lat_off,
                                             0))]


def _mod_spec(batch, layer, n_ctx_tiles, tile_off):
    return pl.BlockSpec((None, None, batch, 1, 6 * D_MODEL),
                        lambda j, *_: (layer, (j + tile_off >= n_ctx_tiles).astype(jnp.int32), 0, 0, 0))


def _lane_group(batch):
    return lax.broadcasted_iota(jnp.int32, (batch, 128), 1) // SSM_GROUP


def _to_chunk_layout(ua_scr, u_ref, batch):
    blk = _lane_group(batch)
    for cl in range(TILE_TOK // S5_CHUNK):
        for g in range(N_SSM_GROUPS):
            for q in range(2):
                acc = None
                for s in range(8 * q, 8 * q + 8):
                    src = ua_scr[g // 8, pl.ds(cl * S5_CHUNK + s, batch, stride=ROW_PITCH), :]
                    shift = ((s - g) % 8) * SSM_GROUP
                    v = pltpu.roll(src, shift, 1) if shift else src
                    acc = v if acc is None else jnp.where(blk == s % 8, v, acc)
                u_ref[g, cl * batch:(cl + 1) * batch, 128 * q:128 * q + 128] = acc


def _from_chunk_layout(y_ref, ya_scr, batch):
    blk = _lane_group(batch)
    for cl in range(TILE_TOK // S5_CHUNK):
        for s in range(S5_CHUNK):
            for p in range(2):
                acc = None
                for g in range(8 * p, 8 * p + 8):
                    src = y_ref[g, cl * batch:(cl + 1) * batch, 128 * (s // 8):128 * (s // 8) + 128]
                    shift = ((g - s) % 8) * SSM_GROUP
                    v = pltpu.roll(src, shift, 1) if shift else src
                    acc = v if acc is None else jnp.where(blk == g % 8, v, acc)
                ya_scr[p, pl.ds(cl * S5_CHUNK + s, batch, stride=ROW_PITCH), :] = acc


def _start_row_gather(idx_at, src_hbm, dst_ref, sem, part=0, n_parts=1, max_row=None):
    n = dst_ref.shape[0]
    for r in range(part * n // n_parts, (part + 1) * n // n_parts):
        row = idx_at(r)
        if max_row is not None:
            row = jnp.minimum(row, max_row)
        pltpu.make_async_copy(src_hbm.at[pl.ds(row, 1)], dst_ref.at[pl.ds(r, 1)],
                              sem).start(priority=r % 2)


def _rows_of(idx_ref, row0):
    return lambda r: idx_ref[row0 + r // 128, r % 128]


def _wait_row_gather(src_hbm, dst_ref, sem):
    pltpu.make_async_copy(src_hbm.at[pl.ds(0, dst_ref.shape[0])], dst_ref, sem).wait()


def _weighted_expert_rows(rcol_ref, f_rows):
    return (rcol_ref[:, 1:2] * f_rows[:, 0:D_MODEL] + rcol_ref[:, 2:3] * f_rows[:, D_MODEL:2 * D_MODEL])


INPROJ_DOTS = 4


def _inproj_kernel(*refs, batch, n_ctx_tiles, moe_in):
    if moe_in:
        (dest_ref, xc_ref, xl_ref, fs_hbm, rcol_ref, modp_ref, mod_ref, g_ref, w_ref,
         x2_ref, u_ref, ub_ref, gate_ref, h_scr, ua_scr, f_buf, sem) = refs
    else:
        xc_ref, xl_ref, mod_ref, g_ref, w_ref, u_ref, ub_ref, gate_ref, h_scr, ua_scr = refs
    rows = batch * TILE_TOK
    j = pl.program_id(0)
    last = pl.num_programs(0) - 1
    slot = j % 2
    x = _pick(xc_ref, xl_ref, j, n_ctx_tiles)

    def fetch(part):
        if moe_in:
            _start_row_gather(_rows_of(dest_ref, jnp.minimum(j + 1, last) * (rows // 128)), fs_hbm,
                              f_buf.at[1 - slot], sem.at[1 - slot], part, INPROJ_DOTS)

    if moe_in:
        @pl.when(j == 0)
        def _():
            _start_row_gather(_rows_of(dest_ref, 0), fs_hbm, f_buf.at[0], sem.at[0])

        _wait_row_gather(fs_hbm, f_buf.at[slot], sem.at[slot])
        f = _weighted_expert_rows(rcol_ref, f_buf[slot])
        x = x + modp_ref[:, :, 5 * D_MODEL:6 * D_MODEL] * f.reshape(batch, TILE_TOK, D_MODEL)
        x2_ref[...] = x
    y = x * lax.rsqrt(jnp.mean(x * x, axis=-1, keepdims=True) + EPS) * g_ref[...]
    shift = mod_ref[:, :, 0:D_MODEL]
    scale = mod_ref[:, :, D_MODEL:2 * D_MODEL]
    h_scr[...] = (y * (1.0 + scale) + shift).reshape(rows, D_MODEL).astype(BF16)
    h = h_scr[...]
    fetch(0)
    ua = jnp.dot(h, w_ref[:, 0:D_SSM], preferred_element_type=F32)
    for b in range(batch):
        for p in range(2):
            ua_scr[p, b * ROW_PITCH:b * ROW_PITCH + TILE_TOK, :] = (
                ua[b * TILE_TOK:(b + 1) * TILE_TOK, 128 * p:128 * p + 128])
    fetch(1)
    ub = jnp.dot(h, w_ref[:, D_SSM:GATE_OFF], preferred_element_type=F32)
    ub_ref[...] = ub.reshape(batch, TILE_TOK, D_POOL)
    for k in range(2):
        lo = GATE_OFF + k * D_MODEL
        fetch(2 + k)
        g = jnp.dot(h, w_ref[:, lo:lo + D_MODEL], preferred_element_type=F32)
        gate_ref[:, :, k * D_MODEL:(k + 1) * D_MODEL] = (
            jax.nn.sigmoid(g).astype(BF16).reshape(batch, TILE_TOK, D_MODEL))
    _to_chunk_layout(ua_scr, u_ref, batch)
    if moe_in:
        @pl.when(j == last)
        def _():
            _wait_row_gather(fs_hbm, f_buf.at[1 - slot], sem.at[1 - slot])


def _inproj(x_ctx, x_lat, lat_off, mod_sel, norm_g, w_in_bf, layer, batch, n_ctx_tiles, n_tiles,
            moe=None):
    rows = batch * TILE_TOK
    n_tok = n_tiles * TILE_TOK
    chunk_rows = batch * TILE_TOK // S5_CHUNK
    tok = lambda cols: pl.BlockSpec((batch, TILE_TOK, cols), lambda j, *_: (0, j, 0))
    out_shape = [jax.ShapeDtypeStruct((N_SSM_GROUPS, n_tiles * chunk_rows, D_SSM), F32),
                 jax.ShapeDtypeStruct((batch, n_tok, D_POOL), F32),
                 jax.ShapeDtypeStruct((batch, n_tok, 2 * D_MODEL), BF16)]
    out_specs = [pl.BlockSpec((N_SSM_GROUPS, chunk_rows, D_SSM), lambda j, *_: (0, j, 0)),
                 tok(D_POOL), tok(2 * D_MODEL)]
    in_specs = _src_specs(D_MODEL, batch, n_ctx_tiles, 0, lat_off)
    args = [x_ctx, x_lat]
    scratch = [pltpu.VMEM((rows, D_MODEL), BF16), pltpu.VMEM((2, batch * ROW_PITCH, 128), F32)]
    prefetch = []
    if moe is not None:
        dest, fs, rcol = moe
        prefetch = [dest]
        in_specs += [pl.BlockSpec(memory_space=pl.ANY),
                     pl.BlockSpec((rows, ROUTE_LANES), lambda j, *_: (j, 0)),
                     _mod_spec(batch, layer - 1, n_ctx_tiles, 0)]
        args += [fs, rcol, mod_sel]
        out_shape.insert(0, jax.ShapeDtypeStruct((batch, n_tok, D_MODEL), F32))
        out_specs.insert(0, tok(D_MODEL))
        scratch += [pltpu.VMEM((2, rows, 2 * D_MODEL), F32), pltpu.SemaphoreType.DMA((2,))]
    in_specs += [_mod_spec(batch, layer, n_ctx_tiles, 0),
                 pl.BlockSpec((None, 1, D_MODEL), lambda j, *_: (layer, 0, 0)),
                 pl.BlockSpec((None, D_MODEL, D_IN), lambda j, *_: (layer, 0, 0))]
    args += [mod_sel, norm_g, w_in_bf]
    return pl.pallas_call(
        functools.partial(_inproj_kernel, batch=batch, n_ctx_tiles=n_ctx_tiles, moe_in=moe is not None),
        out_shape=tuple(out_shape),
        grid_spec=pltpu.PrefetchScalarGridSpec(
            num_scalar_prefetch=len(prefetch), grid=(n_tiles,), in_specs=in_specs,
            out_specs=tuple(out_specs), scratch_shapes=scratch),
        compiler_params=_cparams("arbitrary"),
        name="inproj",
    )(*prefetch, *args)


def _s5_matrices(lam_re, lam_im, log_dt, b_re, b_im, c_re, c_im):
    L, G, N, H = S5_CHUNK, N_SSM_GROUPS, SSM_STATE, SSM_GROUP
    lr, li = lam_re.astype(F32), lam_im.astype(F32)
    dt = jnp.exp(log_dt.astype(F32))[..., None]
    zr, zi = lr * dt, li * dt
    k = jnp.arange(L + 1, dtype=F32)[:, None, None, None]
    pm = jnp.exp(zr[None] * k)
    pr, pi = pm * jnp.cos(zi[None] * k), pm * jnp.sin(zi[None] * k)
    nr, ni = pr[1] - 1.0, pi[1]
    den = lr * lr + li * li
    fr = (nr * lr + ni * li) / den
    fi = (ni * lr - nr * li) / den
    br, bi = b_re.astype(F32), b_im.astype(F32)
    bbr = fr[..., None] * br - fi[..., None] * bi
    bbi = fr[..., None] * bi + fi[..., None] * br
    cr, ci = c_re.astype(F32), c_im.astype(F32)
    cpr = cr[None] * pr[:, :, :, None, :] - ci[None] * pi[:, :, :, None, :]
    cpi = cr[None] * pi[:, :, :, None, :] + ci[None] * pr[:, :, :, None, :]
    kern = (jnp.einsum('kdgan,dgnh->kdgah', cpr, bbr)
            - jnp.einsum('kdgan,dgnh->kdgah', cpi, bbi))
    lag = (np.arange(L)[None, :] - np.arange(L)[:, None])[None, :, None, :, None]
    kern_t = kern.transpose(0, 1, 2, 4, 3)
    m = sum(jnp.where(lag == k, kern_t[k, 0][:, None, :, None, :], 0.0)
            + jnp.where(lag == -k, kern_t[k, 1][:, None, :, None, :], 0.0) for k in range(L))
    m = m.reshape(G, L * H, L * H)
    powers = (L - 1 - np.arange(L), np.arange(L))
    inj = []
    for d in range(2):
        p_r, p_i = pr[powers[d], d], pi[powers[d], d]
        wr = p_r[..., None] * bbr[d][None] - p_i[..., None] * bbi[d][None]
        wi = p_r[..., None] * bbi[d][None] + p_i[..., None] * bbr[d][None]
        inj.append((wr, wi))
    w_inj = jnp.concatenate([inj[0][0], inj[1][0], inj[0][1], inj[1][1]], axis=2)
    mcat = jnp.concatenate([m, w_inj.transpose(1, 0, 3, 2).reshape(G, L * H, 4 * N)], axis=2)
    rd = (np.arange(L) + 1, L - np.arange(L))
    w_rd = jnp.concatenate([cpr[rd[0], 0], cpr[rd[1], 1], -cpi[rd[0], 0], -cpi[rd[1], 1]],
                           axis=3)
    wy = w_rd.transpose(1, 3, 0, 2).reshape(G, 4 * N, L * H)
    a_r = jnp.concatenate([pr[L][0], pr[L][1]], axis=-1)
    a_i = jnp.concatenate([pi[L][0], pi[L][1]], axis=-1)
    coef = jnp.stack([a_r, a_i] + [jnp.zeros_like(a_r)] * 6, axis=1)
    return mcat.astype(BF16), wy.astype(BF16), coef


S5_GROUPS_PER_STEP = 2


def _s5_kernel(u_ref, mcat_ref, wy_ref, coef_ref, dskip_ref, y_ref, s_scr, h_scr, *, batch,
               n_chunks, n_ctx_chunks):
    n = SSM_STATE
    gps = S5_GROUPS_PER_STEP
    rb = S5_CHUNK * batch
    n_rb = n_chunks // S5_CHUNK

    def inject(i, c):
        r0 = pl.multiple_of(i * rb, rb)
        for g in range(gps):
            u = u_ref[g, pl.ds(r0, rb), :]
            ub = u.astype(BF16)
            y_ref[g, pl.ds(r0, rb), :] = (
                jnp.dot(ub, mcat_ref[g, :, 0:D_SSM], preferred_element_type=F32) + dskip_ref[g] * u)
            s_scr[g, pl.ds(r0, rb), :] = jnp.dot(ub, mcat_ref[g, :, D_SSM:2 * D_SSM],
                                                 preferred_element_type=F32)
        return c

    lax.fori_loop(0, n_rb, inject, 0)
    a_r = [coef_ref[g, 0:1, :] for g in range(gps)]
    a_i = [coef_ref[g, 1:2, :] for g in range(gps)]
    fwd_lanes = lax.broadcasted_iota(jnp.int32, (batch, 2 * n), 1) < n

    def step(k, carry):
        cb = jnp.where(k < n_ctx_chunks, n_ctx_chunks - 1 - k, n_chunks - 1 + n_ctx_chunks - k)
        rf = pl.multiple_of(k * batch, batch)
        rk = pl.multiple_of(cb * batch, batch)
        out = []
        for g in range(gps):
            h_re, h_im = carry[2 * g], carry[2 * g + 1]
            h_scr[g, pl.ds(rf, batch), 0:n] = h_re[:, 0:n]
            h_scr[g, pl.ds(rk, batch), n:2 * n] = h_re[:, n:2 * n]
            h_scr[g, pl.ds(rf, batch), 2 * n:3 * n] = h_im[:, 0:n]
            h_scr[g, pl.ds(rk, batch), 3 * n:4 * n] = h_im[:, n:2 * n]
            s_re = jnp.where(fwd_lanes, s_scr[g, pl.ds(rf, batch), 0:2 * n],
                             s_scr[g, pl.ds(rk, batch), 0:2 * n])
            s_im = jnp.where(fwd_lanes, s_scr[g, pl.ds(rf, batch), 2 * n:4 * n],
                             s_scr[g, pl.ds(rk, batch), 2 * n:4 * n])
            out += [a_r[g] * h_re - a_i[g] * h_im + s_re, a_r[g] * h_im + a_i[g] * h_re + s_im]
        return tuple(out)

    zero = jnp.zeros((batch, 2 * n), F32)
    lax.fori_loop(0, n_chunks, step, (zero,) * (2 * gps), unroll=4)

    def readout(i, c):
        r0 = pl.multiple_of(i * rb, rb)
        for g in range(gps):
            y_ref[g, pl.ds(r0, rb), :] += jnp.dot(h_scr[g, pl.ds(r0, rb), :].astype(BF16), wy_ref[g],
                                                  preferred_element_type=F32)
        return c

    lax.fori_loop(0, n_rb, readout, 0)


def _s5(u_t, mcat, wy, coef, dskip, batch, n_chunks, n_ctx_chunks):
    G = N_SSM_GROUPS
    gps = S5_GROUPS_PER_STEP
    rows = u_t.shape[1]
    kern = functools.partial(_s5_kernel, batch=batch, n_chunks=n_chunks, n_ctx_chunks=n_ctx_chunks)
    per_group = lambda *blk: pl.BlockSpec((gps,) + blk, lambda g: (g, 0, 0))
    return pl.pallas_call(
        kern,
        out_shape=jax.ShapeDtypeStruct((G, rows, D_SSM), F32),
        grid=(G // gps,),
        in_specs=[per_group(rows, D_SSM), per_group(D_SSM, 2 * D_SSM), per_group(D_SSM, D_SSM),
                  per_group(8, 2 * SSM_STATE), per_group(1, D_SSM)],
        out_specs=per_group(rows, D_SSM),
        scratch_shapes=[pltpu.VMEM((gps, rows, D_SSM), F32), pltpu.VMEM((gps, rows, D_SSM), F32)],
        compiler_params=_cparams("arbitrary"),
        name="s5",
    )(u_t, mcat, wy, coef, dskip)


def _window(w):
    return -(w // 2), w - 1 - w // 2


def _pool_constants(rows, width):
    rpb = POOL_BLOCK // width
    pm = np.zeros((4, POOL_BLOCK, POOL_BLOCK), np.float32)
    inv = np.zeros((rows * width, 4), np.float32)
    col = np.arange(width)
    row = np.arange(rows)
    for i, w in enumerate(POOL_WINDOWS):
        lo, hi = _window(w)
        c0, c1 = np.clip(col + lo, 0, width - 1), np.clip(col + hi, 0, width - 1)
        r0, r1 = np.clip(row + lo, 0, rows - 1), np.clip(row + hi, 0, rows - 1)
        band = ((col[None, :] >= c0[:, None]) & (col[None, :] <= c1[:, None])).astype(np.float32)
        for r in range(rpb):
            pm[i, r * width:(r + 1) * width, r * width:(r + 1) * width] = band
        cnt = (r1 - r0 + 1)[:, None] * (c1 - c0 + 1)[None, :]
        inv[:, i] = (1.0 / cnt).reshape(-1)
    return jnp.asarray(pm, BF16), jnp.asarray(inv, F32)


def _pool_segment(u_ref, o_ref, cs_scr, pm_ref, inv_ref, wp_ref, ps_ref, tok0, rows, width):
    n_tok = rows * width
    n_blk = n_tok // POOL_BLOCK
    pad = 8 * width if rows > 1 else 0
    if rows > 1:
        zeros = jnp.zeros((pad, 2 * 128), F32)
        for i in range(4):
            cs_scr[i, 0:pad, :] = zeros
            cs_scr[i, pad + n_tok:pad + n_tok + pad, :] = zeros
    for b in range(n_blk):
        t0 = b * POOL_BLOCK
        for i in range(4):
            seg = POOL_SEG[i]
            xb = u_ref[tok0 + t0:tok0 + t0 + POOL_BLOCK, seg:seg + 256].astype(BF16)
            cs_scr[i, pad + t0:pad + t0 + POOL_BLOCK, :] = jnp.dot(
                pm_ref[i], xb, preferred_element_type=F32)
    lane = lax.broadcasted_iota(jnp.int32, (POOL_BLOCK, 128), 1)
    low_half = lane < 64
    for b in range(n_blk):
        t0 = b * POOL_BLOCK
        win = []
        for i, w in enumerate(POOL_WINDOWS):
            lo, hi = _window(w) if rows > 1 else (0, 0)
            acc = None
            for k in range(lo, hi + 1):
                base = pad + t0 + k * width
                piece = cs_scr[i, base:base + POOL_BLOCK, :]
                acc = piece if acc is None else acc + piece
            win.append(acc * inv_ref[t0:t0 + POOL_BLOCK, i:i + 1])
        x = u_ref[tok0 + t0:tok0 + t0 + POOL_BLOCK, :]
        pooled = jnp.concatenate([
            win[0][:, 0:128],
            jnp.where(low_half, win[0][:, 128:256], win[1][:, 0:128]),
            win[1][:, 128:256],
            win[2][:, 0:128],
            jnp.where(low_half, win[2][:, 128:256], win[3][:, 0:128]),
            win[3][:, 128:256]], axis=1)
        d = (pooled - x).astype(BF16)
        y0 = jnp.dot(d[:, 0:384], wp_ref[0], preferred_element_type=F32)
        y1 = jnp.dot(d[:, 384:768], wp_ref[1], preferred_element_type=F32)
        y = jnp.concatenate([y0, y1], axis=1) * ps_ref[...]
        o_ref[tok0 + t0:tok0 + t0 + POOL_BLOCK, :] = y.astype(BF16)


def _pool_kernel(u_ref, pmc_ref, invc_ref, pml_ref, invl_ref, wp_ref, ps_ref, o_ref, cs_scr, *,
                 lat_rows, with_ctx):
    if with_ctx:
        _pool_segment(u_ref, o_ref, cs_scr, pmc_ref, invc_ref, wp_ref, ps_ref, 0, 1, CTX_LEN)
    else:
        o_ref[0:CTX_LEN, :] = jnp.zeros((CTX_LEN, D_POOL), BF16)
    _pool_segment(u_ref, o_ref, cs_scr, pml_ref, invl_ref, wp_ref, ps_ref, CTX_LEN, lat_rows, GRID_W)


def _pool(ub, wp2, pscale, layer, with_ctx):
    batch, n_tok, _ = ub.shape
    lat_rows = (n_tok - CTX_LEN) // GRID_W
    pmc, invc = _pool_constants(1, CTX_LEN)
    pml, invl = _pool_constants(lat_rows, GRID_W)
    const = lambda *blk: pl.BlockSpec(blk, lambda b: (0,) * len(blk))
    return pl.pallas_call(
        functools.partial(_pool_kernel, lat_rows=lat_rows, with_ctx=with_ctx),
        out_shape=jax.ShapeDtypeStruct((batch, n_tok, D_POOL), BF16),
        grid=(batch,),
        in_specs=[
            pl.BlockSpec((None, n_tok, D_POOL), lambda b: (b, 0, 0)),
            const(4, POOL_BLOCK, POOL_BLOCK), const(CTX_LEN, 4),
            const(4, POOL_BLOCK, POOL_BLOCK), const(lat_rows * GRID_W, 4),
            pl.BlockSpec((None, 2, 384, 384), lambda b: (layer, 0, 0, 0)),
            pl.BlockSpec((None, 1, D_POOL), lambda b: (layer, 0, 0)),
        ],
        out_specs=pl.BlockSpec((None, n_tok, D_POOL), lambda b: (b, 0, 0)),
        scratch_shapes=[pltpu.VMEM((4, lat_rows * GRID_W + 16 * GRID_W, 256), F32)],
        compiler_params=_cparams("arbitrary"),
        name="pool",
    )(ub, pmc, invc, pml, invl, wp2, pscale)


def _first_argmax(rows_):
    best, idx = rows_[0], jnp.zeros_like(rows_[0], dtype=jnp.int32)
    for k in range(1, len(rows_)):
        take = rows_[k] > best
        idx = jnp.where(take, k, idx)
        best = jnp.where(take, rows_[k], best)
    return best, idx


def _route(lt):
    g_rows = [lt[k:k + 1, :] for k in range(N_EXPERT_GROUPS)]
    best, grp = _first_argmax(g_rows)
    denom = sum(jnp.exp(r - best) for r in g_rows)
    p_grp = 1.0 / denom
    inner = []
    for e in range(EXPERTS_PER_GROUP):
        acc = jnp.zeros_like(best)
        for g in range(N_EXPERT_GROUPS):
            r = 4 + 4 * g + e
            acc = jnp.where(grp == g, lt[r:r + 1, :], acc)
        inner.append(acc)
    v1, i1 = _first_argmax(inner)
    masked = [jnp.where(i1 == e, -jnp.inf, inner[e]) for e in range(EXPERTS_PER_GROUP)]
    v2, i2 = _first_argmax(masked)
    e21 = jnp.exp(v2 - v1)
    w1 = p_grp / (1.0 + e21)
    w2 = p_grp * e21 / (1.0 + e21)
    first_low = i1 < i2
    lo = jnp.where(first_low, i1, i2)
    hi = jnp.where(first_low, i2, i1)
    w_lo = jnp.where(first_low, w1, w2)
    w_hi = jnp.where(first_low, w2, w1)
    off = jnp.where(lo == 0, 0, jnp.where(lo == 1, 3, 5))
    cls = N_PAIRS * grp + off + hi - lo - 1
    return cls.astype(F32), w_lo, w_hi


def _merge_kernel(xc_ref, xl_ref, y_ref, yb_ref, gate_ref, mod_ref, g2_ref, wglu_ref, bglu_ref,
                  wbra_ref, wbrb_ref, wout_ref, wrh_ref, wrl_ref, br_ref, x1_ref, h2_ref, route_ref,
                  rcol_ref, ya_scr, *, batch, n_ctx_tiles, tile_off):
    D = D_MODEL
    rows = batch * TILE_TOK
    tile = pl.program_id(0) + tile_off
    _from_chunk_layout(y_ref, ya_scr, batch)
    y = jnp.concatenate(
        [jnp.concatenate([ya_scr[p, b * ROW_PITCH:b * ROW_PITCH + TILE_TOK, :] for p in range(2)],
                         axis=1) for b in range(batch)], axis=0)
    z = jax.nn.gelu(y)
    glu = jax.nn.sigmoid(jnp.dot(z.astype(BF16), wglu_ref[...], preferred_element_type=F32)
                         + bglu_ref[...])
    ya = (z * glu).astype(BF16)
    gates = gate_ref[...].reshape(rows, 2 * D)
    m = gates[:, 0:D].astype(F32) * jnp.dot(ya, wbra_ref[...], preferred_element_type=F32)
    m += gates[:, D:2 * D].astype(F32) * jnp.dot(yb_ref[...].reshape(rows, D_POOL), wbrb_ref[...],
                                                 preferred_element_type=F32)
    out = jnp.dot(m.astype(BF16), wout_ref[...], preferred_element_type=F32)
    x_in = _pick(xc_ref, xl_ref, tile, n_ctx_tiles)
    x1 = x_in + mod_ref[:, :, 2 * D:3 * D] * out.reshape(batch, TILE_TOK, D)
    x1_ref[...] = x1
    yn = x1 * lax.rsqrt(jnp.mean(x1 * x1, axis=-1, keepdims=True) + EPS) * g2_ref[...]
    h2 = yn * (1.0 + mod_ref[:, :, 4 * D:5 * D]) + mod_ref[:, :, 3 * D:4 * D]
    h2_ref[...] = h2.reshape(rows, D)
    h_hi, h_lo = _split_bf16(h2.reshape(rows, D))
    logits = jnp.dot(h_hi, wrh_ref[...], preferred_element_type=F32)
    logits += jnp.dot(h_lo, wrh_ref[...], preferred_element_type=F32)
    logits += jnp.dot(h_hi, wrl_ref[...], preferred_element_type=F32)
    logits += br_ref[...]
    cls, w_lo, w_hi = _route(logits.T)
    zero = jnp.zeros_like(cls)
    route_ref[...] = jnp.concatenate([cls, w_lo, w_hi, zero, zero, zero, zero, zero], axis=0)
    cols = jnp.concatenate([cls, w_lo, w_hi, jnp.zeros((ROUTE_LANES - 3, rows), F32)], axis=0)
    rcol_ref[...] = cols.T


def _merge(x_ctx, x_lat, lat_off, y_t, yb, gates, mod_sel, norm2_g, p, layer, batch, tile_off,
           n_tiles, n_ctx_tiles):
    rows = batch * TILE_TOK
    chunk_rows = rows // S5_CHUNK
    tok = lambda cols: pl.BlockSpec((batch, TILE_TOK, cols), lambda j: (0, j + tile_off, 0))
    lay3 = lambda *blk: pl.BlockSpec((None,) + blk, lambda j: (layer, 0, 0))
    n_tok = n_tiles * TILE_TOK
    return pl.pallas_call(
        functools.partial(_merge_kernel, batch=batch, n_ctx_tiles=n_ctx_tiles, tile_off=tile_off),
        out_shape=(jax.ShapeDtypeStruct((batch, n_tok, D_MODEL), F32),
                   jax.ShapeDtypeStruct((n_tiles * rows, D_MODEL), F32),
                   jax.ShapeDtypeStruct((8, n_tiles * rows), F32),
                   jax.ShapeDtypeStruct((n_tiles * rows, ROUTE_LANES), F32)),
        grid=(n_tiles,),
        in_specs=_src_specs(D_MODEL, batch, n_ctx_tiles, tile_off, lat_off) + [
            pl.BlockSpec((N_SSM_GROUPS, chunk_rows, D_SSM), lambda j: (0, j + tile_off, 0)),
            tok(D_POOL), tok(2 * D_MODEL),
            _mod_spec(batch, layer, n_ctx_tiles, tile_off),
            lay3(1, D_MODEL), lay3(D_SSM, D_SSM), lay3(1, D_SSM), lay3(D_SSM, D_MODEL),
            lay3(D_POOL, D_MODEL), lay3(D_MODEL, D_MODEL), lay3(D_MODEL, ROUTE_LANES),
            lay3(D_MODEL, ROUTE_LANES), lay3(1, ROUTE_LANES),
        ],
        out_specs=(pl.BlockSpec((batch, TILE_TOK, D_MODEL), lambda j: (0, j, 0)),
                   pl.BlockSpec((rows, D_MODEL), lambda j: (j, 0)),
                   pl.BlockSpec((8, rows), lambda j: (0, j)),
                   pl.BlockSpec((rows, ROUTE_LANES), lambda j: (j, 0))),
        scratch_shapes=[pltpu.VMEM((2, batch * ROW_PITCH, 128), F32)],
        compiler_params=_cparams("arbitrary"),
        name="merge",
    )(x_ctx, x_lat, y_t, yb, gates, mod_sel, norm2_g, p['w_glu'], p['b_glu'], p['w_br_a'],
      p['w_br_b'], p['w_out'], p['wr_hi'], p['wr_lo'], p['b_r'])


def _plan_kernel(cls_ref, dest_ref, src_ref, meta_ref, dest_smem, fill_scr, sem, *, n_q):
    rows_q = n_q // 128
    cls = cls_ref[...]
    r_i = lax.broadcasted_iota(jnp.int32, (128, 128), 0)
    c_i = lax.broadcasted_iota(jnp.int32, (128, 128), 1)
    upper = jnp.where(r_i <= c_i, 1.0, 0.0).astype(BF16)
    r_q = lax.broadcasted_iota(jnp.int32, (rows_q, rows_q), 0)
    c_q = lax.broadcasted_iota(jnp.int32, (rows_q, rows_q), 1)
    lower = jnp.where(c_q < r_q, 1.0, 0.0).astype(BF16)
    tile_start = (lax.broadcasted_iota(jnp.int32, (1, META_LANES), 1) * EXPERT_TILE).astype(F32)
    start = jnp.zeros((1, 128), F32)
    dest = jnp.zeros((rows_q, 128), F32)
    tile_cls = jnp.zeros((1, META_LANES), F32)
    for c in range(N_CLASSES):
        m = cls == float(c)
        incl = jnp.dot(jnp.where(m, 1.0, 0.0).astype(BF16), upper, preferred_element_type=F32)
        row_tot = jnp.broadcast_to(incl[:, 127:128], (rows_q, 128))
        before = jnp.dot(lower, row_tot.astype(BF16), preferred_element_type=F32)
        dest = jnp.where(m, start + before + incl - 1.0, dest)
        total = before[rows_q - 1:rows_q, :] + row_tot[rows_q - 1:rows_q, :]
        start = start + jnp.floor((total + (EXPERT_TILE - 1.0)) * (1.0 / EXPERT_TILE)) * EXPERT_TILE
        end2 = jnp.concatenate([start, start], axis=1)
        tile_cls = tile_cls + jnp.where(tile_start >= end2, 1.0, 0.0)
    active = jnp.where(tile_start < end2, 1.0, 0.0)
    tc = jnp.minimum(tile_cls, N_CLASSES - 1.0)
    grp = sum(jnp.where(tc >= float(N_PAIRS * k), 1.0, 0.0) for k in range(1, N_EXPERT_GROUPS))
    pair = tc - N_PAIRS * grp
    p_lo = jnp.where(pair >= 3.0, 1.0, 0.0) + jnp.where(pair >= 5.0, 1.0, 0.0)
    p_hi = jnp.where(pair == 0.0, 1.0, jnp.where(pair == 1.0, 2.0, jnp.where(pair == 3.0, 2.0, 3.0)))
    lane = lax.broadcasted_iota(jnp.int32, (1, META_LANES), 1)
    fresh = jnp.where((lane == 0) | (tc != pltpu.roll(tc, 1, 1)), 1.0, 0.0)
    zero = jnp.zeros_like(tc)
    meta_ref[...] = jnp.concatenate(
        [EXPERTS_PER_GROUP * grp + p_lo, EXPERTS_PER_GROUP * grp + p_hi, active, fresh,
         zero, zero, zero, zero], axis=0).astype(jnp.int32)
    dest_ref[...] = dest.astype(jnp.int32)

    fill_scr[...] = jnp.full(fill_scr.shape, n_q, jnp.int32)
    fill = pltpu.make_async_copy(fill_scr, src_ref, sem.at[0])
    stage = pltpu.make_async_copy(dest_ref, dest_smem, sem.at[1])
    fill.start()
    stage.start()
    fill.wait()
    stage.wait()

    def invert(i, carry):
        for k in range(128):
            src_ref[dest_smem[i, k]] = i * 128 + k
        return carry

    lax.fori_loop(0, rows_q, invert, 0)


def _plan(cls_q):
    n_q = cls_q.shape[0]
    rows_q = n_q // 128
    n_tiles = n_q // EXPERT_TILE + N_CLASSES
    assert n_tiles <= META_LANES
    n_rows = n_tiles * EXPERT_TILE
    i32 = jnp.int32
    return pl.pallas_call(
        functools.partial(_plan_kernel, n_q=n_q),
        out_shape=(jax.ShapeDtypeStruct((rows_q, 128), i32), jax.ShapeDtypeStruct((n_rows,), i32),
                   jax.ShapeDtypeStruct((8, META_LANES), i32)),
        in_specs=[pl.BlockSpec(memory_space=pltpu.VMEM)],
        out_specs=(pl.BlockSpec(memory_space=pltpu.VMEM), pl.BlockSpec(memory_space=pltpu.SMEM),
                   pl.BlockSpec(memory_space=pltpu.VMEM)),
        scratch_shapes=[pltpu.SMEM((rows_q, 128), i32), pltpu.VMEM((n_rows,), i32),
                        pltpu.SemaphoreType.DMA((2,))],
        compiler_params=pltpu.CompilerParams(vmem_limit_bytes=V7X_VMEM_LIMIT),
        name="plan",
    )(cls_q.reshape(rows_q, 128))


GATHER_SLOTS = 3
EXPERT_DOTS = 6


def _expert_kernel(src_ref, elo_ref, ehi_ref, act_ref, fresh_ref, h2_hbm, w1a_ref, w1b_ref, w3a_ref,
                   w3b_ref, w2a_ref, w2b_ref, o_ref, x_buf, w1a_bf, w1b_bf, w3a_bf, w3b_bf, w2a_bf,
                   w2b_bf, sem, *, n_q):
    j = pl.program_id(0)
    last = pl.num_programs(0) - 1
    ahead = GATHER_SLOTS - 1
    slot = lax.rem(j, GATHER_SLOTS)

    def fetch(tile, into, part=0, n_parts=1):
        base = jnp.minimum(tile, last) * EXPERT_TILE
        _start_row_gather(lambda r: src_ref[base + r], h2_hbm, x_buf.at[into], sem.at[into], part,
                          n_parts, n_q - 1)

    @pl.when(j == 0)
    def _():
        for t in range(ahead):
            fetch(t, t)

    _wait_row_gather(h2_hbm, x_buf.at[slot], sem.at[slot])
    nxt, nxt_slot = j + ahead, lax.rem(j + ahead, GATHER_SLOTS)

    @pl.when(fresh_ref[j] == 1)
    def _():
        for src_w, dst_w in ((w1a_ref, w1a_bf), (w1b_ref, w1b_bf), (w3a_ref, w3a_bf),
                             (w3b_ref, w3b_bf), (w2a_ref, w2a_bf), (w2b_ref, w2b_bf)):
            dst_w[...] = src_w[...].astype(BF16)

    @pl.when(act_ref[j] == 1)
    def _():
        x = x_buf[slot].astype(BF16)
        for k, (w1, w3, w2) in enumerate(((w1a_bf, w3a_bf, w2a_bf), (w1b_bf, w3b_bf, w2b_bf))):
            fetch(nxt, nxt_slot, 3 * k, EXPERT_DOTS)
            a = jnp.dot(x, w1[...], preferred_element_type=F32)
            fetch(nxt, nxt_slot, 3 * k + 1, EXPERT_DOTS)
            b = jnp.dot(x, w3[...], preferred_element_type=F32)
            h = (a * jax.nn.sigmoid(a) * b).astype(BF16)
            fetch(nxt, nxt_slot, 3 * k + 2, EXPERT_DOTS)
            o_ref[:, k * D_MODEL:(k + 1) * D_MODEL] = jnp.dot(h, w2[...], preferred_element_type=F32)

    @pl.when(act_ref[j] == 0)
    def _():
        fetch(nxt, nxt_slot)
        o_ref[...] = jnp.zeros_like(o_ref)

    @pl.when(j == last)
    def _():
        for t in range(1, GATHER_SLOTS):
            into = lax.rem(j + t, GATHER_SLOTS)
            _wait_row_gather(h2_hbm, x_buf.at[into], sem.at[into])


def _experts(src, meta, h2, w1, w3, w2, layer):
    n_q = h2.shape[0]
    n_rows = src.shape[0]
    te = EXPERT_TILE
    n_tiles = n_rows // te
    up = lambda sel: pl.BlockSpec((None, None, D_MODEL, D_EXPERT),
                                  lambda j, s, lo, hi, act, fr: (layer, (lo, hi)[sel][j], 0, 0))
    down = lambda sel: pl.BlockSpec((None, None, D_EXPERT, D_MODEL),
                                    lambda j, s, lo, hi, act, fr: (layer, (lo, hi)[sel][j], 0, 0))
    return pl.pallas_call(
        functools.partial(_expert_kernel, n_q=n_q),
        out_shape=jax.ShapeDtypeStruct((n_rows, 2 * D_MODEL), F32),
        grid_spec=pltpu.PrefetchScalarGridSpec(
            num_scalar_prefetch=5,
            grid=(n_tiles,),
            in_specs=[pl.BlockSpec(memory_space=pl.ANY),
                      up(0), up(1), up(0), up(1), down(0), down(1)],
            out_specs=pl.BlockSpec((te, 2 * D_MODEL), lambda j, s, lo, hi, act, fr: (j, 0)),
            scratch_shapes=[pltpu.VMEM((GATHER_SLOTS, te, D_MODEL), F32)]
            + [pltpu.VMEM((D_MODEL, D_EXPERT), BF16)] * 4 + [pltpu.VMEM((D_EXPERT, D_MODEL), BF16)] * 2
            + [pltpu.SemaphoreType.DMA((GATHER_SLOTS,))],
        ),
        compiler_params=_cparams("arbitrary"),
        name="experts",
    )(src, meta[0, :n_tiles], meta[1, :n_tiles], meta[2, :n_tiles], meta[3, :n_tiles],
      h2, w1, w1, w3, w3, w2, w2)


def _combine_kernel(dest_ref, x_ref, fs_hbm, rcol_ref, mod_ref, g_ref, o_ref, f_buf, sem, *, batch,
                    final):
    j = pl.program_id(0)
    last = pl.num_programs(0) - 1
    slot = j % 2
    rows = batch * TILE_TOK
    idx_rows = rows // 128

    @pl.when(j == 0)
    def _():
        _start_row_gather(_rows_of(dest_ref, 0), fs_hbm, f_buf.at[0], sem.at[0])

    _wait_row_gather(fs_hbm, f_buf.at[slot], sem.at[slot])
    _start_row_gather(_rows_of(dest_ref, jnp.minimum(j + 1, last) * idx_rows), fs_hbm,
                      f_buf.at[1 - slot], sem.at[1 - slot])
    f = _weighted_expert_rows(rcol_ref, f_buf[slot])
    x = x_ref[...] + mod_ref[:, :, 5 * D_MODEL:6 * D_MODEL] * f.reshape(batch, TILE_TOK, D_MODEL)
    if final:
        x = x * lax.rsqrt(jnp.mean(x * x, axis=-1, keepdims=True) + EPS) * g_ref[...]
    o_ref[...] = x

    @pl.when(j == last)
    def _():
        _wait_row_gather(fs_hbm, f_buf.at[1 - slot], sem.at[1 - slot])


def _combine(dest, x1, fs, rcol, mod_sel, final_g, layer, tile_off, n_ctx_tiles, final):
    batch, n_tok, _ = x1.shape
    rows = batch * TILE_TOK
    tok = pl.BlockSpec((batch, TILE_TOK, D_MODEL), lambda j, d: (0, j, 0))
    mod_spec = _mod_spec(batch, layer, n_ctx_tiles, tile_off)
    return pl.pallas_call(
        functools.partial(_combine_kernel, batch=batch, final=final),
        out_shape=jax.ShapeDtypeStruct((batch, n_tok, D_MODEL), F32),
        grid_spec=pltpu.PrefetchScalarGridSpec(
            num_scalar_prefetch=1,
            grid=(n_tok // TILE_TOK,),
            in_specs=[tok, pl.BlockSpec(memory_space=pl.ANY),
                      pl.BlockSpec((rows, ROUTE_LANES), lambda j, d: (j, 0)),
                      pl.BlockSpec(mod_spec.block_shape, lambda j, d: mod_spec.index_map(j)),
                      pl.BlockSpec((1, D_MODEL), lambda j, d: (0, 0))],
            out_specs=tok,
            scratch_shapes=[pltpu.VMEM((2, rows, 2 * D_MODEL), F32), pltpu.SemaphoreType.DMA((2,))],
        ),
        compiler_params=_cparams("arbitrary"),
        name="combine",
    )(dest, x1, fs, rcol, mod_sel, final_g)


def kernel(x, c, ctx, c_ctx, norm1_g, norm2_g, w_mod, b_mod, w_in, lam_re, lam_im, log_dt, b_re,
           b_im, c_re, c_im, d_skip, w_glu, b_glu, w_pool, pool_scale, w_br_a, w_br_b, w_out, w_r1,
           b_r1, w_r2, b_r2, w1, w3, w2, final_g):
    batch, seq, d = x.shape
    assert d == D_MODEL and seq % (GRID_W * 8) == 0 and ctx.shape[1] == CTX_LEN and batch % 8 == 0
    n_ctx_tiles, n_lat_tiles = CTX_LEN // TILE_TOK, seq // TILE_TOK
    n_all_tiles = n_ctx_tiles + n_lat_tiles
    n_chunks = (CTX_LEN + seq) // S5_CHUNK
    mod_rows = ((batch + 1 + 7) // 8) * 8

    cc = jnp.concatenate([c, c_ctx[None, :], jnp.zeros((mod_rows - batch - 1, d), F32)], axis=0)
    mod = _modulation(cc, w_mod, b_mod)
    mod_sel = jnp.stack([jnp.broadcast_to(mod[:, batch:batch + 1], (DEPTH, batch, 6 * d)),
                         mod[:, :batch]], axis=1)[:, :, :, None, :]

    w_in_bf = w_in.astype(BF16)
    w_r = jnp.concatenate([w_r1, w_r2.transpose(0, 2, 1, 3).reshape(DEPTH, d, N_EXPERTS),
                           jnp.zeros((DEPTH, d, ROUTE_LANES - 4 - N_EXPERTS), F32)], axis=2)
    wr_hi = w_r.astype(BF16)
    wr_lo = (w_r - wr_hi.astype(F32)).astype(BF16)
    b_r = jnp.concatenate([b_r1, b_r2.reshape(DEPTH, N_EXPERTS),
                           jnp.zeros((DEPTH, ROUTE_LANES - 4 - N_EXPERTS), F32)], axis=1)
    zeros_p = jnp.zeros((DEPTH, POOL_GROUP, POOL_GROUP), F32)
    wp2 = jnp.stack([
        jnp.concatenate([jnp.concatenate([w_pool[:, 0], zeros_p], axis=2),
                         jnp.concatenate([zeros_p, w_pool[:, 1]], axis=2)], axis=1),
        jnp.concatenate([jnp.concatenate([w_pool[:, 2], zeros_p], axis=2),
                         jnp.concatenate([zeros_p, w_pool[:, 3]], axis=2)], axis=1)],
        axis=1).astype(BF16)
    params = {
        'w_glu': w_glu.astype(BF16), 'b_glu': b_glu.reshape(DEPTH, 1, D_SSM),
        'w_br_a': w_br_a.astype(BF16), 'w_br_b': w_br_b.astype(BF16), 'w_out': w_out.astype(BF16),
        'wr_hi': wr_hi, 'wr_lo': wr_lo, 'b_r': b_r.reshape(DEPTH, 1, ROUTE_LANES),
    }
    norm1 = norm1_g.reshape(DEPTH, 1, d)
    norm2 = norm2_g.reshape(DEPTH, 1, d)
    pscale = pool_scale.reshape(DEPTH, 1, D_POOL)
    dskip = jnp.tile(d_skip.reshape(DEPTH, N_SSM_GROUPS, 1, SSM_GROUP), (1, 1, 1, S5_CHUNK))

    s5_mats = jax.vmap(_s5_matrices)(lam_re, lam_im, log_dt, b_re, b_im, c_re, c_im)
    x_ctx, x_lat, lat_off = ctx, x, 0
    moe = None
    for l in range(DEPTH):
        last = l == DEPTH - 1
        outs = _inproj(x_ctx, x_lat, lat_off, mod_sel, norm1, w_in_bf, l, batch, n_ctx_tiles,
                       n_all_tiles, moe)
        if moe is not None:
            x_ctx = x_lat = outs[0]
            outs = outs[1:]
        u_t, ub, gates = outs
        y_t = _s5(u_t, s5_mats[0][l], s5_mats[1][l], s5_mats[2][l], dskip[l], batch, n_chunks,
                  CTX_LEN // S5_CHUNK)
        yb = _pool(ub, wp2, pscale, l, not last)

        tile_off = n_ctx_tiles if last else 0
        n_tiles = n_lat_tiles if last else n_all_tiles
        x1, h2, route, rcol = _merge(x_ctx, x_lat, lat_off, y_t, yb, gates, mod_sel, norm2, params, l,
                                     batch, tile_off, n_tiles, n_ctx_tiles)
        dest, src, meta = _plan(route[0])
        fs = _experts(src, meta, h2, w1, w3, w2, l)
        if last:
            return _combine(dest, x1, fs, rcol, mod_sel, final_g.reshape(1, d), l, tile_off,
                            n_ctx_tiles, True)
        x_ctx, x_lat, lat_off, moe = x1, x1, n_ctx_tiles, (dest, fs, rcol)
```

```python
import functools

import numpy as np
import jax
import jax.numpy as jnp
from jax import lax
from jax.experimental import pallas as pl
from jax.experimental.pallas import tpu as pltpu

F32 = jnp.float32
BF16 = jnp.bfloat16

D_MODEL = 1024
DEPTH = 2
GRID_W = 64
CTX_LEN = 256
EPS = 1e-6

D_SSM = 256
SSM_GROUP = 16
N_SSM_GROUPS = 16
SSM_STATE = 64
S5_CHUNK = 16

D_POOL = 768
POOL_WINDOWS = (2, 4, 8, 16)
POOL_GROUP = 192
POOL_SEG = (0, 128, 384, 512)
GATE_OFF = D_SSM + D_POOL
D_IN = D_SSM + D_POOL + 2 * D_MODEL

N_EXPERT_GROUPS = 4
EXPERTS_PER_GROUP = 4
N_EXPERTS = 16
D_EXPERT = 512
N_PAIRS = 6
N_CLASSES = N_EXPERT_GROUPS * N_PAIRS
ROUTE_LANES = 128

TILE_TOK = 32
ROW_PITCH = TILE_TOK + 4
POOL_BLOCK = 256
EXPERT_TILE = 256
META_LANES = 256
V7X_VMEM_LIMIT = 56 * 1024 * 1024


def _cparams(*sem):
    return pltpu.CompilerParams(dimension_semantics=sem, vmem_limit_bytes=V7X_VMEM_LIMIT)


def _split_bf16(v):
    hi = v.astype(BF16)
    lo = (v - hi.astype(F32)).astype(BF16)
    return hi, lo


def _mod_kernel(c_ref, w_ref, b_ref, o_ref):
    c = c_ref[...]
    a = c * jax.nn.sigmoid(c)
    a_hi, a_lo = _split_bf16(a)
    w_hi, w_lo = _split_bf16(w_ref[...])
    acc = jnp.dot(a_hi, w_hi, preferred_element_type=F32)
    acc += jnp.dot(a_lo, w_hi, preferred_element_type=F32)
    acc += jnp.dot(a_hi, w_lo, preferred_element_type=F32)
    o_ref[...] = acc + b_ref[...]


def _modulation(cc, w_mod, b_mod):
    rows = cc.shape[0]
    nblk = 4
    cols = 6 * D_MODEL // nblk
    return pl.pallas_call(
        _mod_kernel,
        out_shape=jax.ShapeDtypeStruct((DEPTH, rows, 6 * D_MODEL), F32),
        grid=(DEPTH, nblk),
        in_specs=[
            pl.BlockSpec((rows, D_MODEL), lambda l, j: (0, 0)),
            pl.BlockSpec((None, D_MODEL, cols), lambda l, j: (l, 0, j)),
            pl.BlockSpec((None, 1, cols), lambda l, j: (l, 0, j)),
        ],
        out_specs=pl.BlockSpec((None, rows, cols), lambda l, j: (l, 0, j)),
        compiler_params=_cparams("arbitrary", "arbitrary"),
        name="modulation",
    )(cc, w_mod, b_mod.reshape(DEPTH, 1, 6 * D_MODEL))


def _pick(ctx_ref, lat_ref, tile, n_ctx_tiles):
    return jnp.where(tile < n_ctx_tiles, ctx_ref[...], lat_ref[...])


def _src_specs(cols, batch, n_ctx_tiles, tile_off, lat_off):
    blk = (batch, TILE_TOK, cols)
    return [pl.BlockSpec(blk, lambda j, *_: (0, jnp.minimum(j + tile_off, n_ctx_tiles - 1), 0)),
            pl.BlockSpec(blk, lambda j, *_: (0, jnp.maximum(j + tile_off - n_ctx_tiles, 0) + lat_off,
                                             0))]


def _mod_spec(batch, layer, n_ctx_tiles, tile_off):
    return pl.BlockSpec((None, None, batch, 1, 6 * D_MODEL),
                        lambda j, *_: (layer, (j + tile_off >= n_ctx_tiles).astype(jnp.int32), 0, 0, 0))


def _lane_group(batch):
    return lax.broadcasted_iota(jnp.int32, (batch, 128), 1) // SSM_GROUP


def _to_chunk_layout(ua_scr, u_ref, batch):
    blk = _lane_group(batch)
    for cl in range(TILE_TOK // S5_CHUNK):
        for g in range(N_SSM_GROUPS):
            for q in range(2):
                acc = None
                for s in range(8 * q, 8 * q + 8):
                    src = ua_scr[g // 8, pl.ds(cl * S5_CHUNK + s, batch, stride=ROW_PITCH), :]
                    shift = ((s - g) % 8) * SSM_GROUP
                    v = pltpu.roll(src, shift, 1) if shift else src
                    acc = v if acc is None else jnp.where(blk == s % 8, v, acc)
                u_ref[g, cl * batch:(cl + 1) * batch, 128 * q:128 * q + 128] = acc


def _from_chunk_layout(y_ref, ya_scr, batch):
    blk = _lane_group(batch)
    for cl in range(TILE_TOK // S5_CHUNK):
        for s in range(S5_CHUNK):
            for p in range(2):
                acc = None
                for g in range(8 * p, 8 * p + 8):
                    src = y_ref[g, cl * batch:(cl + 1) * batch, 128 * (s // 8):128 * (s // 8) + 128]
                    shift = ((g - s) % 8) * SSM_GROUP
                    v = pltpu.roll(src, shift, 1) if shift else src
                    acc = v if acc is None else jnp.where(blk == g % 8, v, acc)
                ya_scr[p, pl.ds(cl * S5_CHUNK + s, batch, stride=ROW_PITCH), :] = acc


def _start_row_gather(idx_at, src_hbm, dst_ref, sem, part=0, n_parts=1, max_row=None):
    n = dst_ref.shape[0]
    for r in range(part * n // n_parts, (part + 1) * n // n_parts):
        row = idx_at(r)
        if max_row is not None:
            row = jnp.minimum(row, max_row)
        pltpu.make_async_copy(src_hbm.at[pl.ds(row, 1)], dst_ref.at[pl.ds(r, 1)],
                              sem).start(priority=r % 2)


def _rows_of(idx_ref, row0):
    return lambda r: idx_ref[row0 + r // 128, r % 128]


def _wait_row_gather(src_hbm, dst_ref, sem):
    pltpu.make_async_copy(src_hbm.at[pl.ds(0, dst_ref.shape[0])], dst_ref, sem).wait()


def _weighted_expert_rows(rcol_ref, f_rows):
    return (rcol_ref[:, 1:2] * f_rows[:, 0:D_MODEL] + rcol_ref[:, 2:3] * f_rows[:, D_MODEL:2 * D_MODEL])


INPROJ_DOTS = 4


def _inproj_kernel(*refs, batch, n_ctx_tiles, moe_in):
    if moe_in:
        (dest_ref, xc_ref, xl_ref, fs_hbm, rcol_ref, modp_ref, mod_ref, g_ref, w_ref,
         x2_ref, u_ref, ub_ref, gate_ref, h_scr, ua_scr, f_buf, sem) = refs
    else:
        xc_ref, xl_ref, mod_ref, g_ref, w_ref, u_ref, ub_ref, gate_ref, h_scr, ua_scr = refs
    rows = batch * TILE_TOK
    j = pl.program_id(0)
    last = pl.num_programs(0) - 1
    slot = j % 2
    x = _pick(xc_ref, xl_ref, j, n_ctx_tiles)

    def fetch(part):
        if moe_in:
            _start_row_gather(_rows_of(dest_ref, jnp.minimum(j + 1, last) * (rows // 128)), fs_hbm,
                              f_buf.at[1 - slot], sem.at[1 - slot], part, INPROJ_DOTS)

    if moe_in:
        @pl.when(j == 0)
        def _():
            _start_row_gather(_rows_of(dest_ref, 0), fs_hbm, f_buf.at[0], sem.at[0])

        _wait_row_gather(fs_hbm, f_buf.at[slot], sem.at[slot])
        f = _weighted_expert_rows(rcol_ref, f_buf[slot])
        x = x + modp_ref[:, :, 5 * D_MODEL:6 * D_MODEL] * f.reshape(batch, TILE_TOK, D_MODEL)
        x2_ref[...] = x
    y = x * lax.rsqrt(jnp.mean(x * x, axis=-1, keepdims=True) + EPS) * g_ref[...]
    shift = mod_ref[:, :, 0:D_MODEL]
    scale = mod_ref[:, :, D_MODEL:2 * D_MODEL]
    h_scr[...] = (y * (1.0 + scale) + shift).reshape(rows, D_MODEL).astype(BF16)
    h = h_scr[...]
    fetch(0)
    ua = jnp.dot(h, w_ref[:, 0:D_SSM], preferred_element_type=F32)
    for b in range(batch):
        for p in range(2):
            ua_scr[p, b * ROW_PITCH:b * ROW_PITCH + TILE_TOK, :] = (
                ua[b * TILE_TOK:(b + 1) * TILE_TOK, 128 * p:128 * p + 128])
    fetch(1)
    ub = jnp.dot(h, w_ref[:, D_SSM:GATE_OFF], preferred_element_type=F32)
    ub_ref[...] = ub.reshape(batch, TILE_TOK, D_POOL)
    for k in range(2):
        lo = GATE_OFF + k * D_MODEL
        fetch(2 + k)
        g = jnp.dot(h, w_ref[:, lo:lo + D_MODEL], preferred_element_type=F32)
        gate_ref[:, :, k * D_MODEL:(k + 1) * D_MODEL] = (
            jax.nn.sigmoid(g).astype(BF16).reshape(batch, TILE_TOK, D_MODEL))
    _to_chunk_layout(ua_scr, u_ref, batch)
    if moe_in:
        @pl.when(j == last)
        def _():
            _wait_row_gather(fs_hbm, f_buf.at[1 - slot], sem.at[1 - slot])


def _inproj(x_ctx, x_lat, lat_off, mod_sel, norm_g, w_in_bf, layer, batch, n_ctx_tiles, n_tiles,
            moe=None):
    rows = batch * TILE_TOK
    n_tok = n_tiles * TILE_TOK
    chunk_rows = batch * TILE_TOK // S5_CHUNK
    tok = lambda cols: pl.BlockSpec((batch, TILE_TOK, cols), lambda j, *_: (0, j, 0))
    out_shape = [jax.ShapeDtypeStruct((N_SSM_GROUPS, n_tiles * chunk_rows, D_SSM), F32),
                 jax.ShapeDtypeStruct((batch, n_tok, D_POOL), F32),
                 jax.ShapeDtypeStruct((batch, n_tok, 2 * D_MODEL), BF16)]
    out_specs = [pl.BlockSpec((N_SSM_GROUPS, chunk_rows, D_SSM), lambda j, *_: (0, j, 0)),
                 tok(D_POOL), tok(2 * D_MODEL)]
    in_specs = _src_specs(D_MODEL, batch, n_ctx_tiles, 0, lat_off)
    args = [x_ctx, x_lat]
    scratch = [pltpu.VMEM((rows, D_MODEL), BF16), pltpu.VMEM((2, batch * ROW_PITCH, 128), F32)]
    prefetch = []
    if moe is not None:
        dest, fs, rcol = moe
        prefetch = [dest]
        in_specs += [pl.BlockSpec(memory_space=pl.ANY),
                     pl.BlockSpec((rows, ROUTE_LANES), lambda j, *_: (j, 0)),
                     _mod_spec(batch, layer - 1, n_ctx_tiles, 0)]
        args += [fs, rcol, mod_sel]
        out_shape.insert(0, jax.ShapeDtypeStruct((batch, n_tok, D_MODEL), F32))
        out_specs.insert(0, tok(D_MODEL))
        scratch += [pltpu.VMEM((2, rows, 2 * D_MODEL), F32), pltpu.SemaphoreType.DMA((2,))]
    in_specs += [_mod_spec(batch, layer, n_ctx_tiles, 0),
                 pl.BlockSpec((None, 1, D_MODEL), lambda j, *_: (layer, 0, 0)),
                 pl.BlockSpec((None, D_MODEL, D_IN), lambda j, *_: (layer, 0, 0))]
    args += [mod_sel, norm_g, w_in_bf]
    return pl.pallas_call(
        functools.partial(_inproj_kernel, batch=batch, n_ctx_tiles=n_ctx_tiles, moe_in=moe is not None),
        out_shape=tuple(out_shape),
        grid_spec=pltpu.PrefetchScalarGridSpec(
            num_scalar_prefetch=len(prefetch), grid=(n_tiles,), in_specs=in_specs,
            out_specs=tuple(out_specs), scratch_shapes=scratch),
        compiler_params=_cparams("arbitrary"),
        name="inproj",
    )(*prefetch, *args)


def _s5_matrices(lam_re, lam_im, log_dt, b_re, b_im, c_re, c_im):
    L, G, N, H = S5_CHUNK, N_SSM_GROUPS, SSM_STATE, SSM_GROUP
    lr, li = lam_re.astype(F32), lam_im.astype(F32)
    dt = jnp.exp(log_dt.astype(F32))[..., None]
    zr, zi = lr * dt, li * dt
    k = jnp.arange(L + 1, dtype=F32)[:, None, None, None]
    pm = jnp.exp(zr[None] * k)
    pr, pi = pm * jnp.cos(zi[None] * k), pm * jnp.sin(zi[None] * k)
    nr, ni = pr[1] - 1.0, pi[1]
    den = lr * lr + li * li
    fr = (nr * lr + ni * li) / den
    fi = (ni * lr - nr * li) / den
    br, bi = b_re.astype(F32), b_im.astype(F32)
    bbr = fr[..., None] * br - fi[..., None] * bi
    bbi = fr[..., None] * bi + fi[..., None] * br
    cr, ci = c_re.astype(F32), c_im.astype(F32)
    cpr = cr[None] * pr[:, :, :, None, :] - ci[None] * pi[:, :, :, None, :]
    cpi = cr[None] * pi[:, :, :, None, :] + ci[None] * pr[:, :, :, None, :]
    kern = (jnp.einsum('kdgan,dgnh->kdgah', cpr, bbr)
            - jnp.einsum('kdgan,dgnh->kdgah', cpi, bbi))
    s_idx = np.arange(L)[:, None]
    t_idx = np.arange(L)[None, :]
    lag_f = np.clip(t_idx - s_idx, 0, L - 1)
    lag_b = np.clip(s_idx - t_idx, 0, L - 1)
    mf = jnp.where((t_idx >= s_idx)[:, :, None, None, None], kern[:, 0][lag_f], 0.0)
    mb = jnp.where((s_idx >= t_idx)[:, :, None, None, None], kern[:, 1][lag_b], 0.0)
    m = (mf + mb).transpose(2, 0, 4, 1, 3).reshape(G, L * H, L * H)
    powers = (L - 1 - np.arange(L), np.arange(L))
    inj = []
    for d in range(2):
        p_r, p_i = pr[powers[d], d], pi[powers[d], d]
        wr = p_r[..., None] * bbr[d][None] - p_i[..., None] * bbi[d][None]
        wi = p_r[..., None] * bbi[d][None] + p_i[..., None] * bbr[d][None]
        inj.append((wr, wi))
    w_inj = jnp.concatenate([inj[0][0], inj[1][0], inj[0][1], inj[1][1]], axis=2)
    mcat = jnp.concatenate([m, w_inj.transpose(1, 0, 3, 2).reshape(G, L * H, 4 * N)], axis=2)
    rd = (np.arange(L) + 1, L - np.arange(L))
    w_rd = jnp.concatenate([cpr[rd[0], 0], cpr[rd[1], 1], -cpi[rd[0], 0], -cpi[rd[1], 1]],
                           axis=3)
    wy = w_rd.transpose(1, 3, 0, 2).reshape(G, 4 * N, L * H)
    a_r = jnp.concatenate([pr[L][0], pr[L][1]], axis=-1)
    a_i = jnp.concatenate([pi[L][0], pi[L][1]], axis=-1)
    coef = jnp.stack([a_r, a_i] + [jnp.zeros_like(a_r)] * 6, axis=1)
    return mcat.astype(BF16), wy.astype(BF16), coef


S5_GROUPS_PER_STEP = 2


def _s5_kernel(u_ref, mcat_ref, wy_ref, coef_ref, dskip_ref, y_ref, s_scr, h_scr, *, batch,
               n_chunks, n_ctx_chunks):
    n = SSM_STATE
    gps = S5_GROUPS_PER_STEP
    rb = S5_CHUNK * batch
    n_rb = n_chunks // S5_CHUNK

    def inject(i, c):
        r0 = pl.multiple_of(i * rb, rb)
        for g in range(gps):
            u = u_ref[g, pl.ds(r0, rb), :]
            ub = u.astype(BF16)
            y_ref[g, pl.ds(r0, rb), :] = (
                jnp.dot(ub, mcat_ref[g, :, 0:D_SSM], preferred_element_type=F32) + dskip_ref[g] * u)
            s_scr[g, pl.ds(r0, rb), :] = jnp.dot(ub, mcat_ref[g, :, D_SSM:2 * D_SSM],
                                                 preferred_element_type=F32)
        return c

    lax.fori_loop(0, n_rb, inject, 0)
    a_r = [coef_ref[g, 0:1, :] for g in range(gps)]
    a_i = [coef_ref[g, 1:2, :] for g in range(gps)]
    fwd_lanes = lax.broadcasted_iota(jnp.int32, (batch, 2 * n), 1) < n

    def step(k, carry):
        cb = jnp.where(k < n_ctx_chunks, n_ctx_chunks - 1 - k, n_chunks - 1 + n_ctx_chunks - k)
        rf = pl.multiple_of(k * batch, batch)
        rk = pl.multiple_of(cb * batch, batch)
        out = []
        for g in range(gps):
            h_re, h_im = carry[2 * g], carry[2 * g + 1]
            h_scr[g, pl.ds(rf, batch), 0:n] = h_re[:, 0:n]
            h_scr[g, pl.ds(rk, batch), n:2 * n] = h_re[:, n:2 * n]
            h_scr[g, pl.ds(rf, batch), 2 * n:3 * n] = h_im[:, 0:n]
            h_scr[g, pl.ds(rk, batch), 3 * n:4 * n] = h_im[:, n:2 * n]
            s_re = jnp.where(fwd_lanes, s_scr[g, pl.ds(rf, batch), 0:2 * n],
                             s_scr[g, pl.ds(rk, batch), 0:2 * n])
            s_im = jnp.where(fwd_lanes, s_scr[g, pl.ds(rf, batch), 2 * n:4 * n],
                             s_scr[g, pl.ds(rk, batch), 2 * n:4 * n])
            out += [a_r[g] * h_re - a_i[g] * h_im + s_re, a_r[g] * h_im + a_i[g] * h_re + s_im]
        return tuple(out)

    zero = jnp.zeros((batch, 2 * n), F32)
    lax.fori_loop(0, n_chunks, step, (zero,) * (2 * gps), unroll=4)

    def readout(i, c):
        r0 = pl.multiple_of(i * rb, rb)
        for g in range(gps):
            y_ref[g, pl.ds(r0, rb), :] += jnp.dot(h_scr[g, pl.ds(r0, rb), :].astype(BF16), wy_ref[g],
                                                  preferred_element_type=F32)
        return c

    lax.fori_loop(0, n_rb, readout, 0)


def _s5(u_t, mcat, wy, coef, dskip, batch, n_chunks, n_ctx_chunks):
    G = N_SSM_GROUPS
    gps = S5_GROUPS_PER_STEP
    rows = u_t.shape[1]
    kern = functools.partial(_s5_kernel, batch=batch, n_chunks=n_chunks, n_ctx_chunks=n_ctx_chunks)
    per_group = lambda *blk: pl.BlockSpec((gps,) + blk, lambda g: (g, 0, 0))
    return pl.pallas_call(
        kern,
        out_shape=jax.ShapeDtypeStruct((G, rows, D_SSM), F32),
        grid=(G // gps,),
        in_specs=[per_group(rows, D_SSM), per_group(D_SSM, 2 * D_SSM), per_group(D_SSM, D_SSM),
                  per_group(8, 2 * SSM_STATE), per_group(1, D_SSM)],
        out_specs=per_group(rows, D_SSM),
        scratch_shapes=[pltpu.VMEM((gps, rows, D_SSM), F32), pltpu.VMEM((gps, rows, D_SSM), F32)],
        compiler_params=_cparams("arbitrary"),
        name="s5",
    )(u_t, mcat, wy, coef, dskip)


def _window(w):
    return -(w // 2), w - 1 - w // 2


def _pool_constants(rows, width):
    rpb = POOL_BLOCK // width
    pm = np.zeros((4, POOL_BLOCK, POOL_BLOCK), np.float32)
    inv = np.zeros((rows * width, 4), np.float32)
    col = np.arange(width)
    row = np.arange(rows)
    for i, w in enumerate(POOL_WINDOWS):
        lo, hi = _window(w)
        c0, c1 = np.clip(col + lo, 0, width - 1), np.clip(col + hi, 0, width - 1)
        r0, r1 = np.clip(row + lo, 0, rows - 1), np.clip(row + hi, 0, rows - 1)
        band = ((col[None, :] >= c0[:, None]) & (col[None, :] <= c1[:, None])).astype(np.float32)
        for r in range(rpb):
            pm[i, r * width:(r + 1) * width, r * width:(r + 1) * width] = band
        cnt = (r1 - r0 + 1)[:, None] * (c1 - c0 + 1)[None, :]
        inv[:, i] = (1.0 / cnt).reshape(-1)
    return jnp.asarray(pm, BF16), jnp.asarray(inv, F32)


def _pool_segment(u_ref, o_ref, cs_scr, pm_ref, inv_ref, wp_ref, ps_ref, tok0, rows, width):
    n_tok = rows * width
    n_blk = n_tok // POOL_BLOCK
    pad = 8 * width if rows > 1 else 0
    if rows > 1:
        zeros = jnp.zeros((pad, 2 * 128), F32)
        for i in range(4):
            cs_scr[i, 0:pad, :] = zeros
            cs_scr[i, pad + n_tok:pad + n_tok + pad, :] = zeros
    for b in range(n_blk):
        t0 = b * POOL_BLOCK
        for i in range(4):
            seg = POOL_SEG[i]
            xb = u_ref[tok0 + t0:tok0 + t0 + POOL_BLOCK, seg:seg + 256].astype(BF16)
            cs_scr[i, pad + t0:pad + t0 + POOL_BLOCK, :] = jnp.dot(
                pm_ref[i], xb, preferred_element_type=F32)
    lane = lax.broadcasted_iota(jnp.int32, (POOL_BLOCK, 128), 1)
    low_half = lane < 64
    for b in range(n_blk):
        t0 = b * POOL_BLOCK
        win = []
        for i, w in enumerate(POOL_WINDOWS):
            lo, hi = _window(w) if rows > 1 else (0, 0)
            acc = None
            for k in range(lo, hi + 1):
                base = pad + t0 + k * width
                piece = cs_scr[i, base:base + POOL_BLOCK, :]
                acc = piece if acc is None else acc + piece
            win.append(acc * inv_ref[t0:t0 + POOL_BLOCK, i:i + 1])
        x = u_ref[tok0 + t0:tok0 + t0 + POOL_BLOCK, :]
        pooled = jnp.concatenate([
            win[0][:, 0:128],
            jnp.where(low_half, win[0][:, 128:256], win[1][:, 0:128]),
            win[1][:, 128:256],
            win[2][:, 0:128],
            jnp.where(low_half, win[2][:, 128:256], win[3][:, 0:128]),
            win[3][:, 128:256]], axis=1)
        d = (pooled - x).astype(BF16)
        y0 = jnp.dot(d[:, 0:384], wp_ref[0], preferred_element_type=F32)
        y1 = jnp.dot(d[:, 384:768], wp_ref[1], preferred_element_type=F32)
        y = jnp.concatenate([y0, y1], axis=1) * ps_ref[...]
        o_ref[tok0 + t0:tok0 + t0 + POOL_BLOCK, :] = y.astype(BF16)


def _pool_kernel(u_ref, pmc_ref, invc_ref, pml_ref, invl_ref, wp_ref, ps_ref, o_ref, cs_scr, *,
                 lat_rows, with_ctx):
    if with_ctx:
        _pool_segment(u_ref, o_ref, cs_scr, pmc_ref, invc_ref, wp_ref, ps_ref, 0, 1, CTX_LEN)
    else:
        o_ref[0:CTX_LEN, :] = jnp.zeros((CTX_LEN, D_POOL), BF16)
    _pool_segment(u_ref, o_ref, cs_scr, pml_ref, invl_ref, wp_ref, ps_ref, CTX_LEN, lat_rows, GRID_W)


def _pool(ub, wp2, pscale, layer, with_ctx):
    batch, n_tok, _ = ub.shape
    lat_rows = (n_tok - CTX_LEN) // GRID_W
    pmc, invc = _pool_constants(1, CTX_LEN)
    pml, invl = _pool_constants(lat_rows, GRID_W)
    const = lambda *blk: pl.BlockSpec(blk, lambda b: (0,) * len(blk))
    return pl.pallas_call(
        functools.partial(_pool_kernel, lat_rows=lat_rows, with_ctx=with_ctx),
        out_shape=jax.ShapeDtypeStruct((batch, n_tok, D_POOL), BF16),
        grid=(batch,),
        in_specs=[
            pl.BlockSpec((None, n_tok, D_POOL), lambda b: (b, 0, 0)),
            const(4, POOL_BLOCK, POOL_BLOCK), const(CTX_LEN, 4),
            const(4, POOL_BLOCK, POOL_BLOCK), const(lat_rows * GRID_W, 4),
            pl.BlockSpec((None, 2, 384, 384), lambda b: (layer, 0, 0, 0)),
            pl.BlockSpec((None, 1, D_POOL), lambda b: (layer, 0, 0)),
        ],
        out_specs=pl.BlockSpec((None, n_tok, D_POOL), lambda b: (b, 0, 0)),
        scratch_shapes=[pltpu.VMEM((4, lat_rows * GRID_W + 16 * GRID_W, 256), F32)],
        compiler_params=_cparams("arbitrary"),
        name="pool",
    )(ub, pmc, invc, pml, invl, wp2, pscale)


def _first_argmax(rows_):
    best, idx = rows_[0], jnp.zeros_like(rows_[0], dtype=jnp.int32)
    for k in range(1, len(rows_)):
        take = rows_[k] > best
        idx = jnp.where(take, k, idx)
        best = jnp.where(take, rows_[k], best)
    return best, idx


def _route(lt):
    g_rows = [lt[k:k + 1, :] for k in range(N_EXPERT_GROUPS)]
    best, grp = _first_argmax(g_rows)
    denom = sum(jnp.exp(r - best) for r in g_rows)
    p_grp = 1.0 / denom
    inner = []
    for e in range(EXPERTS_PER_GROUP):
        acc = jnp.zeros_like(best)
        for g in range(N_EXPERT_GROUPS):
            r = 4 + 4 * g + e
            acc = jnp.where(grp == g, lt[r:r + 1, :], acc)
        inner.append(acc)
    v1, i1 = _first_argmax(inner)
    masked = [jnp.where(i1 == e, -jnp.inf, inner[e]) for e in range(EXPERTS_PER_GROUP)]
    v2, i2 = _first_argmax(masked)
    e21 = jnp.exp(v2 - v1)
    w1 = p_grp / (1.0 + e21)
    w2 = p_grp * e21 / (1.0 + e21)
    first_low = i1 < i2
    lo = jnp.where(first_low, i1, i2)
    hi = jnp.where(first_low, i2, i1)
    w_lo = jnp.where(first_low, w1, w2)
    w_hi = jnp.where(first_low, w2, w1)
    off = jnp.where(lo == 0, 0, jnp.where(lo == 1, 3, 5))
    cls = N_PAIRS * grp + off + hi - lo - 1
    return cls.astype(F32), w_lo, w_hi


def _merge_kernel(xc_ref, xl_ref, y_ref, yb_ref, gate_ref, mod_ref, g2_ref, wglu_ref, bglu_ref,
                  wbra_ref, wbrb_ref, wout_ref, wrh_ref, wrl_ref, br_ref, x1_ref, h2_ref, route_ref,
                  rcol_ref, ya_scr, *, batch, n_ctx_tiles, tile_off):
    D = D_MODEL
    rows = batch * TILE_TOK
    tile = pl.program_id(0) + tile_off
    _from_chunk_layout(y_ref, ya_scr, batch)
    y = jnp.concatenate(
        [jnp.concatenate([ya_scr[p, b * ROW_PITCH:b * ROW_PITCH + TILE_TOK, :] for p in range(2)],
                         axis=1) for b in range(batch)], axis=0)
    z = jax.nn.gelu(y)
    glu = jax.nn.sigmoid(jnp.dot(z.astype(BF16), wglu_ref[...], preferred_element_type=F32)
                         + bglu_ref[...])
    ya = (z * glu).astype(BF16)
    gates = gate_ref[...].reshape(rows, 2 * D)
    m = gates[:, 0:D].astype(F32) * jnp.dot(ya, wbra_ref[...], preferred_element_type=F32)
    m += gates[:, D:2 * D].astype(F32) * jnp.dot(yb_ref[...].reshape(rows, D_POOL), wbrb_ref[...],
                                                 preferred_element_type=F32)
    out = jnp.dot(m.astype(BF16), wout_ref[...], preferred_element_type=F32)
    x_in = _pick(xc_ref, xl_ref, tile, n_ctx_tiles)
    x1 = x_in + mod_ref[:, :, 2 * D:3 * D] * out.reshape(batch, TILE_TOK, D)
    x1_ref[...] = x1
    yn = x1 * lax.rsqrt(jnp.mean(x1 * x1, axis=-1, keepdims=True) + EPS) * g2_ref[...]
    h2 = yn * (1.0 + mod_ref[:, :, 4 * D:5 * D]) + mod_ref[:, :, 3 * D:4 * D]
    h2_ref[...] = h2.reshape(rows, D)
    h_hi, h_lo = _split_bf16(h2.reshape(rows, D))
    logits = jnp.dot(h_hi, wrh_ref[...], preferred_element_type=F32)
    logits += jnp.dot(h_lo, wrh_ref[...], preferred_element_type=F32)
    logits += jnp.dot(h_hi, wrl_ref[...], preferred_element_type=F32)
    logits += br_ref[...]
    cls, w_lo, w_hi = _route(logits.T)
    zero = jnp.zeros_like(cls)
    route_ref[...] = jnp.concatenate([cls, w_lo, w_hi, zero, zero, zero, zero, zero], axis=0)
    cols = jnp.concatenate([cls, w_lo, w_hi, jnp.zeros((ROUTE_LANES - 3, rows), F32)], axis=0)
    rcol_ref[...] = cols.T


def _merge(x_ctx, x_lat, lat_off, y_t, yb, gates, mod_sel, norm2_g, p, layer, batch, tile_off,
           n_tiles, n_ctx_tiles):
    rows = batch * TILE_TOK
    chunk_rows = rows // S5_CHUNK
    tok = lambda cols: pl.BlockSpec((batch, TILE_TOK, cols), lambda j: (0, j + tile_off, 0))
    lay3 = lambda *blk: pl.BlockSpec((None,) + blk, lambda j: (layer, 0, 0))
    n_tok = n_tiles * TILE_TOK
    return pl.pallas_call(
        functools.partial(_merge_kernel, batch=batch, n_ctx_tiles=n_ctx_tiles, tile_off=tile_off),
        out_shape=(jax.ShapeDtypeStruct((batch, n_tok, D_MODEL), F32),
                   jax.ShapeDtypeStruct((n_tiles * rows, D_MODEL), F32),
                   jax.ShapeDtypeStruct((8, n_tiles * rows), F32),
                   jax.ShapeDtypeStruct((n_tiles * rows, ROUTE_LANES), F32)),
        grid=(n_tiles,),
        in_specs=_src_specs(D_MODEL, batch, n_ctx_tiles, tile_off, lat_off) + [
            pl.BlockSpec((N_SSM_GROUPS, chunk_rows, D_SSM), lambda j: (0, j + tile_off, 0)),
            tok(D_POOL), tok(2 * D_MODEL),
            _mod_spec(batch, layer, n_ctx_tiles, tile_off),
            lay3(1, D_MODEL), lay3(D_SSM, D_SSM), lay3(1, D_SSM), lay3(D_SSM, D_MODEL),
            lay3(D_POOL, D_MODEL), lay3(D_MODEL, D_MODEL), lay3(D_MODEL, ROUTE_LANES),
            lay3(D_MODEL, ROUTE_LANES), lay3(1, ROUTE_LANES),
        ],
        out_specs=(pl.BlockSpec((batch, TILE_TOK, D_MODEL), lambda j: (0, j, 0)),
                   pl.BlockSpec((rows, D_MODEL), lambda j: (j, 0)),
                   pl.BlockSpec((8, rows), lambda j: (0, j)),
                   pl.BlockSpec((rows, ROUTE_LANES), lambda j: (j, 0))),
        scratch_shapes=[pltpu.VMEM((2, batch * ROW_PITCH, 128), F32)],
        compiler_params=_cparams("arbitrary"),
        name="merge",
    )(x_ctx, x_lat, y_t, yb, gates, mod_sel, norm2_g, p['w_glu'], p['b_glu'], p['w_br_a'],
      p['w_br_b'], p['w_out'], p['wr_hi'], p['wr_lo'], p['b_r'])


def _plan_kernel(cls_ref, dest_ref, src_ref, meta_ref, dest_smem, fill_scr, sem, *, n_q):
    rows_q = n_q // 128
    cls = cls_ref[...]
    r_i = lax.broadcasted_iota(jnp.int32, (128, 128), 0)
    c_i = lax.broadcasted_iota(jnp.int32, (128, 128), 1)
    upper = jnp.where(r_i <= c_i, 1.0, 0.0).astype(BF16)
    r_q = lax.broadcasted_iota(jnp.int32, (rows_q, rows_q), 0)
    c_q = lax.broadcasted_iota(jnp.int32, (rows_q, rows_q), 1)
    lower = jnp.where(c_q < r_q, 1.0, 0.0).astype(BF16)
    tile_start = (lax.broadcasted_iota(jnp.int32, (1, META_LANES), 1) * EXPERT_TILE).astype(F32)
    start = jnp.zeros((1, 128), F32)
    dest = jnp.zeros((rows_q, 128), F32)
    tile_cls = jnp.zeros((1, META_LANES), F32)
    for c in range(N_CLASSES):
        m = cls == float(c)
        incl = jnp.dot(jnp.where(m, 1.0, 0.0).astype(BF16), upper, preferred_element_type=F32)
        row_tot = jnp.broadcast_to(incl[:, 127:128], (rows_q, 128))
        before = jnp.dot(lower, row_tot.astype(BF16), preferred_element_type=F32)
        dest = jnp.where(m, start + before + incl - 1.0, dest)
        total = before[rows_q - 1:rows_q, :] + row_tot[rows_q - 1:rows_q, :]
        start = start + jnp.floor((total + (EXPERT_TILE - 1.0)) * (1.0 / EXPERT_TILE)) * EXPERT_TILE
        end2 = jnp.concatenate([start, start], axis=1)
        tile_cls = tile_cls + jnp.where(tile_start >= end2, 1.0, 0.0)
    active = jnp.where(tile_start < end2, 1.0, 0.0)
    tc = jnp.minimum(tile_cls, N_CLASSES - 1.0)
    grp = sum(jnp.where(tc >= float(N_PAIRS * k), 1.0, 0.0) for k in range(1, N_EXPERT_GROUPS))
    pair = tc - N_PAIRS * grp
    p_lo = jnp.where(pair >= 3.0, 1.0, 0.0) + jnp.where(pair >= 5.0, 1.0, 0.0)
    p_hi = jnp.where(pair == 0.0, 1.0, jnp.where(pair == 1.0, 2.0, jnp.where(pair == 3.0, 2.0, 3.0)))
    lane = lax.broadcasted_iota(jnp.int32, (1, META_LANES), 1)
    fresh = jnp.where((lane == 0) | (tc != pltpu.roll(tc, 1, 1)), 1.0, 0.0)
    zero = jnp.zeros_like(tc)
    meta_ref[...] = jnp.concatenate(
        [EXPERTS_PER_GROUP * grp + p_lo, EXPERTS_PER_GROUP * grp + p_hi, active, fresh,
         zero, zero, zero, zero], axis=0).astype(jnp.int32)
    dest_ref[...] = dest.astype(jnp.int32)

    fill_scr[...] = jnp.full(fill_scr.shape, n_q, jnp.int32)
    fill = pltpu.make_async_copy(fill_scr, src_ref, sem.at[0])
    stage = pltpu.make_async_copy(dest_ref, dest_smem, sem.at[1])
    fill.start()
    stage.start()
    fill.wait()
    stage.wait()

    def invert(i, carry):
        for k in range(128):
            src_ref[dest_smem[i, k]] = i * 128 + k
        return carry

    lax.fori_loop(0, rows_q, invert, 0)


def _plan(cls_q):
    n_q = cls_q.shape[0]
    rows_q = n_q // 128
    n_tiles = n_q // EXPERT_TILE + N_CLASSES
    assert n_tiles <= META_LANES
    n_rows = n_tiles * EXPERT_TILE
    i32 = jnp.int32
    return pl.pallas_call(
        functools.partial(_plan_kernel, n_q=n_q),
        out_shape=(jax.ShapeDtypeStruct((rows_q, 128), i32), jax.ShapeDtypeStruct((n_rows,), i32),
                   jax.ShapeDtypeStruct((8, META_LANES), i32)),
        in_specs=[pl.BlockSpec(memory_space=pltpu.VMEM)],
        out_specs=(pl.BlockSpec(memory_space=pltpu.VMEM), pl.BlockSpec(memory_space=pltpu.SMEM),
                   pl.BlockSpec(memory_space=pltpu.VMEM)),
        scratch_shapes=[pltpu.SMEM((rows_q, 128), i32), pltpu.VMEM((n_rows,), i32),
                        pltpu.SemaphoreType.DMA((2,))],
        compiler_params=pltpu.CompilerParams(vmem_limit_bytes=V7X_VMEM_LIMIT),
        name="plan",
    )(cls_q.reshape(rows_q, 128))


GATHER_SLOTS = 3
EXPERT_DOTS = 6


def _expert_kernel(src_ref, elo_ref, ehi_ref, act_ref, fresh_ref, h2_hbm, w1a_ref, w1b_ref, w3a_ref,
                   w3b_ref, w2a_ref, w2b_ref, o_ref, x_buf, w1a_bf, w1b_bf, w3a_bf, w3b_bf, w2a_bf,
                   w2b_bf, sem, *, n_q):
    j = pl.program_id(0)
    last = pl.num_programs(0) - 1
    ahead = GATHER_SLOTS - 1
    slot = lax.rem(j, GATHER_SLOTS)

    def fetch(tile, into, part=0, n_parts=1):
        base = jnp.minimum(tile, last) * EXPERT_TILE
        _start_row_gather(lambda r: src_ref[base + r], h2_hbm, x_buf.at[into], sem.at[into], part,
                          n_parts, n_q - 1)

    @pl.when(j == 0)
    def _():
        for t in range(ahead):
            fetch(t, t)

    _wait_row_gather(h2_hbm, x_buf.at[slot], sem.at[slot])
    nxt, nxt_slot = j + ahead, lax.rem(j + ahead, GATHER_SLOTS)

    @pl.when(fresh_ref[j] == 1)
    def _():
        for src_w, dst_w in ((w1a_ref, w1a_bf), (w1b_ref, w1b_bf), (w3a_ref, w3a_bf),
                             (w3b_ref, w3b_bf), (w2a_ref, w2a_bf), (w2b_ref, w2b_bf)):
            dst_w[...] = src_w[...].astype(BF16)

    @pl.when(act_ref[j] == 1)
    def _():
        x = x_buf[slot].astype(BF16)
        for k, (w1, w3, w2) in enumerate(((w1a_bf, w3a_bf, w2a_bf), (w1b_bf, w3b_bf, w2b_bf))):
            fetch(nxt, nxt_slot, 3 * k, EXPERT_DOTS)
            a = jnp.dot(x, w1[...], preferred_element_type=F32)
            fetch(nxt, nxt_slot, 3 * k + 1, EXPERT_DOTS)
            b = jnp.dot(x, w3[...], preferred_element_type=F32)
            h = (a * jax.nn.sigmoid(a) * b).astype(BF16)
            fetch(nxt, nxt_slot, 3 * k + 2, EXPERT_DOTS)
            o_ref[:, k * D_MODEL:(k + 1) * D_MODEL] = jnp.dot(h, w2[...], preferred_element_type=F32)

    @pl.when(act_ref[j] == 0)
    def _():
        fetch(nxt, nxt_slot)
        o_ref[...] = jnp.zeros_like(o_ref)

    @pl.when(j == last)
    def _():
        for t in range(1, GATHER_SLOTS):
            into = lax.rem(j + t, GATHER_SLOTS)
            _wait_row_gather(h2_hbm, x_buf.at[into], sem.at[into])


def _experts(src, meta, h2, w1, w3, w2, layer):
    n_q = h2.shape[0]
    n_rows = src.shape[0]
    te = EXPERT_TILE
    n_tiles = n_rows // te
    up = lambda sel: pl.BlockSpec((None, None, D_MODEL, D_EXPERT),
                                  lambda j, s, lo, hi, act, fr: (layer, (lo, hi)[sel][j], 0, 0))
    down = lambda sel: pl.BlockSpec((None, None, D_EXPERT, D_MODEL),
                                    lambda j, s, lo, hi, act, fr: (layer, (lo, hi)[sel][j], 0, 0))
    return pl.pallas_call(
        functools.partial(_expert_kernel, n_q=n_q),
        out_shape=jax.ShapeDtypeStruct((n_rows, 2 * D_MODEL), F32),
        grid_spec=pltpu.PrefetchScalarGridSpec(
            num_scalar_prefetch=5,
            grid=(n_tiles,),
            in_specs=[pl.BlockSpec(memory_space=pl.ANY),
                      up(0), up(1), up(0), up(1), down(0), down(1)],
            out_specs=pl.BlockSpec((te, 2 * D_MODEL), lambda j, s, lo, hi, act, fr: (j, 0)),
            scratch_shapes=[pltpu.VMEM((GATHER_SLOTS, te, D_MODEL), F32)]
            + [pltpu.VMEM((D_MODEL, D_EXPERT), BF16)] * 4 + [pltpu.VMEM((D_EXPERT, D_MODEL), BF16)] * 2
            + [pltpu.SemaphoreType.DMA((GATHER_SLOTS,))],
        ),
        compiler_params=_cparams("arbitrary"),
        name="experts",
    )(src, meta[0, :n_tiles], meta[1, :n_tiles], meta[2, :n_tiles], meta[3, :n_tiles],
      h2, w1, w1, w3, w3, w2, w2)


def _combine_kernel(dest_ref, x_ref, fs_hbm, rcol_ref, mod_ref, g_ref, o_ref, f_buf, sem, *, batch,
                    final):
    j = pl.program_id(0)
    last = pl.num_programs(0) - 1
    slot = j % 2
    rows = batch * TILE_TOK
    idx_rows = rows // 128

    @pl.when(j == 0)
    def _():
        _start_row_gather(_rows_of(dest_ref, 0), fs_hbm, f_buf.at[0], sem.at[0])

    _wait_row_gather(fs_hbm, f_buf.at[slot], sem.at[slot])
    _start_row_gather(_rows_of(dest_ref, jnp.minimum(j + 1, last) * idx_rows), fs_hbm,
                      f_buf.at[1 - slot], sem.at[1 - slot])
    f = _weighted_expert_rows(rcol_ref, f_buf[slot])
    x = x_ref[...] + mod_ref[:, :, 5 * D_MODEL:6 * D_MODEL] * f.reshape(batch, TILE_TOK, D_MODEL)
    if final:
        x = x * lax.rsqrt(jnp.mean(x * x, axis=-1, keepdims=True) + EPS) * g_ref[...]
    o_ref[...] = x

    @pl.when(j == last)
    def _():
        _wait_row_gather(fs_hbm, f_buf.at[1 - slot], sem.at[1 - slot])


def _combine(dest, x1, fs, rcol, mod_sel, final_g, layer, tile_off, n_ctx_tiles, final):
    batch, n_tok, _ = x1.shape
    rows = batch * TILE_TOK
    tok = pl.BlockSpec((batch, TILE_TOK, D_MODEL), lambda j, d: (0, j, 0))
    mod_spec = _mod_spec(batch, layer, n_ctx_tiles, tile_off)
    return pl.pallas_call(
        functools.partial(_combine_kernel, batch=batch, final=final),
        out_shape=jax.ShapeDtypeStruct((batch, n_tok, D_MODEL), F32),
        grid_spec=pltpu.PrefetchScalarGridSpec(
            num_scalar_prefetch=1,
            grid=(n_tok // TILE_TOK,),
            in_specs=[tok, pl.BlockSpec(memory_space=pl.ANY),
                      pl.BlockSpec((rows, ROUTE_LANES), lambda j, d: (j, 0)),
                      pl.BlockSpec(mod_spec.block_shape, lambda j, d: mod_spec.index_map(j)),
                      pl.BlockSpec((1, D_MODEL), lambda j, d: (0, 0))],
            out_specs=tok,
            scratch_shapes=[pltpu.VMEM((2, rows, 2 * D_MODEL), F32), pltpu.SemaphoreType.DMA((2,))],
        ),
        compiler_params=_cparams("arbitrary"),
        name="combine",
    )(dest, x1, fs, rcol, mod_sel, final_g)


def kernel(x, c, ctx, c_ctx, norm1_g, norm2_g, w_mod, b_mod, w_in, lam_re, lam_im, log_dt, b_re,
           b_im, c_re, c_im, d_skip, w_glu, b_glu, w_pool, pool_scale, w_br_a, w_br_b, w_out, w_r1,
           b_r1, w_r2, b_r2, w1, w3, w2, final_g):
    batch, seq, d = x.shape
    assert d == D_MODEL and seq % (GRID_W * 8) == 0 and ctx.shape[1] == CTX_LEN and batch % 8 == 0
    n_ctx_tiles, n_lat_tiles = CTX_LEN // TILE_TOK, seq // TILE_TOK
    n_all_tiles = n_ctx_tiles + n_lat_tiles
    n_chunks = (CTX_LEN + seq) // S5_CHUNK
    mod_rows = ((batch + 1 + 7) // 8) * 8

    cc = jnp.concatenate([c, c_ctx[None, :], jnp.zeros((mod_rows - batch - 1, d), F32)], axis=0)
    mod = _modulation(cc, w_mod, b_mod)
    mod_sel = jnp.stack([jnp.broadcast_to(mod[:, batch:batch + 1], (DEPTH, batch, 6 * d)),
                         mod[:, :batch]], axis=1)[:, :, :, None, :]

    w_in_bf = w_in.astype(BF16)
    w_r = jnp.concatenate([w_r1, w_r2.transpose(0, 2, 1, 3).reshape(DEPTH, d, N_EXPERTS),
                           jnp.zeros((DEPTH, d, ROUTE_LANES - 4 - N_EXPERTS), F32)], axis=2)
    wr_hi = w_r.astype(BF16)
    wr_lo = (w_r - wr_hi.astype(F32)).astype(BF16)
    b_r = jnp.concatenate([b_r1, b_r2.reshape(DEPTH, N_EXPERTS),
                           jnp.zeros((DEPTH, ROUTE_LANES - 4 - N_EXPERTS), F32)], axis=1)
    zeros_p = jnp.zeros((DEPTH, POOL_GROUP, POOL_GROUP), F32)
    wp2 = jnp.stack([
        jnp.concatenate([jnp.concatenate([w_pool[:, 0], zeros_p], axis=2),
                         jnp.concatenate([zeros_p, w_pool[:, 1]], axis=2)], axis=1),
        jnp.concatenate([jnp.concatenate([w_pool[:, 2], zeros_p], axis=2),
                         jnp.concatenate([zeros_p, w_pool[:, 3]], axis=2)], axis=1)],
        axis=1).astype(BF16)
    params = {
        'w_glu': w_glu.astype(BF16), 'b_glu': b_glu.reshape(DEPTH, 1, D_SSM),
        'w_br_a': w_br_a.astype(BF16), 'w_br_b': w_br_b.astype(BF16), 'w_out': w_out.astype(BF16),
        'wr_hi': wr_hi, 'wr_lo': wr_lo, 'b_r': b_r.reshape(DEPTH, 1, ROUTE_LANES),
    }
    norm1 = norm1_g.reshape(DEPTH, 1, d)
    norm2 = norm2_g.reshape(DEPTH, 1, d)
    pscale = pool_scale.reshape(DEPTH, 1, D_POOL)
    dskip = jnp.tile(d_skip.reshape(DEPTH, N_SSM_GROUPS, 1, SSM_GROUP), (1, 1, 1, S5_CHUNK))

    s5_mats = jax.vmap(_s5_matrices)(lam_re, lam_im, log_dt, b_re, b_im, c_re, c_im)
    x_ctx, x_lat, lat_off = ctx, x, 0
    moe = None
    for l in range(DEPTH):
        last = l == DEPTH - 1
        outs = _inproj(x_ctx, x_lat, lat_off, mod_sel, norm1, w_in_bf, l, batch, n_ctx_tiles,
                       n_all_tiles, moe)
        if moe is not None:
            x_ctx = x_lat = outs[0]
            outs = outs[1:]
        u_t, ub, gates = outs
        y_t = _s5(u_t, s5_mats[0][l], s5_mats[1][l], s5_mats[2][l], dskip[l], batch, n_chunks,
                  CTX_LEN // S5_CHUNK)
        yb = _pool(ub, wp2, pscale, l, not last)

        tile_off = n_ctx_tiles if last else 0
        n_tiles = n_lat_tiles if last else n_all_tiles
        x1, h2, route, rcol = _merge(x_ctx, x_lat, lat_off, y_t, yb, gates, mod_sel, norm2, params, l,
                                     batch, tile_off, n_tiles, n_ctx_tiles)
        dest, src, meta = _plan(route[0])
        fs = _experts(src, meta, h2, w1, w3, w2, l)
        if last:
            return _combine(dest, x1, fs, rcol, mod_sel, final_g.reshape(1, d), l, tile_off,
                            n_ctx_tiles, True)
        x_ctx, x_lat, lat_off, moe = x1, x1, n_ctx_tiles, (dest, fs, rcol)
```
